```python
import math
import jax, jax.numpy as jnp
from jax import lax
import numpy as np

D_MODEL = 1024
BATCH = 8
SEQ = 2048
DEPTH = 1
DEC_BATCH = 128
DEC_SEQ = 8
PAST_LEN = 16384
PAGE_SIZE = 128

RET_HEADS = 4
RET_DK = D_MODEL // RET_HEADS
RET_DV = D_MODEL // RET_HEADS
RET_WIDTH = RET_HEADS * RET_DV
RET_CHUNK = 128
ROPE_BASE = 10000.0
CONV_DIM = D_MODEL
CONV_WIDTH = 3
MEM_LEN = 256
MEM_HEADS = 4
MEM_HEAD_DIM = D_MODEL // MEM_HEADS
MEM_WIDTH = MEM_HEADS * MEM_HEAD_DIM
N_BRANCH = 3
EPS = 1e-6
IN_SIZES = (RET_HEADS * RET_DK, RET_HEADS * RET_DK, RET_WIDTH, RET_WIDTH,
            CONV_DIM, CONV_DIM, CONV_DIM, CONV_DIM,
            MEM_WIDTH, MEM_WIDTH, N_BRANCH * D_MODEL)
IN_TOTAL = sum(IN_SIZES)

kernel_name = 'hybrid_retention_shortconv_memxattn_step'


def rms_norm(x, g):
    xf = x.astype(jnp.float32)
    y = xf * lax.rsqrt(jnp.mean(xf * xf, axis=-1, keepdims=True) + EPS)
    return (y * g.astype(jnp.float32)).astype(x.dtype)


def rotary(x, pos):
    half = x.shape[-1] // 2
    inv = ROPE_BASE ** (-jnp.arange(half, dtype=jnp.float32) / half)
    ang = pos.astype(jnp.float32)[:, None] * inv[None, :]
    cos = jnp.cos(ang)[None, :, None, :]
    sin = jnp.sin(ang)[None, :, None, :]
    xf = x.astype(jnp.float32)
    x1, x2 = xf[..., :half], xf[..., half:]
    return jnp.concatenate([x1 * cos - x2 * sin, x1 * sin + x2 * cos], axis=-1)


def retention(q, k, v, state0):
    B, L, H, _ = q.shape
    C = RET_CHUNK if L % RET_CHUNK == 0 else L
    n = L // C
    lg = jnp.log1p(-jnp.exp2(-5.0 - jnp.arange(H, dtype=jnp.float32)))
    idx = jnp.arange(C, dtype=jnp.float32)
    diff = idx[:, None] - idx[None, :]
    inner = jnp.where(diff[None] >= 0, jnp.exp(lg[:, None, None] * jnp.maximum(diff, 0.0)[None]), 0.0)
    q_dec = jnp.exp(lg[None, :] * (idx[:, None] + 1.0))
    k_dec = jnp.exp(lg[None, :] * (C - 1.0 - idx[:, None]))
    chunk_dec = jnp.exp(lg * C)

    def to_chunks(t):
        return t.reshape(B, n, C, H, t.shape[-1]).transpose(1, 0, 2, 3, 4)

    def step(S, blk):
        qc, kc, vc = blk
        s = jnp.einsum('bihd,bjhd->bhij', qc, kc) * inner[None]
        o = (jnp.einsum('bhij,bjhe->bihe', s, vc)
             + jnp.einsum('bihd,bhde->bihe', qc, S) * q_dec[None, :, :, None])
        S = (S * chunk_dec[None, :, None, None]
             + jnp.einsum('bjhd,bjhe->bhde', kc * k_dec[None, :, :, None], vc))
        return S, o

    S, o = lax.scan(step, state0, (to_chunks(q), to_chunks(k), to_chunks(v)))
    o = o.transpose(1, 0, 2, 3, 4).reshape(B, L, H, v.shape[-1])
    return o, S


def memory_kv(mem, mem_norm_g, w_mem_kv):
    B = mem.shape[0]
    kv = rms_norm(mem, mem_norm_g) @ w_mem_kv
    k, v = jnp.split(kv, 2, axis=-1)
    return (k.reshape(B, MEM_LEN, MEM_HEADS, MEM_HEAD_DIM),
            v.reshape(B, MEM_LEN, MEM_HEADS, MEM_HEAD_DIM))


def mixer_layer(x, pos, ret_state, conv_state, mem_k, mem_v,
                norm_g, w_in, b_gate, ret_norm_g, conv_w, conv_b,
                w_ret_o, w_conv_o, w_mem_o, w_out):
    B, L, _ = x.shape
    h = rms_norm(x, norm_g)
    z = h @ w_in
    (rq, rk, rv, rg, cu, cb, cc, cg, mq, mg, gl) = jnp.split(
        z, [int(o) for o in np.cumsum(IN_SIZES)[:-1]], axis=-1)

    q = rotary(rq.reshape(B, L, RET_HEADS, RET_DK), pos)
    k = rotary(rk.reshape(B, L, RET_HEADS, RET_DK), pos) * (RET_DK ** -0.5)
    v = rv.reshape(B, L, RET_HEADS, RET_DV).astype(jnp.float32)
    o, new_ret = retention(q, k, v, ret_state.astype(jnp.float32))
    o = o * lax.rsqrt(jnp.mean(o * o, axis=-1, keepdims=True) + EPS) * ret_norm_g.astype(jnp.float32)
    o = o.reshape(B, L, RET_WIDTH).astype(x.dtype) * jax.nn.silu(rg)
    y_ret = o @ w_ret_o

    pre = cc * cu
    xp = jnp.concatenate([conv_state.astype(pre.dtype), pre], axis=1)
    conv = conv_b + sum(conv_w[t] * xp[:, t:t + L] for t in range(CONV_WIDTH))
    new_conv = xp[:, L:]
    y_conv = (cb * conv * jax.nn.silu(cg)) @ w_conv_o

    qm = mq.reshape(B, L, MEM_HEADS, MEM_HEAD_DIM).astype(jnp.float32)
    s = jnp.einsum('blhd,bmhd->bhlm', qm, mem_k.astype(jnp.float32)) * (MEM_HEAD_DIM ** -0.5)
    p = jax.nn.softmax(s, axis=-1)
    om = jnp.einsum('bhlm,bmhd->blhd', p, mem_v.astype(jnp.float32))
    om = om.reshape(B, L, MEM_WIDTH).astype(x.dtype) * jax.nn.silu(mg)
    y_mem = om @ w_mem_o

    g_ret, g_conv, g_mem = jnp.split(jax.nn.sigmoid(gl + b_gate), N_BRANCH, axis=-1)
    merged = g_ret * y_ret + g_conv * y_conv + g_mem * y_mem
    y = x + merged @ w_out
    return y, new_ret.astype(x.dtype), new_conv


def setup_inputs(seed: int = 0) -> dict:
    key = jax.random.key(seed)
    ks = jax.random.split(key, 24)
    f32 = jnp.float32
    nrm = lambda k, s, sc: jax.random.normal(k, s, f32) * sc
    return {
        'x_prompt': nrm(ks[0], (BATCH, SEQ, D_MODEL), 1.0),
        'x_sample': nrm(ks[1], (DEC_BATCH, DEC_SEQ, D_MODEL), 1.0),
        'state_ret': nrm(ks[2], (DEPTH, DEC_BATCH, RET_HEADS, RET_DK, RET_DV), 0.5),
        'state_conv': nrm(ks[3], (DEPTH, DEC_BATCH, CONV_WIDTH - 1, CONV_DIM), 1.0),
        'cache_mem_k': nrm(ks[4], (DEPTH, DEC_BATCH, MEM_LEN, MEM_HEADS, MEM_HEAD_DIM), 1.0),
        'cache_mem_v': nrm(ks[5], (DEPTH, DEC_BATCH, MEM_LEN, MEM_HEADS, MEM_HEAD_DIM), 1.0),
        'mem_prompt': nrm(ks[6], (BATCH, MEM_LEN, D_MODEL), 1.0),
        'norm_g': 1.0 + nrm(ks[7], (DEPTH, D_MODEL), 0.02),
        'w_in': nrm(ks[8], (DEPTH, D_MODEL, IN_TOTAL), D_MODEL ** -0.5),
        'b_gate': nrm(ks[9], (DEPTH, N_BRANCH * D_MODEL), 0.02),
        'ret_norm_g': 1.0 + nrm(ks[10], (DEPTH, RET_HEADS, RET_DV), 0.02),
        'conv_w': nrm(ks[11], (DEPTH, CONV_WIDTH, CONV_DIM), CONV_WIDTH ** -0.5),
        'conv_b': nrm(ks[12], (DEPTH, CONV_DIM), 0.02),
        'w_ret_o': nrm(ks[13], (DEPTH, RET_WIDTH, D_MODEL), RET_WIDTH ** -0.5),
        'w_conv_o': nrm(ks[14], (DEPTH, CONV_DIM, D_MODEL), CONV_DIM ** -0.5),
        'w_mem_o': nrm(ks[15], (DEPTH, MEM_WIDTH, D_MODEL), MEM_WIDTH ** -0.5),
        'w_out': nrm(ks[16], (DEPTH, D_MODEL, D_MODEL), D_MODEL ** -0.5),
        'mem_norm_g': 1.0 + nrm(ks[17], (DEPTH, D_MODEL), 0.02),
        'w_mem_kv': nrm(ks[18], (DEPTH, D_MODEL, 2 * MEM_WIDTH), D_MODEL ** -0.5),
        'final_norm_g': 1.0 + nrm(ks[19], (D_MODEL,), 0.02),
    }


def reference(x_prompt, x_sample, state_ret, state_conv, cache_mem_k, cache_mem_v, mem_prompt,
              norm_g, w_in, b_gate, ret_norm_g, conv_w, conv_b, w_ret_o, w_conv_o, w_mem_o,
              w_out, mem_norm_g, w_mem_kv, final_norm_g):
    Bp, Lp, _ = x_prompt.shape
    Ls = x_sample.shape[1]
    pos_p = jnp.arange(Lp, dtype=jnp.int32)
    pos_s = PAST_LEN + jnp.arange(Ls, dtype=jnp.int32)
    hp, hs = x_prompt, x_sample
    ret_p, ret_s, conv_p, conv_s, mk_list, mv_list = [], [], [], [], [], []
    for l in range(DEPTH):
        mk_p, mv_p = memory_kv(mem_prompt, mem_norm_g[l], w_mem_kv[l])
        zr = jnp.zeros((Bp, RET_HEADS, RET_DK, RET_DV), x_prompt.dtype)
        zc = jnp.zeros((Bp, CONV_WIDTH - 1, CONV_DIM), x_prompt.dtype)
        hp, rp, cp = mixer_layer(hp, pos_p, zr, zc, mk_p, mv_p,
                                 norm_g[l], w_in[l], b_gate[l], ret_norm_g[l], conv_w[l], conv_b[l],
                                 w_ret_o[l], w_conv_o[l], w_mem_o[l], w_out[l])
        hs, rs, cs = mixer_layer(hs, pos_s, state_ret[l], state_conv[l], cache_mem_k[l], cache_mem_v[l],
                                 norm_g[l], w_in[l], b_gate[l], ret_norm_g[l], conv_w[l], conv_b[l],
                                 w_ret_o[l], w_conv_o[l], w_mem_o[l], w_out[l])
        ret_p.append(rp); ret_s.append(rs); conv_p.append(cp); conv_s.append(cs)
        mk_list.append(mk_p); mv_list.append(mv_p)
    y_prompt = rms_norm(hp, final_norm_g)
    y_sample = rms_norm(hs, final_norm_g)
    new_ret_prompt = jnp.stack(ret_p)
    new_ret_sample = jnp.stack(ret_s)
    new_conv_prompt = jnp.stack(conv_p)
    new_conv_sample = jnp.stack(conv_s)
    new_mem_k_prompt = jnp.stack(mk_list)
    new_mem_v_prompt = jnp.stack(mv_list)
    return (y_prompt, y_sample, new_ret_prompt, new_ret_sample, new_conv_prompt, new_conv_sample,
            new_mem_k_prompt, new_mem_v_prompt)
```

```python
import functools

import jax
import jax.numpy as jnp
from jax import lax
from jax.experimental import pallas as pl
from jax.experimental.pallas import tpu as pltpu

F32 = jnp.float32
BF16 = jnp.bfloat16

D_MODEL = 1024
HEADS = 4
HEAD_DIM = D_MODEL // HEADS
ROT_HALF = HEAD_DIM // 2
RET_CHUNK = 128
ROPE_BASE = 10000.0
CONV_WIDTH = 3
MEM_LEN = 256
N_BRANCH = 3
EPS = 1e-6
PAST_LEN = 16384
QK_SCALE = HEAD_DIM ** -0.5

OFF_RQ, OFF_RK, OFF_RV, OFF_RG = 0, 1024, 2048, 3072
OFF_CU, OFF_CB, OFF_CC, OFF_CG = 4096, 5120, 6144, 7168
OFF_MQ, OFF_MG = 8192, 9216
OFF_GATE = 10240
IN_TOTAL = OFF_GATE + N_BRANCH * D_MODEL

SUBLANES = 8
VMEM_LIMIT_BYTES = 60 * 1024 * 1024

PROMPT_TILE = 256
SAMPLE_GROUP = 2
SAMPLE_TAIL_TILE = 256


def _bf(x):
    return x.astype(BF16)


def _dot(a, b):
    return jnp.dot(a, b, preferred_element_type=F32)


def _dot_nt(a, b):
    return lax.dot_general(a, b, (((1,), (1,)), ((), ())), preferred_element_type=F32)


def _dot_tn(a, b):
    return lax.dot_general(a, b, (((0,), (0,)), ((), ())), preferred_element_type=F32)


def _rms(x, g):
    return x * lax.rsqrt(jnp.mean(x * x, axis=-1, keepdims=True) + EPS) * g


def _silu(x):
    return x * jax.nn.sigmoid(x)


def _rope(x, cos, sin):
    x1 = x[:, :ROT_HALF]
    x2 = x[:, ROT_HALF:]
    return jnp.concatenate([x1 * cos - x2 * sin, x1 * sin + x2 * cos], axis=-1)


def _fill_decay(dec_ref, inner_s, qdec_s, kdec_s, chunk):
    i = lax.broadcasted_iota(jnp.int32, (chunk, chunk), 0)
    j = lax.broadcasted_iota(jnp.int32, (chunk, chunk), 1)
    diff = (i - j).astype(F32)
    r = lax.broadcasted_iota(jnp.int32, (chunk, HEAD_DIM), 0).astype(F32)
    for hd in range(HEADS):
        lg = dec_ref[hd]
        inner_s[hd] = jnp.where(diff >= 0, jnp.exp(lg * jnp.maximum(diff, 0.0)), 0.0)
        qdec_s[hd] = jnp.exp(lg * (r + 1.0))
        kdec_s[hd] = jnp.exp(lg * (chunk - 1.0 - r))


def _retention_chunk(q, k, v, state, inner, qdec, kdec, chunk_dec):
    qb, vb = _bf(q), _bf(v)
    s = _dot_nt(qb, _bf(k)) * inner
    o = _dot(_bf(s), vb) + _dot(qb, _bf(state)) * qdec
    new_state = state * chunk_dec + _dot_tn(_bf(k * kdec), vb)
    return o, new_state


def _conv_branch(cu, cb, cc, cg, cbuf, cw_ref, cbias_ref, rows):
    pre = cc * cu
    cbuf[SUBLANES:SUBLANES + rows, :] = pre
    conv = (cbias_ref[...]
            + cw_ref[0:1, :] * cbuf[SUBLANES - 2:SUBLANES - 2 + rows, :]
            + cw_ref[1:2, :] * cbuf[SUBLANES - 1:SUBLANES - 1 + rows, :]
            + cw_ref[2:3, :] * pre)
    return pre, cb * conv * _silu(cg)


def _mem_head(q, mk, mv):
    s = _dot_nt(_bf(q), mk) * QK_SCALE
    e = jnp.exp(s - jnp.max(s, axis=-1, keepdims=True))
    p = e / jnp.sum(e, axis=-1, keepdims=True)
    return _dot(_bf(p), mv)


def _gate(h, win_ref, bg_ref, branch):
    lo = branch * D_MODEL
    return jax.nn.sigmoid(_dot(h, win_ref[:, OFF_GATE + lo:OFF_GATE + lo + D_MODEL]) + bg_ref[:, lo:lo + D_MODEL])


def _memkv_body(mem_ref, g_ref, w_ref, k_ref, v_ref, kb_ref, vb_ref):
    kv = _dot(_bf(_rms(mem_ref[...], g_ref[...])), w_ref[...])
    k = kv[:, :D_MODEL]
    v = kv[:, D_MODEL:]
    k_ref[...] = k
    v_ref[...] = v
    kb_ref[...] = _bf(k)
    vb_ref[...] = _bf(v)


def _memory_kv(mem, g, w_bf):
    nb = mem.shape[0]
    blk = pl.BlockSpec((None, MEM_LEN, D_MODEL), lambda b: (b, 0, 0))
    return pl.pallas_call(
        _memkv_body,
        grid=(nb,),
        in_specs=[blk,
                  pl.BlockSpec((1, D_MODEL), lambda b: (0, 0)),
                  pl.BlockSpec((D_MODEL, 2 * D_MODEL), lambda b: (0, 0))],
        out_specs=[blk, blk, blk, blk],
        out_shape=[jax.ShapeDtypeStruct(mem.shape, F32), jax.ShapeDtypeStruct(mem.shape, F32),
                   jax.ShapeDtypeStruct(mem.shape, BF16), jax.ShapeDtypeStruct(mem.shape, BF16)],
        compiler_params=pltpu.CompilerParams(dimension_semantics=("arbitrary",),
                                             vmem_limit_bytes=VMEM_LIMIT_BYTES),
        name="memory_kv",
    )(mem, g, w_bf)


def _prompt_body(dec_ref, x_ref, cos_ref, sin_ref, mk_ref, mv_ref, ng_ref, win_ref, bg_ref, rng_ref,
                 cw_ref, cbias_ref, wro_ref, wco_ref, wmo_ref, wout_ref, fg_ref,
                 y_ref, state_ref, nconv_ref,
                 cbuf, inner_s, qdec_s, kdec_s, act_s):
    b = pl.program_id(0)
    c = pl.program_id(1)
    tile = x_ref.shape[0]

    @pl.when(jnp.logical_and(b == 0, c == 0))
    def _():
        _fill_decay(dec_ref, inner_s, qdec_s, kdec_s, RET_CHUNK)

    @pl.when(c == 0)
    def _():
        state_ref[...] = jnp.zeros(state_ref.shape, F32)
        cbuf[0:SUBLANES, :] = jnp.zeros((SUBLANES, D_MODEL), F32)

    x = x_ref[...]
    h = _bf(_rms(x, ng_ref[...]))
    cos = cos_ref[...]
    sin = sin_ref[...]

    def proj(off, width=HEAD_DIM):
        return _dot(h, win_ref[:, off:off + width])

    for hd in range(HEADS):
        lo = hd * HEAD_DIM
        q = _rope(proj(OFF_RQ + lo), cos, sin)
        k = _rope(proj(OFF_RK + lo), cos, sin) * QK_SCALE
        v = proj(OFF_RV + lo)
        outs = []
        for j in range(tile // RET_CHUNK):
            rows = slice(j * RET_CHUNK, (j + 1) * RET_CHUNK)
            o, new_state = _retention_chunk(q[rows], k[rows], v[rows], state_ref[hd],
                                            inner_s[hd], qdec_s[hd], kdec_s[hd], dec_ref[HEADS + hd])
            state_ref[hd] = new_state
            outs.append(o)
        o = jnp.concatenate(outs, axis=0)
        o = _rms(o, rng_ref[:, lo:lo + HEAD_DIM])
        act_s[:, lo:lo + HEAD_DIM] = _bf(o * _silu(proj(OFF_RG + lo)))
    merged = _gate(h, win_ref, bg_ref, 0) * _dot(act_s[...], wro_ref[...])

    pre, a_conv = _conv_branch(proj(OFF_CU, D_MODEL), proj(OFF_CB, D_MODEL), proj(OFF_CC, D_MODEL),
                               proj(OFF_CG, D_MODEL), cbuf, cw_ref, cbias_ref, tile)
    cbuf[0:SUBLANES, :] = pre[tile - SUBLANES:, :]
    nconv_ref[...] = pre[tile - (CONV_WIDTH - 1):, :]
    merged = merged + _gate(h, win_ref, bg_ref, 1) * _dot(_bf(a_conv), wco_ref[...])

    for hd in range(HEADS):
        lo = hd * HEAD_DIM
        om = _mem_head(proj(OFF_MQ + lo), mk_ref[:, lo:lo + HEAD_DIM], mv_ref[:, lo:lo + HEAD_DIM])
        act_s[:, lo:lo + HEAD_DIM] = _bf(om * _silu(proj(OFF_MG + lo)))
    merged = merged + _gate(h, win_ref, bg_ref, 2) * _dot(act_s[...], wmo_ref[...])

    y = x + _dot(_bf(merged), wout_ref[...])
    y_ref[...] = _rms(y, fg_ref[...])


def _resident(shape):
    zeros = (0,) * len(shape)
    return pl.BlockSpec(shape, lambda *_: zeros, pipeline_mode=pl.Buffered(1))


def _prompt_layer(x, dec, cos, sin, mk_bf, mv_bf, ng, win_bf, bg, rng, cw, cbias, wro, wco, wmo, wout, fg):
    nb, seq, _ = x.shape
    tile = PROMPT_TILE
    tok = pl.BlockSpec((None, tile, D_MODEL), lambda b, c: (b, c, 0))
    rot = pl.BlockSpec((tile, ROT_HALF), lambda b, c: (c, 0))
    mem = pl.BlockSpec((None, MEM_LEN, D_MODEL), lambda b, c: (b, 0, 0))
    return pl.pallas_call(
        _prompt_body,
        grid=(nb, seq // tile),
        in_specs=[pl.BlockSpec(memory_space=pltpu.SMEM), tok, rot, rot, mem, mem,
                  _resident(ng.shape), _resident(win_bf.shape), _resident(bg.shape), _resident(rng.shape),
                  _resident(cw.shape), _resident(cbias.shape), _resident(wro.shape), _resident(wco.shape),
                  _resident(wmo.shape), _resident(wout.shape), _resident(fg.shape)],
        out_specs=[tok,
                   pl.BlockSpec((None, HEADS, HEAD_DIM, HEAD_DIM), lambda b, c: (b, 0, 0, 0)),
                   pl.BlockSpec((None, CONV_WIDTH - 1, D_MODEL), lambda b, c: (b, 0, 0))],
        out_shape=[jax.ShapeDtypeStruct(x.shape, F32),
                   jax.ShapeDtypeStruct((nb, HEADS, HEAD_DIM, HEAD_DIM), F32),
                   jax.ShapeDtypeStruct((nb, CONV_WIDTH - 1, D_MODEL), F32)],
        scratch_shapes=[pltpu.VMEM((SUBLANES + tile, D_MODEL), F32),
                        pltpu.VMEM((HEADS, RET_CHUNK, RET_CHUNK), F32),
                        pltpu.VMEM((HEADS, RET_CHUNK, HEAD_DIM), F32),
                        pltpu.VMEM((HEADS, RET_CHUNK, HEAD_DIM), F32),
                        pltpu.VMEM((tile, D_MODEL), BF16)],
        compiler_params=pltpu.CompilerParams(dimension_semantics=("arbitrary", "arbitrary"),
                                             vmem_limit_bytes=VMEM_LIMIT_BYTES),
        name="prompt_layer",
    )(dec, x, cos, sin, mk_bf, mv_bf, ng, win_bf, bg, rng, cw, cbias, wro, wco, wmo, wout, fg)


def _sample_proj_body(x_ref, ng_ref, w_ref, z_ref, h_s):
    @pl.when(pl.program_id(0) == 0)
    def _():
        h_s[...] = _bf(_rms(x_ref[...], ng_ref[...]))

    z_ref[...] = _dot(h_s[...], w_ref[...])


def _sample_proj(x2d, ng, win_bf):
    ntok = x2d.shape[0]
    blk = D_MODEL
    return pl.pallas_call(
        _sample_proj_body,
        grid=(OFF_GATE // blk,),
        in_specs=[pl.BlockSpec((ntok, D_MODEL), lambda n: (0, 0)),
                  pl.BlockSpec((1, D_MODEL), lambda n: (0, 0)),
                  pl.BlockSpec((D_MODEL, blk), lambda n: (0, n))],
        out_specs=pl.BlockSpec((ntok, blk), lambda n: (0, n)),
        out_shape=jax.ShapeDtypeStruct((ntok, OFF_GATE), F32),
        scratch_shapes=[pltpu.VMEM((ntok, D_MODEL), BF16)],
        compiler_params=pltpu.CompilerParams(dimension_semantics=("arbitrary",),
                                             vmem_limit_bytes=VMEM_LIMIT_BYTES),
        name="sample_proj",
    )(x2d, ng, win_bf)


def _sample_branch_body(dec_ref, z_ref, cos_ref, sin_ref, state_ref, cstate_ref, mk_ref, mv_ref, rng_ref,
                        cw_ref, cbias_ref,
                        aret_ref, aconv_ref, amem_ref, nstate_ref, nconv_ref,
                        cbuf, inner_s, qdec_s, kdec_s):
    group, seq, _ = z_ref.shape

    @pl.when(pl.program_id(0) == 0)
    def _():
        _fill_decay(dec_ref, inner_s, qdec_s, kdec_s, seq)

    cos = cos_ref[...]
    sin = sin_ref[...]
    for g in range(group):
        def col(off, width=HEAD_DIM):
            return z_ref[g, :, off:off + width]

        for hd in range(HEADS):
            lo = hd * HEAD_DIM
            q = _rope(col(OFF_RQ + lo), cos, sin)
            k = _rope(col(OFF_RK + lo), cos, sin) * QK_SCALE
            o, new_state = _retention_chunk(q, k, col(OFF_RV + lo), state_ref[g, hd],
                                            inner_s[hd], qdec_s[hd], kdec_s[hd], dec_ref[HEADS + hd])
            nstate_ref[g, hd] = new_state
            o = _rms(o, rng_ref[:, lo:lo + HEAD_DIM])
            aret_ref[g, :, lo:lo + HEAD_DIM] = _bf(o * _silu(col(OFF_RG + lo)))

        cbuf[0:SUBLANES, :] = jnp.zeros((SUBLANES, D_MODEL), F32)
        cbuf[SUBLANES - (CONV_WIDTH - 1):SUBLANES, :] = cstate_ref[g]
        pre, a_conv = _conv_branch(col(OFF_CU, D_MODEL), col(OFF_CB, D_MODEL), col(OFF_CC, D_MODEL),
                                   col(OFF_CG, D_MODEL), cbuf, cw_ref, cbias_ref, seq)
        nconv_ref[g] = pre[seq - (CONV_WIDTH - 1):, :]
        aconv_ref[g] = _bf(a_conv)

        for hd in range(HEADS):
            lo = hd * HEAD_DIM
            om = _mem_head(col(OFF_MQ + lo), _bf(mk_ref[g, :, lo:lo + HEAD_DIM]), _bf(mv_ref[g, :, lo:lo + HEAD_DIM]))
            amem_ref[g, :, lo:lo + HEAD_DIM] = _bf(om * _silu(col(OFF_MG + lo)))


def _sample_branches(z, dec, cos, sin, state, cstate, mk, mv, rng, cw, cbias):
    nb, seq, _ = z.shape
    grp = SAMPLE_GROUP

    def per_req(*tail):
        zeros = (0,) * len(tail)
        return pl.BlockSpec((grp,) + tail, lambda i: (i,) + zeros)

    def const(shape):
        zeros = (0,) * len(shape)
        return pl.BlockSpec(shape, lambda i: zeros)

    act = jax.ShapeDtypeStruct((nb, seq, D_MODEL), BF16)
    return pl.pallas_call(
        _sample_branch_body,
        grid=(nb // grp,),
        in_specs=[pl.BlockSpec(memory_space=pltpu.SMEM),
                  per_req(seq, OFF_GATE), const(cos.shape), const(sin.shape),
                  per_req(HEADS, HEAD_DIM, HEAD_DIM), per_req(CONV_WIDTH - 1, D_MODEL),
                  per_req(MEM_LEN, D_MODEL), per_req(MEM_LEN, D_MODEL),
                  const(rng.shape), const(cw.shape), const(cbias.shape)],
        out_specs=[per_req(seq, D_MODEL), per_req(seq, D_MODEL), per_req(seq, D_MODEL),
                   per_req(HEADS, HEAD_DIM, HEAD_DIM), per_req(CONV_WIDTH - 1, D_MODEL)],
        out_shape=[act, act, act,
                   jax.ShapeDtypeStruct(state.shape, F32),
                   jax.ShapeDtypeStruct(cstate.shape, F32)],
        scratch_shapes=[pltpu.VMEM((SUBLANES + seq, D_MODEL), F32),
                        pltpu.VMEM((HEADS, seq, seq), F32),
                        pltpu.VMEM((HEADS, seq, HEAD_DIM), F32),
                        pltpu.VMEM((HEADS, seq, HEAD_DIM), F32)],
        compiler_params=pltpu.CompilerParams(dimension_semantics=("arbitrary",),
                                             vmem_limit_bytes=VMEM_LIMIT_BYTES),
        name="sample_branches",
    )(dec, z, cos, sin, state, cstate, mk, mv, rng, cw, cbias)


def _sample_tail_body(x_ref, aret_ref, aconv_ref, amem_ref, ng_ref, wg0_ref, wg1_ref, wg2_ref, bg_ref,
                      wro_ref, wco_ref, wmo_ref, wout_ref, fg_ref, y_ref):
    x = x_ref[...]
    h = _bf(_rms(x, ng_ref[...]))

    def gate(branch, wg_ref):
        lo = branch * D_MODEL
        return jax.nn.sigmoid(_dot(h, wg_ref[...]) + bg_ref[:, lo:lo + D_MODEL])

    merged = gate(0, wg0_ref) * _dot(aret_ref[...], wro_ref[...])
    merged = merged + gate(1, wg1_ref) * _dot(aconv_ref[...], wco_ref[...])
    merged = merged + gate(2, wg2_ref) * _dot(amem_ref[...], wmo_ref[...])
    y = x + _dot(_bf(merged), wout_ref[...])
    y_ref[...] = _rms(y, fg_ref[...])


def _sample_tail(x2d, aret, aconv, amem, ng, win_bf, bg, wro, wco, wmo, wout, fg):
    ntok = x2d.shape[0]
    tile = SAMPLE_TAIL_TILE
    tok = pl.BlockSpec((tile, D_MODEL), lambda i: (i, 0))

    def gate_cols(branch):
        blk = OFF_GATE // D_MODEL + branch
        return pl.BlockSpec((D_MODEL, D_MODEL), lambda i: (0, blk), pipeline_mode=pl.Buffered(1))

    return pl.pallas_call(
        _sample_tail_body,
        grid=(ntok // tile,),
        in_specs=[tok, tok, tok, tok,
                  _resident(ng.shape), gate_cols(0), gate_cols(1), gate_cols(2), _resident(bg.shape),
                  _resident(wro.shape), _resident(wco.shape), _resident(wmo.shape), _resident(wout.shape),
                  _resident(fg.shape)],
        out_specs=tok,
        out_shape=jax.ShapeDtypeStruct(x2d.shape, F32),
        compiler_params=pltpu.CompilerParams(dimension_semantics=("arbitrary",),
                                             vmem_limit_bytes=VMEM_LIMIT_BYTES),
        name="sample_tail",
    )(x2d, aret, aconv, amem, ng, win_bf, win_bf, win_bf, bg, wro, wco, wmo, wout, fg)


def _rope_tables(pos):
    inv = ROPE_BASE ** (-jnp.arange(ROT_HALF, dtype=F32) / ROT_HALF)
    ang = pos.astype(F32)[:, None] * inv[None, :]
    return jnp.cos(ang), jnp.sin(ang)


def _decay_scalars(chunk):
    lg = jnp.log1p(-jnp.exp2(-5.0 - jnp.arange(HEADS, dtype=F32)))
    return jnp.concatenate([lg, jnp.exp(lg * chunk)])


def kernel(x_prompt, x_sample, state_ret, state_conv, cache_mem_k, cache_mem_v, mem_prompt, norm_g, w_in, b_gate, ret_norm_g, conv_w, conv_b, w_ret_o, w_conv_o, w_mem_o, w_out, mem_norm_g, w_mem_kv, final_norm_g):
    assert norm_g.shape[0] == 1, "single-layer trunk"
    nbp, seq_p, _ = x_prompt.shape
    nbs, seq_s, _ = x_sample.shape
    assert seq_p % PROMPT_TILE == 0 and PROMPT_TILE % RET_CHUNK == 0
    assert seq_s % RET_CHUNK != 0 and seq_s == SUBLANES and nbs % SAMPLE_GROUP == 0

    ng = norm_g[0][None, :]
    win_bf = _bf(w_in[0])
    bg = b_gate[0][None, :]
    rng = ret_norm_g[0].reshape(1, D_MODEL)
    cw = conv_w[0]
    cbias = conv_b[0][None, :]
    wro, wco, wmo, wout = _bf(w_ret_o[0]), _bf(w_conv_o[0]), _bf(w_mem_o[0]), _bf(w_out[0])
    fg = final_norm_g[None, :]

    cos_p, sin_p = _rope_tables(jnp.arange(seq_p, dtype=jnp.int32))
    cos_s, sin_s = _rope_tables(PAST_LEN + jnp.arange(seq_s, dtype=jnp.int32))

    mk, mv, mk_bf, mv_bf = _memory_kv(mem_prompt, mem_norm_g[0][None, :], _bf(w_mem_kv[0]))
    y_prompt, ret_p, conv_p = _prompt_layer(x_prompt, _decay_scalars(RET_CHUNK), cos_p, sin_p, mk_bf, mv_bf,
                                            ng, win_bf, bg, rng, cw, cbias, wro, wco, wmo, wout, fg)

    xs2d = x_sample.reshape(nbs * seq_s, D_MODEL)
    z = _sample_proj(xs2d, ng, win_bf).reshape(nbs, seq_s, OFF_GATE)
    aret, aconv, amem, ret_s, conv_s = _sample_branches(
        z, _decay_scalars(seq_s), cos_s, sin_s, state_ret[0], state_conv[0],
        cache_mem_k[0].reshape(nbs, MEM_LEN, D_MODEL), cache_mem_v[0].reshape(nbs, MEM_LEN, D_MODEL),
        rng, cw, cbias)
    flat = lambda a: a.reshape(nbs * seq_s, D_MODEL)
    y_sample = _sample_tail(xs2d, flat(aret), flat(aconv), flat(amem), ng, win_bf, bg,
                            wro, wco, wmo, wout, fg).reshape(x_sample.shape)

    mem_shape = (1, nbp, MEM_LEN, HEADS, HEAD_DIM)
    return (y_prompt, y_sample, ret_p[None], ret_s[None], conv_p[None], conv_s[None],
            mk.reshape(mem_shape), mv.reshape(mem_shape))
```

```python
import functools

import jax
import jax.numpy as jnp
from jax import lax
from jax.experimental import pallas as pl
from jax.experimental.pallas import tpu as pltpu

F32 = jnp.float32
BF16 = jnp.bfloat16

D_MODEL = 1024
HEADS = 4
HEAD_DIM = D_MODEL // HEADS
ROT_HALF = HEAD_DIM // 2
RET_CHUNK = 128
ROPE_BASE = 10000.0
CONV_WIDTH = 3
MEM_LEN = 256
N_BRANCH = 3
EPS = 1e-6
PAST_LEN = 16384
QK_SCALE = HEAD_DIM ** -0.5

OFF_RQ, OFF_RK, OFF_RV, OFF_RG = 0, 1024, 2048, 3072
OFF_CU, OFF_CB, OFF_CC, OFF_CG = 4096, 5120, 6144, 7168
OFF_MQ, OFF_MG = 8192, 9216
OFF_GATE = 10240
IN_TOTAL = OFF_GATE + N_BRANCH * D_MODEL

SUBLANES = 8
VMEM_LIMIT_BYTES = 60 * 1024 * 1024

PROMPT_TILE = 256
SAMPLE_GROUP = 2
SAMPLE_TAIL_TILE = 256


def _bf(x):
    return x.astype(BF16)


def _dot(a, b):
    return jnp.dot(a, b, preferred_element_type=F32)


def _dot_nt(a, b):
    return lax.dot_general(a, b, (((1,), (1,)), ((), ())), preferred_element_type=F32)


def _dot_tn(a, b):
    return lax.dot_general(a, b, (((0,), (0,)), ((), ())), preferred_element_type=F32)


def _rms(x, g):
    return x * lax.rsqrt(jnp.mean(x * x, axis=-1, keepdims=True) + EPS) * g


def _silu(x):
    return x * jax.nn.sigmoid(x)


def _rope(x, cos, sin):
    x1 = x[:, :ROT_HALF]
    x2 = x[:, ROT_HALF:]
    return jnp.concatenate([x1 * cos - x2 * sin, x1 * sin + x2 * cos], axis=-1)


def _fill_decay(dec_ref, inner_s, qdec_s, kdec_s, chunk):
    i = lax.broadcasted_iota(jnp.int32, (chunk, chunk), 0)
    j = lax.broadcasted_iota(jnp.int32, (chunk, chunk), 1)
    diff = (i - j).astype(F32)
    r = lax.broadcasted_iota(jnp.int32, (chunk, HEAD_DIM), 0).astype(F32)
    for hd in range(HEADS):
        lg = dec_ref[hd]
        inner_s[hd] = jnp.where(diff >= 0, jnp.exp(lg * jnp.maximum(diff, 0.0)), 0.0)
        qdec_s[hd] = jnp.exp(lg * (r + 1.0))
        kdec_s[hd] = jnp.exp(lg * (chunk - 1.0 - r))


def _retention_chunk(q, k, v, state, inner, qdec, kdec, chunk_dec):
    qb, vb = _bf(q), _bf(v)
    s = _dot_nt(qb, _bf(k)) * inner
    o = _dot(_bf(s), vb) + _dot(qb, _bf(state)) * qdec
    new_state = state * chunk_dec + _dot_tn(_bf(k * kdec), vb)
    return o, new_state


def _conv_branch(cu, cb, cc, cg, cbuf, cw_ref, cbias_ref, rows):
    pre = cc * cu
    cbuf[SUBLANES:SUBLANES + rows, :] = pre
    conv = (cbias_ref[...]
            + cw_ref[0:1, :] * cbuf[SUBLANES - 2:SUBLANES - 2 + rows, :]
            + cw_ref[1:2, :] * cbuf[SUBLANES - 1:SUBLANES - 1 + rows, :]
            + cw_ref[2:3, :] * pre)
    return pre, cb * conv * _silu(cg)


def _mem_head(q, mk, mv):
    s = _dot_nt(_bf(q), mk) * QK_SCALE
    e = jnp.exp(s - jnp.max(s, axis=-1, keepdims=True))
    p = e / jnp.sum(e, axis=-1, keepdims=True)
    return _dot(_bf(p), mv)


def _gate(h, win_ref, bg_ref, branch):
    lo = branch * D_MODEL
    return jax.nn.sigmoid(_dot(h, win_ref[:, OFF_GATE + lo:OFF_GATE + lo + D_MODEL]) + bg_ref[:, lo:lo + D_MODEL])


def _memkv_body(mem_ref, g_ref, w_ref, k_ref, v_ref, kb_ref, vb_ref):
    kv = _dot(_bf(_rms(mem_ref[...], g_ref[...])), w_ref[...])
    k = kv[:, :D_MODEL]
    v = kv[:, D_MODEL:]
    for hd in range(HEADS):
        lo = hd * HEAD_DIM
        k_ref[:, hd, :] = k[:, lo:lo + HEAD_DIM]
        v_ref[:, hd, :] = v[:, lo:lo + HEAD_DIM]
    kb_ref[...] = _bf(k)
    vb_ref[...] = _bf(v)


def _memory_kv(mem, g, w_bf):
    nb = mem.shape[0]
    blk = pl.BlockSpec((None, MEM_LEN, D_MODEL), lambda b: (b, 0, 0))
    blk_heads = pl.BlockSpec((None, MEM_LEN, HEADS, HEAD_DIM), lambda b: (b, 0, 0, 0))
    heads_shape = jax.ShapeDtypeStruct((nb, MEM_LEN, HEADS, HEAD_DIM), F32)
    return pl.pallas_call(
        _memkv_body,
        grid=(nb,),
        in_specs=[blk,
                  pl.BlockSpec((1, D_MODEL), lambda b: (0, 0)),
                  pl.BlockSpec((D_MODEL, 2 * D_MODEL), lambda b: (0, 0))],
        out_specs=[blk_heads, blk_heads, blk, blk],
        out_shape=[heads_shape, heads_shape,
                   jax.ShapeDtypeStruct(mem.shape, BF16), jax.ShapeDtypeStruct(mem.shape, BF16)],
        compiler_params=pltpu.CompilerParams(dimension_semantics=("arbitrary",),
                                             vmem_limit_bytes=VMEM_LIMIT_BYTES),
        name="memory_kv",
    )(mem, g, w_bf)


def _prompt_body(dec_ref, x_ref, cos_ref, sin_ref, mk_ref, mv_ref, ng_ref, win_ref, bg_ref, rng_ref,
                 cw_ref, cbias_ref, wro_ref, wco_ref, wmo_ref, wout_ref, fg_ref,
                 y_ref, state_ref, nconv_ref,
                 cbuf, inner_s, qdec_s, kdec_s, act_s):
    b = pl.program_id(0)
    c = pl.program_id(1)
    tile = x_ref.shape[0]

    @pl.when(jnp.logical_and(b == 0, c == 0))
    def _():
        _fill_decay(dec_ref, inner_s, qdec_s, kdec_s, RET_CHUNK)

    @pl.when(c == 0)
    def _():
        state_ref[...] = jnp.zeros(state_ref.shape, F32)
        cbuf[0:SUBLANES, :] = jnp.zeros((SUBLANES, D_MODEL), F32)

    x = x_ref[...]
    h = _bf(_rms(x, ng_ref[...]))
    cos = cos_ref[...]
    sin = sin_ref[...]

    def proj(off, width=HEAD_DIM):
        return _dot(h, win_ref[:, off:off + width])

    for hd in range(HEADS):
        lo = hd * HEAD_DIM
        q = _rope(proj(OFF_RQ + lo), cos, sin)
        k = _rope(proj(OFF_RK + lo), cos, sin) * QK_SCALE
        v = proj(OFF_RV + lo)
        outs = []
        for j in range(tile // RET_CHUNK):
            rows = slice(j * RET_CHUNK, (j + 1) * RET_CHUNK)
            o, new_state = _retention_chunk(q[rows], k[rows], v[rows], state_ref[hd],
                                            inner_s[hd], qdec_s[hd], kdec_s[hd], dec_ref[HEADS + hd])
            state_ref[hd] = new_state
            outs.append(o)
        o = jnp.concatenate(outs, axis=0)
        o = _rms(o, rng_ref[:, lo:lo + HEAD_DIM])
        act_s[:, lo:lo + HEAD_DIM] = _bf(o * _silu(proj(OFF_RG + lo)))
    merged = _gate(h, win_ref, bg_ref, 0) * _dot(act_s[...], wro_ref[...])

    pre, a_conv = _conv_branch(proj(OFF_CU, D_MODEL), proj(OFF_CB, D_MODEL), proj(OFF_CC, D_MODEL),
                               proj(OFF_CG, D_MODEL), cbuf, cw_ref, cbias_ref, tile)
    cbuf[0:SUBLANES, :] = pre[tile - SUBLANES:, :]
    nconv_ref[...] = pre[tile - (CONV_WIDTH - 1):, :]
    merged = merged + _gate(h, win_ref, bg_ref, 1) * _dot(_bf(a_conv), wco_ref[...])

    for hd in range(HEADS):
        lo = hd * HEAD_DIM
        om = _mem_head(proj(OFF_MQ + lo), mk_ref[:, lo:lo + HEAD_DIM], mv_ref[:, lo:lo + HEAD_DIM])
        act_s[:, lo:lo + HEAD_DIM] = _bf(om * _silu(proj(OFF_MG + lo)))
    merged = merged + _gate(h, win_ref, bg_ref, 2) * _dot(act_s[...], wmo_ref[...])

    y = x + _dot(_bf(merged), wout_ref[...])
    y_ref[...] = _rms(y, fg_ref[...])


def _resident(shape):
    zeros = (0,) * len(shape)
    return pl.BlockSpec(shape, lambda *_: zeros, pipeline_mode=pl.Buffered(1))


def _prompt_layer(x, dec, cos, sin, mk_bf, mv_bf, ng, win_bf, bg, rng, cw, cbias, wro, wco, wmo, wout, fg):
    nb, seq, _ = x.shape
    tile = PROMPT_TILE
    tok = pl.BlockSpec((None, tile, D_MODEL), lambda b, c: (b, c, 0))
    rot = pl.BlockSpec((tile, ROT_HALF), lambda b, c: (c, 0))
    mem = pl.BlockSpec((None, MEM_LEN, D_MODEL), lambda b, c: (b, 0, 0))
    return pl.pallas_call(
        _prompt_body,
        grid=(nb, seq // tile),
        in_specs=[pl.BlockSpec(memory_space=pltpu.SMEM), tok, rot, rot, mem, mem,
                  _resident(ng.shape), _resident(win_bf.shape), _resident(bg.shape), _resident(rng.shape),
                  _resident(cw.shape), _resident(cbias.shape), _resident(wro.shape), _resident(wco.shape),
                  _resident(wmo.shape), _resident(wout.shape), _resident(fg.shape)],
        out_specs=[tok,
                   pl.BlockSpec((None, HEADS, HEAD_DIM, HEAD_DIM), lambda b, c: (b, 0, 0, 0)),
                   pl.BlockSpec((None, CONV_WIDTH - 1, D_MODEL), lambda b, c: (b, 0, 0))],
        out_shape=[jax.ShapeDtypeStruct(x.shape, F32),
                   jax.ShapeDtypeStruct((nb, HEADS, HEAD_DIM, HEAD_DIM), F32),
                   jax.ShapeDtypeStruct((nb, CONV_WIDTH - 1, D_MODEL), F32)],
        scratch_shapes=[pltpu.VMEM((SUBLANES + tile, D_MODEL), F32),
                        pltpu.VMEM((HEADS, RET_CHUNK, RET_CHUNK), F32),
                        pltpu.VMEM((HEADS, RET_CHUNK, HEAD_DIM), F32),
                        pltpu.VMEM((HEADS, RET_CHUNK, HEAD_DIM), F32),
                        pltpu.VMEM((tile, D_MODEL), BF16)],
        compiler_params=pltpu.CompilerParams(dimension_semantics=("arbitrary", "arbitrary"),
                                             vmem_limit_bytes=VMEM_LIMIT_BYTES),
        name="prompt_layer",
    )(dec, x, cos, sin, mk_bf, mv_bf, ng, win_bf, bg, rng, cw, cbias, wro, wco, wmo, wout, fg)


def _sample_proj_body(x_ref, ng_ref, w_ref, z_ref, h_s):
    @pl.when(pl.program_id(0) == 0)
    def _():
        h_s[...] = _bf(_rms(x_ref[...], ng_ref[...]))

    z_ref[...] = _dot(h_s[...], w_ref[...])


def _sample_proj(x2d, ng, win_bf):
    ntok = x2d.shape[0]
    blk = D_MODEL
    return pl.pallas_call(
        _sample_proj_body,
        grid=(OFF_GATE // blk,),
        in_specs=[pl.BlockSpec((ntok, D_MODEL), lambda n: (0, 0)),
                  pl.BlockSpec((1, D_MODEL), lambda n: (0, 0)),
                  pl.BlockSpec((D_MODEL, blk), lambda n: (0, n))],
        out_specs=pl.BlockSpec((ntok, blk), lambda n: (0, n)),
        out_shape=jax.ShapeDtypeStruct((ntok, OFF_GATE), F32),
        scratch_shapes=[pltpu.VMEM((ntok, D_MODEL), BF16)],
        compiler_params=pltpu.CompilerParams(dimension_semantics=("arbitrary",),
                                             vmem_limit_bytes=VMEM_LIMIT_BYTES),
        name="sample_proj",
    )(x2d, ng, win_bf)


def _sample_branch_body(dec_ref, z_ref, cos_ref, sin_ref, state_ref, cstate_ref, mk_ref, mv_ref, rng_ref,
                        cw_ref, cbias_ref,
                        aret_ref, aconv_ref, amem_ref, nstate_ref, nconv_ref,
                        cbuf, inner_s, qdec_s, kdec_s):
    group, seq, _ = z_ref.shape

    @pl.when(pl.program_id(0) == 0)
    def _():
        _fill_decay(dec_ref, inner_s, qdec_s, kdec_s, seq)

    cos = cos_ref[...]
    sin = sin_ref[...]
    for g in range(group):
        def col(off, width=HEAD_DIM):
            return z_ref[g, :, off:off + width]

        for hd in range(HEADS):
            lo = hd * HEAD_DIM
            q = _rope(col(OFF_RQ + lo), cos, sin)
            k = _rope(col(OFF_RK + lo), cos, sin) * QK_SCALE
            o, new_state = _retention_chunk(q, k, col(OFF_RV + lo), state_ref[g, hd],
                                            inner_s[hd], qdec_s[hd], kdec_s[hd], dec_ref[HEADS + hd])
            nstate_ref[g, hd] = new_state
            o = _rms(o, rng_ref[:, lo:lo + HEAD_DIM])
            aret_ref[g, :, lo:lo + HEAD_DIM] = _bf(o * _silu(col(OFF_RG + lo)))

        cbuf[0:SUBLANES, :] = jnp.zeros((SUBLANES, D_MODEL), F32)
        cbuf[SUBLANES - (CONV_WIDTH - 1):SUBLANES, :] = cstate_ref[g]
        pre, a_conv = _conv_branch(col(OFF_CU, D_MODEL), col(OFF_CB, D_MODEL), col(OFF_CC, D_MODEL),
                                   col(OFF_CG, D_MODEL), cbuf, cw_ref, cbias_ref, seq)
        nconv_ref[g] = pre[seq - (CONV_WIDTH - 1):, :]
        aconv_ref[g] = _bf(a_conv)

        q_all = jnp.concatenate([col(OFF_MQ + hd * HEAD_DIM) for hd in range(HEADS)], axis=0)
        mk2d = _bf(mk_ref[g].reshape(MEM_LEN * HEADS, HEAD_DIM))
        mv2d = _bf(mv_ref[g].reshape(MEM_LEN * HEADS, HEAD_DIM))
        s = _dot_nt(_bf(q_all), mk2d) * QK_SCALE
        row_head = lax.broadcasted_iota(jnp.int32, s.shape, 0) // seq
        col_head = lax.broadcasted_iota(jnp.int32, s.shape, 1) % HEADS
        s = jnp.where(row_head == col_head, s, -jnp.inf)
        e = jnp.exp(s - jnp.max(s, axis=-1, keepdims=True))
        p = e / jnp.sum(e, axis=-1, keepdims=True)
        om_all = _dot(_bf(p), mv2d)
        for hd in range(HEADS):
            lo = hd * HEAD_DIM
            om = om_all[hd * seq:(hd + 1) * seq, :]
            amem_ref[g, :, lo:lo + HEAD_DIM] = _bf(om * _silu(col(OFF_MG + lo)))


def _sample_branches(z, dec, cos, sin, state, cstate, mk, mv, rng, cw, cbias):
    nb, seq, _ = z.shape
    grp = SAMPLE_GROUP

    def per_req(*tail):
        zeros = (0,) * len(tail)
        return pl.BlockSpec((grp,) + tail, lambda i: (i,) + zeros)

    def const(shape):
        zeros = (0,) * len(shape)
        return pl.BlockSpec(shape, lambda i: zeros)

    act = jax.ShapeDtypeStruct((nb, seq, D_MODEL), BF16)
    return pl.pallas_call(
        _sample_branch_body,
        grid=(nb // grp,),
        in_specs=[pl.BlockSpec(memory_space=pltpu.SMEM),
                  per_req(seq, OFF_GATE), const(cos.shape), const(sin.shape),
                  per_req(HEADS, HEAD_DIM, HEAD_DIM), per_req(CONV_WIDTH - 1, D_MODEL),
                  per_req(MEM_LEN, HEADS, HEAD_DIM), per_req(MEM_LEN, HEADS, HEAD_DIM),
                  const(rng.shape), const(cw.shape), const(cbias.shape)],
        out_specs=[per_req(seq, D_MODEL), per_req(seq, D_MODEL), per_req(seq, D_MODEL),
                   per_req(HEADS, HEAD_DIM, HEAD_DIM), per_req(CONV_WIDTH - 1, D_MODEL)],
        out_shape=[act, act, act,
                   jax.ShapeDtypeStruct(state.shape, F32),
                   jax.ShapeDtypeStruct(cstate.shape, F32)],
        scratch_shapes=[pltpu.VMEM((SUBLANES + seq, D_MODEL), F32),
                        pltpu.VMEM((HEADS, seq, seq), F32),
                        pltpu.VMEM((HEADS, seq, HEAD_DIM), F32),
                        pltpu.VMEM((HEADS, seq, HEAD_DIM), F32)],
        compiler_params=pltpu.CompilerParams(dimension_semantics=("arbitrary",),
                                             vmem_limit_bytes=VMEM_LIMIT_BYTES),
        name="sample_branches",
    )(dec, z, cos, sin, state, cstate, mk, mv, rng, cw, cbias)


def _sample_tail_body(x_ref, aret_ref, aconv_ref, amem_ref, ng_ref, wg0_ref, wg1_ref, wg2_ref, bg_ref,
                      wro_ref, wco_ref, wmo_ref, wout_ref, fg_ref, y_ref):
    x = x_ref[...]
    h = _bf(_rms(x, ng_ref[...]))

    def gate(branch, wg_ref):
        lo = branch * D_MODEL
        return jax.nn.sigmoid(_dot(h, wg_ref[...]) + bg_ref[:, lo:lo + D_MODEL])

    merged = gate(0, wg0_ref) * _dot(aret_ref[...], wro_ref[...])
    merged = merged + gate(1, wg1_ref) * _dot(aconv_ref[...], wco_ref[...])
    merged = merged + gate(2, wg2_ref) * _dot(amem_ref[...], wmo_ref[...])
    y = x + _dot(_bf(merged), wout_ref[...])
    y_ref[...] = _rms(y, fg_ref[...])


def _sample_tail(x2d, aret, aconv, amem, ng, win_bf, bg, wro, wco, wmo, wout, fg):
    ntok = x2d.shape[0]
    tile = SAMPLE_TAIL_TILE
    tok = pl.BlockSpec((tile, D_MODEL), lambda i: (i, 0))

    def gate_cols(branch):
        blk = OFF_GATE // D_MODEL + branch
        return pl.BlockSpec((D_MODEL, D_MODEL), lambda i: (0, blk), pipeline_mode=pl.Buffered(1))

    return pl.pallas_call(
        _sample_tail_body,
        grid=(ntok // tile,),
        in_specs=[tok, tok, tok, tok,
                  _resident(ng.shape), gate_cols(0), gate_cols(1), gate_cols(2), _resident(bg.shape),
                  _resident(wro.shape), _resident(wco.shape), _resident(wmo.shape), _resident(wout.shape),
                  _resident(fg.shape)],
        out_specs=tok,
        out_shape=jax.ShapeDtypeStruct(x2d.shape, F32),
        compiler_params=pltpu.CompilerParams(dimension_semantics=("arbitrary",),
                                             vmem_limit_bytes=VMEM_LIMIT_BYTES),
        name="sample_tail",
    )(x2d, aret, aconv, amem, ng, win_bf, win_bf, win_bf, bg, wro, wco, wmo, wout, fg)


def _rope_tables(pos):
    inv = ROPE_BASE ** (-jnp.arange(ROT_HALF, dtype=F32) / ROT_HALF)
    ang = pos.astype(F32)[:, None] * inv[None, :]
    return jnp.cos(ang), jnp.sin(ang)


def _decay_scalars(chunk):
    lg = jnp.log1p(-jnp.exp2(-5.0 - jnp.arange(HEADS, dtype=F32)))
    return jnp.concatenate([lg, jnp.exp(lg * chunk)])


def kernel(x_prompt, x_sample, state_ret, state_conv, cache_mem_k, cache_mem_v, mem_prompt, norm_g, w_in, b_gate, ret_norm_g, conv_w, conv_b, w_ret_o, w_conv_o, w_mem_o, w_out, mem_norm_g, w_mem_kv, final_norm_g):
    assert norm_g.shape[0] == 1, "single-layer trunk"
    nbp, seq_p, _ = x_prompt.shape
    nbs, seq_s, _ = x_sample.shape
    assert seq_p % PROMPT_TILE == 0 and PROMPT_TILE % RET_CHUNK == 0
    assert seq_s % RET_CHUNK != 0 and seq_s == SUBLANES and nbs % SAMPLE_GROUP == 0

    ng = norm_g[0][None, :]
    win_bf = _bf(w_in[0])
    bg = b_gate[0][None, :]
    rng = ret_norm_g[0].reshape(1, D_MODEL)
    cw = conv_w[0]
    cbias = conv_b[0][None, :]
    wro, wco, wmo, wout = _bf(w_ret_o[0]), _bf(w_conv_o[0]), _bf(w_mem_o[0]), _bf(w_out[0])
    fg = final_norm_g[None, :]

    cos_p, sin_p = _rope_tables(jnp.arange(seq_p, dtype=jnp.int32))
    cos_s, sin_s = _rope_tables(PAST_LEN + jnp.arange(seq_s, dtype=jnp.int32))

    mk, mv, mk_bf, mv_bf = _memory_kv(mem_prompt, mem_norm_g[0][None, :], _bf(w_mem_kv[0]))
    y_prompt, ret_p, conv_p = _prompt_layer(x_prompt, _decay_scalars(RET_CHUNK), cos_p, sin_p, mk_bf, mv_bf,
                                            ng, win_bf, bg, rng, cw, cbias, wro, wco, wmo, wout, fg)

    xs2d = x_sample.reshape(nbs * seq_s, D_MODEL)
    z = _sample_proj(xs2d, ng, win_bf).reshape(nbs, seq_s, OFF_GATE)
    aret, aconv, amem, ret_s, conv_s = _sample_branches(
        z, _decay_scalars(seq_s), cos_s, sin_s, state_ret[0], state_conv[0],
        cache_mem_k[0], cache_mem_v[0], rng, cw, cbias)
    flat = lambda a: a.reshape(nbs * seq_s, D_MODEL)
    y_sample = _sample_tail(xs2d, flat(aret), flat(aconv), flat(amem), ng, win_bf, bg,
                            wro, wco, wmo, wout, fg).reshape(x_sample.shape)

    return (y_prompt, y_sample, ret_p[None], ret_s[None], conv_p[None], conv_s[None], mk[None], mv[None])
```

```python
import jax
import jax.numpy as jnp
from jax import lax
from jax.experimental import pallas as pl
from jax.experimental.pallas import tpu as pltpu

F32 = jnp.float32
BF16 = jnp.bfloat16

D_MODEL = 1024
HEADS = 4
HEAD_DIM = D_MODEL // HEADS
ROT_HALF = HEAD_DIM // 2
ROPE_BASE = 10000.0
CONV_WIDTH = 3
MEM_LEN = 256
N_BRANCH = 3
EPS = 1e-6
PAST_LEN = 16384
QK_SCALE = HEAD_DIM ** -0.5

OFF_RQ, OFF_RK, OFF_RV, OFF_RG = 0, 1024, 2048, 3072
OFF_CU, OFF_CB, OFF_CC, OFF_CG = 4096, 5120, 6144, 7168
OFF_MQ, OFF_MG = 8192, 9216
OFF_GATE = 10240
IN_TOTAL = OFF_GATE + N_BRANCH * D_MODEL

SUBLANES = 8
VMEM_LIMIT_BYTES = 60 * 1024 * 1024

PROMPT_TILE = 256
SAMPLE_GROUP = 4
SAMPLE_TAIL_TILE = 256


def _bf(x):
    return x.astype(BF16)


def _dot(a, b):
    return jnp.dot(a, b, preferred_element_type=F32)


def _dot_nt(a, b):
    return lax.dot_general(a, b, (((1,), (1,)), ((), ())), preferred_element_type=F32)


def _dot_tn(a, b):
    return lax.dot_general(a, b, (((0,), (0,)), ((), ())), preferred_element_type=F32)


def _rms(x, g):
    return x * lax.rsqrt(jnp.mean(x * x, axis=-1, keepdims=True) + EPS) * g


def _silu(x):
    return x * jax.nn.sigmoid(x)


def _rope(x, cos, sin):
    x1 = x[:, :ROT_HALF]
    x2 = x[:, ROT_HALF:]
    return jnp.concatenate([x1 * cos - x2 * sin, x1 * sin + x2 * cos], axis=-1)


def _fill_decay(dec_ref, inner_s, qdec_s, kdec_s, chunk):
    i = lax.broadcasted_iota(jnp.int32, (chunk, chunk), 0)
    j = lax.broadcasted_iota(jnp.int32, (chunk, chunk), 1)
    diff = (i - j).astype(F32)
    r = lax.broadcasted_iota(jnp.int32, (chunk, ROT_HALF), 0).astype(F32)
    for hd in range(HEADS):
        lg = dec_ref[hd]
        inner_s[hd] = jnp.where(diff >= 0, jnp.exp(lg * jnp.maximum(diff, 0.0)), 0.0)
        qdec_s[hd] = jnp.exp(lg * (r + 1.0))
        kdec_s[hd] = jnp.exp(lg * (chunk - 1.0 - r))


def _retention_chunk(q, k, v, state, inner, qdec, kdec, chunk_dec):
    qb, vb = _bf(q), _bf(v)
    qdec = jnp.concatenate([qdec, qdec], axis=-1)
    kdec = jnp.concatenate([kdec, kdec], axis=-1)
    s = _dot_nt(qb, _bf(k)) * inner
    o = _dot(_bf(s), vb) + _dot(qb, _bf(state)) * qdec
    new_state = state * chunk_dec + _dot_tn(_bf(k * kdec), vb)
    return o, new_state


def _conv_branch(cu, cb, cc, cg, cbuf, cw_ref, cbias_ref, rows, cols):
    pre = cc * cu
    cbuf[SUBLANES:SUBLANES + rows, cols] = pre
    conv = (cbias_ref[:, cols]
            + cw_ref[0:1, cols] * cbuf[SUBLANES - 2:SUBLANES - 2 + rows, cols]
            + cw_ref[1:2, cols] * cbuf[SUBLANES - 1:SUBLANES - 1 + rows, cols]
            + cw_ref[2:3, cols] * pre)
    return pre, cb * conv * _silu(cg)


def _softmax_rows(s):
    e = jnp.exp(s - jnp.max(s, axis=-1, keepdims=True))
    return e * (1.0 / jnp.sum(e, axis=-1, keepdims=True))


def _gate(h, win_ref, bg_ref, branch):
    lo = branch * D_MODEL
    return jax.nn.sigmoid(_dot(h, win_ref[:, OFF_GATE + lo:OFF_GATE + lo + D_MODEL]) + bg_ref[:, lo:lo + D_MODEL])


def _memkv_body(mem_ref, g_ref, w_ref, k_ref, v_ref, kb_ref, vb_ref):
    kv = _dot(_bf(_rms(mem_ref[...], g_ref[...])), w_ref[...])
    k = kv[:, :D_MODEL]
    v = kv[:, D_MODEL:]
    for hd in range(HEADS):
        lo = hd * HEAD_DIM
        k_ref[:, hd, :] = k[:, lo:lo + HEAD_DIM]
        v_ref[:, hd, :] = v[:, lo:lo + HEAD_DIM]
    kb_ref[...] = _bf(k)
    vb_ref[...] = _bf(v)


def _memory_kv(mem, g, w_bf):
    nb = mem.shape[0]
    blk = pl.BlockSpec((None, MEM_LEN, D_MODEL), lambda b: (b, 0, 0))
    blk_heads = pl.BlockSpec((None, MEM_LEN, HEADS, HEAD_DIM), lambda b: (b, 0, 0, 0))
    heads_shape = jax.ShapeDtypeStruct((nb, MEM_LEN, HEADS, HEAD_DIM), F32)
    return pl.pallas_call(
        _memkv_body,
        grid=(nb,),
        in_specs=[blk,
                  pl.BlockSpec((1, D_MODEL), lambda b: (0, 0)),
                  pl.BlockSpec((D_MODEL, 2 * D_MODEL), lambda b: (0, 0))],
        out_specs=[blk_heads, blk_heads, blk, blk],
        out_shape=[heads_shape, heads_shape,
                   jax.ShapeDtypeStruct(mem.shape, BF16), jax.ShapeDtypeStruct(mem.shape, BF16)],
        compiler_params=pltpu.CompilerParams(dimension_semantics=("arbitrary",),
                                             vmem_limit_bytes=VMEM_LIMIT_BYTES),
        name="memory_kv",
    )(mem, g, w_bf)


def _prompt_body(dec_ref, x_ref, cos_ref, sin_ref, mk_ref, mv_ref, ng_ref, win_ref, bg_ref, rng_ref,
                 cw_ref, cbias_ref, wro_ref, wco_ref, wmo_ref, wout_ref, fg_ref,
                 y_ref, state_ref, nconv_ref,
                 cbuf, inner_s, qdec_s, kdec_s, aret_s, aconv_s, amem_s):
    b = pl.program_id(0)
    c = pl.program_id(1)
    tile = x_ref.shape[0]

    @pl.when(jnp.logical_and(b == 0, c == 0))
    def _():
        _fill_decay(dec_ref, inner_s, qdec_s, kdec_s, tile)

    @pl.when(c == 0)
    def _():
        state_ref[...] = jnp.zeros(state_ref.shape, F32)
        cbuf[0:SUBLANES, :] = jnp.zeros((SUBLANES, D_MODEL), F32)

    x = x_ref[...]
    h = _bf(_rms(x, ng_ref[...]))
    cos = cos_ref[...]
    sin = sin_ref[...]

    def proj(off):
        return _dot(h, win_ref[:, off:off + HEAD_DIM])

    heads = range(HEADS)
    head_cols = [slice(hd * HEAD_DIM, (hd + 1) * HEAD_DIM) for hd in heads]

    def conv_block(hd):
        lo = hd * HEAD_DIM
        cols = head_cols[hd]
        pre, a_conv = _conv_branch(proj(OFF_CU + lo), proj(OFF_CB + lo), proj(OFF_CC + lo), proj(OFF_CG + lo),
                                   cbuf, cw_ref, cbias_ref, tile, cols)
        cbuf[0:SUBLANES, cols] = pre[tile - SUBLANES:, :]
        nconv_ref[:, cols] = pre[tile - (CONV_WIDTH - 1):, :]
        aconv_s[:, cols] = _bf(a_conv)

    q = [_bf(_rope(proj(OFF_RQ + hd * HEAD_DIM), cos, sin)) for hd in heads]
    k = [_rope(proj(OFF_RK + hd * HEAD_DIM), cos, sin) * QK_SCALE for hd in heads]
    v = [_bf(proj(OFF_RV + hd * HEAD_DIM)) for hd in heads]
    scores, cross = [], []
    for hd in heads:
        state = state_ref[hd]
        kdec = jnp.concatenate([kdec_s[hd], kdec_s[hd]], axis=-1)
        scores.append(_bf(_dot_nt(q[hd], _bf(k[hd])) * inner_s[hd]))
        cross.append(_dot(q[hd], _bf(state)))
        state_ref[hd] = state * dec_ref[HEADS + hd] + _dot_tn(_bf(k[hd] * kdec), v[hd])
    conv_block(0)
    conv_block(1)
    ret_o = []
    for hd in heads:
        qdec = jnp.concatenate([qdec_s[hd], qdec_s[hd]], axis=-1)
        ret_o.append(_dot(scores[hd], v[hd]) + cross[hd] * qdec)
    conv_block(2)
    conv_block(3)
    for hd in heads:
        cols = head_cols[hd]
        aret_s[:, cols] = _bf(_rms(ret_o[hd], rng_ref[:, cols]) * _silu(proj(OFF_RG + hd * HEAD_DIM)))

    mq = [_bf(proj(OFF_MQ + hd * HEAD_DIM)) for hd in heads]
    mscores = [_dot_nt(mq[hd], mk_ref[:, head_cols[hd]]) * QK_SCALE for hd in heads]
    merged = _gate(h, win_ref, bg_ref, 0) * _dot(aret_s[...], wro_ref[...])
    probs = [_bf(_softmax_rows(mscores[hd])) for hd in heads]
    merged = merged + _gate(h, win_ref, bg_ref, 1) * _dot(aconv_s[...], wco_ref[...])
    for hd in heads:
        cols = head_cols[hd]
        om = _dot(probs[hd], mv_ref[:, cols])
        amem_s[:, cols] = _bf(om * _silu(proj(OFF_MG + hd * HEAD_DIM)))
    merged = merged + _gate(h, win_ref, bg_ref, 2) * _dot(amem_s[...], wmo_ref[...])

    y = x + _dot(_bf(merged), wout_ref[...])
    y_ref[...] = _rms(y, fg_ref[...])


def _resident(shape):
    zeros = (0,) * len(shape)
    return pl.BlockSpec(shape, lambda *_: zeros, pipeline_mode=pl.Buffered(1))


def _prompt_layer(x, dec, cos, sin, mk_bf, mv_bf, ng, win_bf, bg, rng, cw, cbias, wro, wco, wmo, wout, fg):
    nb, seq, _ = x.shape
    tile = PROMPT_TILE
    tok = pl.BlockSpec((None, tile, D_MODEL), lambda b, c: (b, c, 0))
    rot = pl.BlockSpec((tile, ROT_HALF), lambda b, c: (c, 0))
    mem = pl.BlockSpec((None, MEM_LEN, D_MODEL), lambda b, c: (b, 0, 0))
    return pl.pallas_call(
        _prompt_body,
        grid=(nb, seq // tile),
        in_specs=[pl.BlockSpec(memory_space=pltpu.SMEM), tok, rot, rot, mem, mem,
                  _resident(ng.shape), _resident(win_bf.shape), _resident(bg.shape), _resident(rng.shape),
                  _resident(cw.shape), _resident(cbias.shape), _resident(wro.shape), _resident(wco.shape),
                  _resident(wmo.shape), _resident(wout.shape), _resident(fg.shape)],
        out_specs=[tok,
                   pl.BlockSpec((None, HEADS, HEAD_DIM, HEAD_DIM), lambda b, c: (b, 0, 0, 0)),
                   pl.BlockSpec((None, CONV_WIDTH - 1, D_MODEL), lambda b, c: (b, 0, 0))],
        out_shape=[jax.ShapeDtypeStruct(x.shape, F32),
                   jax.ShapeDtypeStruct((nb, HEADS, HEAD_DIM, HEAD_DIM), F32),
                   jax.ShapeDtypeStruct((nb, CONV_WIDTH - 1, D_MODEL), F32)],
        scratch_shapes=[pltpu.VMEM((SUBLANES + tile, D_MODEL), F32),
                        pltpu.VMEM((HEADS, tile, tile), F32),
                        pltpu.VMEM((HEADS, tile, ROT_HALF), F32),
                        pltpu.VMEM((HEADS, tile, ROT_HALF), F32),
                        pltpu.VMEM((tile, D_MODEL), BF16),
                        pltpu.VMEM((tile, D_MODEL), BF16),
                        pltpu.VMEM((tile, D_MODEL), BF16)],
        compiler_params=pltpu.CompilerParams(dimension_semantics=("arbitrary", "arbitrary"),
                                             vmem_limit_bytes=VMEM_LIMIT_BYTES),
        name="prompt_layer",
    )(dec, x, cos, sin, mk_bf, mv_bf, ng, win_bf, bg, rng, cw, cbias, wro, wco, wmo, wout, fg)


def _sample_proj_body(x_ref, ng_ref, w_ref, z_ref, h_s):
    @pl.when(pl.program_id(0) == 0)
    def _():
        h_s[...] = _bf(_rms(x_ref[...], ng_ref[...]))

    z_ref[...] = _dot(h_s[...], w_ref[...])


def _sample_proj(x2d, ng, win_bf):
    ntok = x2d.shape[0]
    blk = D_MODEL
    return pl.pallas_call(
        _sample_proj_body,
        grid=(OFF_GATE // blk,),
        in_specs=[pl.BlockSpec((ntok, D_MODEL), lambda n: (0, 0)),
                  pl.BlockSpec((1, D_MODEL), lambda n: (0, 0)),
                  pl.BlockSpec((D_MODEL, blk), lambda n: (0, n))],
        out_specs=pl.BlockSpec((ntok, blk), lambda n: (0, n)),
        out_shape=jax.ShapeDtypeStruct((ntok, OFF_GATE), F32),
        scratch_shapes=[pltpu.VMEM((ntok, D_MODEL), BF16)],
        compiler_params=pltpu.CompilerParams(dimension_semantics=("arbitrary",),
                                             vmem_limit_bytes=VMEM_LIMIT_BYTES),
        name="sample_proj",
    )(x2d, ng, win_bf)


def _sample_branch_body(dec_ref, z_ref, cos_ref, sin_ref, state_ref, cstate_ref, mk_ref, mv_ref, rng_ref,
                        cw_ref, cbias_ref,
                        aret_ref, aconv_ref, amem_ref, nstate_ref, nconv_ref,
                        cbuf, inner_s, qdec_s, kdec_s):
    group, seq, _ = z_ref.shape

    @pl.when(pl.program_id(0) == 0)
    def _():
        _fill_decay(dec_ref, inner_s, qdec_s, kdec_s, seq)

    cos = cos_ref[...]
    sin = sin_ref[...]
    for g in range(group):
        def col(off, width=HEAD_DIM):
            return z_ref[g, :, off:off + width]

        for hd in range(HEADS):
            lo = hd * HEAD_DIM
            q = _rope(col(OFF_RQ + lo), cos, sin)
            k = _rope(col(OFF_RK + lo), cos, sin) * QK_SCALE
            o, new_state = _retention_chunk(q, k, col(OFF_RV + lo), state_ref[g, hd],
                                            inner_s[hd], qdec_s[hd], kdec_s[hd], dec_ref[HEADS + hd])
            nstate_ref[g, hd] = new_state
            o = _rms(o, rng_ref[:, lo:lo + HEAD_DIM])
            aret_ref[g, :, lo:lo + HEAD_DIM] = _bf(o * _silu(col(OFF_RG + lo)))

        cbuf[0:SUBLANES, :] = jnp.zeros((SUBLANES, D_MODEL), F32)
        cbuf[SUBLANES - (CONV_WIDTH - 1):SUBLANES, :] = cstate_ref[g]
        pre, a_conv = _conv_branch(col(OFF_CU, D_MODEL), col(OFF_CB, D_MODEL), col(OFF_CC, D_MODEL),
                                   col(OFF_CG, D_MODEL), cbuf, cw_ref, cbias_ref, seq, slice(0, D_MODEL))
        nconv_ref[g] = pre[seq - (CONV_WIDTH - 1):, :]
        aconv_ref[g] = _bf(a_conv)

        q_all = jnp.concatenate([col(OFF_MQ + hd * HEAD_DIM) for hd in range(HEADS)], axis=0)
        mk2d = _bf(mk_ref[g].reshape(MEM_LEN * HEADS, HEAD_DIM))
        mv2d = _bf(mv_ref[g].reshape(MEM_LEN * HEADS, HEAD_DIM))
        s = _dot_nt(_bf(q_all), mk2d) * QK_SCALE
        row_head = lax.broadcasted_iota(jnp.int32, s.shape, 0) // seq
        col_head = lax.broadcasted_iota(jnp.int32, s.shape, 1) % HEADS
        s = jnp.where(row_head == col_head, s, -jnp.inf)
        om_all = _dot(_bf(_softmax_rows(s)), mv2d)
        for hd in range(HEADS):
            lo = hd * HEAD_DIM
            om = om_all[hd * seq:(hd + 1) * seq, :]
            amem_ref[g, :, lo:lo + HEAD_DIM] = _bf(om * _silu(col(OFF_MG + lo)))


def _sample_branches(z, dec, cos, sin, state, cstate, mk, mv, rng, cw, cbias):
    nb, seq, _ = z.shape
    grp = SAMPLE_GROUP

    def per_req(*tail):
        zeros = (0,) * len(tail)
        return pl.BlockSpec((grp,) + tail, lambda i: (i,) + zeros)

    def const(shape):
        zeros = (0,) * len(shape)
        return pl.BlockSpec(shape, lambda i: zeros)

    act = jax.ShapeDtypeStruct((nb, seq, D_MODEL), BF16)
    return pl.pallas_call(
        _sample_branch_body,
        grid=(nb // grp,),
        in_specs=[pl.BlockSpec(memory_space=pltpu.SMEM),
                  per_req(seq, OFF_GATE), const(cos.shape), const(sin.shape),
                  per_req(HEADS, HEAD_DIM, HEAD_DIM), per_req(CONV_WIDTH - 1, D_MODEL),
                  per_req(MEM_LEN, HEADS, HEAD_DIM), per_req(MEM_LEN, HEADS, HEAD_DIM),
                  const(rng.shape), const(cw.shape), const(cbias.shape)],
        out_specs=[per_req(seq, D_MODEL), per_req(seq, D_MODEL), per_req(seq, D_MODEL),
                   per_req(HEADS, HEAD_DIM, HEAD_DIM), per_req(CONV_WIDTH - 1, D_MODEL)],
        out_shape=[act, act, act,
                   jax.ShapeDtypeStruct(state.shape, F32),
                   jax.ShapeDtypeStruct(cstate.shape, F32)],
        scratch_shapes=[pltpu.VMEM((SUBLANES + seq, D_MODEL), F32),
                        pltpu.VMEM((HEADS, seq, seq), F32),
                        pltpu.VMEM((HEADS, seq, ROT_HALF), F32),
                        pltpu.VMEM((HEADS, seq, ROT_HALF), F32)],
        compiler_params=pltpu.CompilerParams(dimension_semantics=("arbitrary",),
                                             vmem_limit_bytes=VMEM_LIMIT_BYTES),
        name="sample_branches",
    )(dec, z, cos, sin, state, cstate, mk, mv, rng, cw, cbias)


def _sample_tail_body(x_ref, aret_ref, aconv_ref, amem_ref, ng_ref, wg0_ref, wg1_ref, wg2_ref, bg_ref,
                      wro_ref, wco_ref, wmo_ref, wout_ref, fg_ref, y_ref):
    x = x_ref[...]
    h = _bf(_rms(x, ng_ref[...]))

    def gate(branch, wg_ref):
        lo = branch * D_MODEL
        return jax.nn.sigmoid(_dot(h, wg_ref[...]) + bg_ref[:, lo:lo + D_MODEL])

    merged = gate(0, wg0_ref) * _dot(aret_ref[...], wro_ref[...])
    merged = merged + gate(1, wg1_ref) * _dot(aconv_ref[...], wco_ref[...])
    merged = merged + gate(2, wg2_ref) * _dot(amem_ref[...], wmo_ref[...])
    y = x + _dot(_bf(merged), wout_ref[...])
    y_ref[...] = _rms(y, fg_ref[...])


def _sample_tail(x2d, aret, aconv, amem, ng, win_bf, bg, wro, wco, wmo, wout, fg):
    ntok = x2d.shape[0]
    tile = SAMPLE_TAIL_TILE
    tok = pl.BlockSpec((tile, D_MODEL), lambda i: (i, 0))

    def gate_cols(branch):
        blk = OFF_GATE // D_MODEL + branch
        return pl.BlockSpec((D_MODEL, D_MODEL), lambda i: (0, blk), pipeline_mode=pl.Buffered(1))

    return pl.pallas_call(
        _sample_tail_body,
        grid=(ntok // tile,),
        in_specs=[tok, tok, tok, tok,
                  _resident(ng.shape), gate_cols(0), gate_cols(1), gate_cols(2), _resident(bg.shape),
                  _resident(wro.shape), _resident(wco.shape), _resident(wmo.shape), _resident(wout.shape),
                  _resident(fg.shape)],
        out_specs=tok,
        out_shape=jax.ShapeDtypeStruct(x2d.shape, F32),
        compiler_params=pltpu.CompilerParams(dimension_semantics=("arbitrary",),
                                             vmem_limit_bytes=VMEM_LIMIT_BYTES),
        name="sample_tail",
    )(x2d, aret, aconv, amem, ng, win_bf, win_bf, win_bf, bg, wro, wco, wmo, wout, fg)


def _rope_tables(pos):
    inv = ROPE_BASE ** (-jnp.arange(ROT_HALF, dtype=F32) / ROT_HALF)
    ang = pos.astype(F32)[:, None] * inv[None, :]
    return jnp.cos(ang), jnp.sin(ang)


def _decay_scalars(chunk):
    lg = jnp.log1p(-jnp.exp2(-5.0 - jnp.arange(HEADS, dtype=F32)))
    return jnp.concatenate([lg, jnp.exp(lg * chunk)])


def kernel(x_prompt, x_sample, state_ret, state_conv, cache_mem_k, cache_mem_v, mem_prompt, norm_g, w_in, b_gate, ret_norm_g, conv_w, conv_b, w_ret_o, w_conv_o, w_mem_o, w_out, mem_norm_g, w_mem_kv, final_norm_g):
    assert norm_g.shape[0] == 1, "single-layer trunk"
    nbp, seq_p, _ = x_prompt.shape
    nbs, seq_s, _ = x_sample.shape
    assert seq_p % PROMPT_TILE == 0
    assert seq_s == SUBLANES and nbs % SAMPLE_GROUP == 0

    ng = norm_g[0][None, :]
    win_bf = _bf(w_in[0])
    bg = b_gate[0][None, :]
    rng = ret_norm_g[0].reshape(1, D_MODEL)
    cw = conv_w[0]
    cbias = conv_b[0][None, :]
    wro, wco, wmo, wout = _bf(w_ret_o[0]), _bf(w_conv_o[0]), _bf(w_mem_o[0]), _bf(w_out[0])
    fg = final_norm_g[None, :]

    cos_p, sin_p = _rope_tables(jnp.arange(seq_p, dtype=jnp.int32))
    cos_s, sin_s = _rope_tables(PAST_LEN + jnp.arange(seq_s, dtype=jnp.int32))

    mk, mv, mk_bf, mv_bf = _memory_kv(mem_prompt, mem_norm_g[0][None, :], _bf(w_mem_kv[0]))
    y_prompt, ret_p, conv_p = _prompt_layer(x_prompt, _decay_scalars(PROMPT_TILE), cos_p, sin_p, mk_bf, mv_bf,
                                            ng, win_bf, bg, rng, cw, cbias, wro, wco, wmo, wout, fg)

    xs2d = x_sample.reshape(nbs * seq_s, D_MODEL)
    z = _sample_proj(xs2d, ng, win_bf).reshape(nbs, seq_s, OFF_GATE)
    aret, aconv, amem, ret_s, conv_s = _sample_branches(
        z, _decay_scalars(seq_s), cos_s, sin_s, state_ret[0], state_conv[0],
        cache_mem_k[0], cache_mem_v[0], rng, cw, cbias)
    flat = lambda a: a.reshape(nbs * seq_s, D_MODEL)
    y_sample = _sample_tail(xs2d, flat(aret), flat(aconv), flat(amem), ng, win_bf, bg,
                            wro, wco, wmo, wout, fg).reshape(x_sample.shape)

    return (y_prompt, y_sample, ret_p[None], ret_s[None], conv_p[None], conv_s[None], mk[None], mv[None])
```

```python
import jax
import jax.numpy as jnp
from jax import lax
from jax.experimental import pallas as pl
from jax.experimental.pallas import tpu as pltpu

F32 = jnp.float32
BF16 = jnp.bfloat16

D_MODEL = 1024
HEADS = 4
HEAD_DIM = D_MODEL // HEADS
ROT_HALF = HEAD_DIM // 2
ROPE_BASE = 10000.0
CONV_WIDTH = 3
MEM_LEN = 256
N_BRANCH = 3
EPS = 1e-6
PAST_LEN = 16384
QK_SCALE = HEAD_DIM ** -0.5

OFF_RQ, OFF_RK, OFF_RV, OFF_RG = 0, 1024, 2048, 3072
OFF_CU, OFF_CB, OFF_CC, OFF_CG = 4096, 5120, 6144, 7168
OFF_MQ, OFF_MG = 8192, 9216
OFF_GATE = 10240
IN_TOTAL = OFF_GATE + N_BRANCH * D_MODEL

SUBLANES = 8
VMEM_LIMIT_BYTES = 60 * 1024 * 1024

PROMPT_TILE = 256
TAIL_TILE = 256


def _bf(x):
    return x.astype(BF16)


def _dot(a, b):
    return jnp.dot(a, b, preferred_element_type=F32)


def _dot_nt(a, b):
    return lax.dot_general(a, b, (((1,), (1,)), ((), ())), preferred_element_type=F32)


def _dot_tn(a, b):
    return lax.dot_general(a, b, (((0,), (0,)), ((), ())), preferred_element_type=F32)


def _rms(x, g):
    return x * lax.rsqrt(jnp.mean(x * x, axis=-1, keepdims=True) + EPS) * g


def _silu(x):
    return x * jax.nn.sigmoid(x)


def _rope(x, cos, sin):
    x1 = x[:, :ROT_HALF]
    x2 = x[:, ROT_HALF:]
    return jnp.concatenate([x1 * cos - x2 * sin, x1 * sin + x2 * cos], axis=-1)


def _lanes_twice(t):
    return jnp.concatenate([t, t], axis=-1)


def _fill_decay(dec_ref, inner_s, qdec_s, kdec_s, chunk):
    i = lax.broadcasted_iota(jnp.int32, (chunk, chunk), 0)
    j = lax.broadcasted_iota(jnp.int32, (chunk, chunk), 1)
    diff = (i - j).astype(F32)
    r = lax.broadcasted_iota(jnp.int32, (chunk, ROT_HALF), 0).astype(F32)
    for hd in range(HEADS):
        lg = dec_ref[hd]
        inner_s[hd] = jnp.where(diff >= 0, jnp.exp(lg * jnp.maximum(diff, 0.0)), 0.0)
        qdec_s[hd] = jnp.exp(lg * (r + 1.0))
        kdec_s[hd] = jnp.exp(lg * (chunk - 1.0 - r))


def _conv_branch(cu, cb, cc, cg, cbuf, cw_ref, cbias_ref, rows, cols):
    pre = cc * cu
    cbuf[SUBLANES:SUBLANES + rows, cols] = pre
    conv = (cbias_ref[:, cols]
            + cw_ref[0:1, cols] * cbuf[SUBLANES - 2:SUBLANES - 2 + rows, cols]
            + cw_ref[1:2, cols] * cbuf[SUBLANES - 1:SUBLANES - 1 + rows, cols]
            + cw_ref[2:3, cols] * pre)
    return pre, cb * conv * _silu(cg)


def _softmax_rows(s):
    e = jnp.exp(s - jnp.max(s, axis=-1, keepdims=True))
    return e * (1.0 / jnp.sum(e, axis=-1, keepdims=True))


def _resident(shape, block=None):
    zeros = (0,) * len(shape)
    return pl.BlockSpec(block or shape, lambda *_: zeros, pipeline_mode=pl.Buffered(1))


def _memkv_body(mem_ref, g_ref, w_ref, k_ref, v_ref, kb_ref, vb_ref):
    kv = _dot(_bf(_rms(mem_ref[...], g_ref[...])), w_ref[...])
    k = kv[:, :D_MODEL]
    v = kv[:, D_MODEL:]
    for hd in range(HEADS):
        lo = hd * HEAD_DIM
        k_ref[:, hd, :] = k[:, lo:lo + HEAD_DIM]
        v_ref[:, hd, :] = v[:, lo:lo + HEAD_DIM]
    kb_ref[...] = _bf(k)
    vb_ref[...] = _bf(v)


def _memory_kv(mem, g, w_bf):
    nb = mem.shape[0]
    blk = pl.BlockSpec((None, MEM_LEN, D_MODEL), lambda b: (b, 0, 0))
    blk_heads = pl.BlockSpec((None, MEM_LEN, HEADS, HEAD_DIM), lambda b: (b, 0, 0, 0))
    heads_shape = jax.ShapeDtypeStruct((nb, MEM_LEN, HEADS, HEAD_DIM), F32)
    return pl.pallas_call(
        _memkv_body,
        grid=(nb,),
        in_specs=[blk,
                  pl.BlockSpec((1, D_MODEL), lambda b: (0, 0)),
                  pl.BlockSpec((D_MODEL, 2 * D_MODEL), lambda b: (0, 0))],
        out_specs=[blk_heads, blk_heads, blk, blk],
        out_shape=[heads_shape, heads_shape,
                   jax.ShapeDtypeStruct(mem.shape, BF16), jax.ShapeDtypeStruct(mem.shape, BF16)],
        compiler_params=pltpu.CompilerParams(dimension_semantics=("arbitrary",),
                                             vmem_limit_bytes=VMEM_LIMIT_BYTES),
        name="memory_kv",
    )(mem, g, w_bf)


def _prompt_branch_body(dec_ref, x_ref, cos_ref, sin_ref, mk_ref, mv_ref, ng_ref, win_ref, rng_ref,
                        cw_ref, cbias_ref,
                        aret_ref, aconv_ref, amem_ref, state_ref, nconv_ref,
                        cbuf, inner_s, qdec_s, kdec_s):
    b = pl.program_id(0)
    c = pl.program_id(1)
    tile = x_ref.shape[0]

    @pl.when(jnp.logical_and(b == 0, c == 0))
    def _():
        _fill_decay(dec_ref, inner_s, qdec_s, kdec_s, tile)

    @pl.when(c == 0)
    def _():
        state_ref[...] = jnp.zeros(state_ref.shape, F32)
        cbuf[0:SUBLANES, :] = jnp.zeros((SUBLANES, D_MODEL), F32)

    h = _bf(_rms(x_ref[...], ng_ref[...]))
    cos = cos_ref[...]
    sin = sin_ref[...]

    def proj(off):
        return _dot(h, win_ref[:, off:off + HEAD_DIM])

    heads = range(HEADS)
    head_cols = [slice(hd * HEAD_DIM, (hd + 1) * HEAD_DIM) for hd in heads]

    def conv_block(hd):
        lo = hd * HEAD_DIM
        cols = head_cols[hd]
        pre, a_conv = _conv_branch(proj(OFF_CU + lo), proj(OFF_CB + lo), proj(OFF_CC + lo), proj(OFF_CG + lo),
                                   cbuf, cw_ref, cbias_ref, tile, cols)
        cbuf[0:SUBLANES, cols] = pre[tile - SUBLANES:, :]
        nconv_ref[:, cols] = pre[tile - (CONV_WIDTH - 1):, :]
        aconv_ref[:, cols] = _bf(a_conv)

    q = [_bf(_rope(proj(OFF_RQ + hd * HEAD_DIM), cos, sin)) for hd in heads]
    k = [_rope(proj(OFF_RK + hd * HEAD_DIM), cos, sin) * QK_SCALE for hd in heads]
    v = [_bf(proj(OFF_RV + hd * HEAD_DIM)) for hd in heads]
    scores, cross = [], []
    for hd in heads:
        state = state_ref[hd]
        scores.append(_bf(_dot_nt(q[hd], _bf(k[hd])) * inner_s[hd]))
        cross.append(_dot(q[hd], _bf(state)))
        state_ref[hd] = state * dec_ref[HEADS + hd] + _dot_tn(_bf(k[hd] * _lanes_twice(kdec_s[hd])), v[hd])
    conv_block(0)
    ret_o = [_dot(scores[hd], v[hd]) + cross[hd] * _lanes_twice(qdec_s[hd]) for hd in heads]
    mq = [_bf(proj(OFF_MQ + hd * HEAD_DIM)) for hd in heads]
    conv_block(1)
    mscores = [_dot_nt(mq[hd], mk_ref[:, head_cols[hd]]) * QK_SCALE for hd in heads]
    for hd in heads:
        cols = head_cols[hd]
        aret_ref[:, cols] = _bf(_rms(ret_o[hd], rng_ref[:, cols]) * _silu(proj(OFF_RG + hd * HEAD_DIM)))
    conv_block(2)
    probs = [_bf(_softmax_rows(mscores[hd])) for hd in heads]
    conv_block(3)
    for hd in heads:
        cols = head_cols[hd]
        om = _dot(probs[hd], mv_ref[:, cols])
        amem_ref[:, cols] = _bf(om * _silu(proj(OFF_MG + hd * HEAD_DIM)))


def _prompt_branches(x, dec, cos, sin, mk_bf, mv_bf, ng, win_bf, rng, cw, cbias):
    nb, seq, _ = x.shape
    tile = PROMPT_TILE
    tok = pl.BlockSpec((None, tile, D_MODEL), lambda b, c: (b, c, 0))
    rot = pl.BlockSpec((tile, ROT_HALF), lambda b, c: (c, 0))
    mem = pl.BlockSpec((None, MEM_LEN, D_MODEL), lambda b, c: (b, 0, 0))
    act = jax.ShapeDtypeStruct(x.shape, BF16)
    return pl.pallas_call(
        _prompt_branch_body,
        grid=(nb, seq // tile),
        in_specs=[pl.BlockSpec(memory_space=pltpu.SMEM), tok, rot, rot, mem, mem,
                  _resident(ng.shape), _resident(win_bf.shape, (D_MODEL, OFF_GATE)), _resident(rng.shape),
                  _resident(cw.shape), _resident(cbias.shape)],
        out_specs=[tok, tok, tok,
                   pl.BlockSpec((None, HEADS, HEAD_DIM, HEAD_DIM), lambda b, c: (b, 0, 0, 0)),
                   pl.BlockSpec((None, CONV_WIDTH - 1, D_MODEL), lambda b, c: (b, 0, 0))],
        out_shape=[act, act, act,
                   jax.ShapeDtypeStruct((nb, HEADS, HEAD_DIM, HEAD_DIM), F32),
                   jax.ShapeDtypeStruct((nb, CONV_WIDTH - 1, D_MODEL), F32)],
        scratch_shapes=[pltpu.VMEM((SUBLANES + tile, D_MODEL), F32),
                        pltpu.VMEM((HEADS, tile, tile), F32),
                        pltpu.VMEM((HEADS, tile, ROT_HALF), F32),
                        pltpu.VMEM((HEADS, tile, ROT_HALF), F32)],
        compiler_params=pltpu.CompilerParams(dimension_semantics=("arbitrary", "arbitrary"),
                                             vmem_limit_bytes=VMEM_LIMIT_BYTES),
        name="prompt_branches",
    )(dec, x, cos, sin, mk_bf, mv_bf, ng, win_bf, rng, cw, cbias)


def _sample_proj_body(x_ref, ng_ref, w_ref, z_ref, h_s):
    @pl.when(pl.program_id(0) == 0)
    def _():
        h_s[...] = _bf(_rms(x_ref[...], ng_ref[...]))

    z_ref[...] = _dot(h_s[...], w_ref[...])


def _sample_proj(x2d, ng, win_bf):
    ntok = x2d.shape[0]
    blk = D_MODEL
    return pl.pallas_call(
        _sample_proj_body,
        grid=(OFF_GATE // blk,),
        in_specs=[pl.BlockSpec((ntok, D_MODEL), lambda n: (0, 0)),
                  pl.BlockSpec((1, D_MODEL), lambda n: (0, 0)),
                  pl.BlockSpec((D_MODEL, blk), lambda n: (0, n))],
        out_specs=pl.BlockSpec((ntok, blk), lambda n: (0, n)),
        out_shape=jax.ShapeDtypeStruct((ntok, OFF_GATE), F32),
        scratch_shapes=[pltpu.VMEM((ntok, D_MODEL), BF16)],
        compiler_params=pltpu.CompilerParams(dimension_semantics=("arbitrary",),
                                             vmem_limit_bytes=VMEM_LIMIT_BYTES),
        name="sample_proj",
    )(x2d, ng, win_bf)


def _tail_stages(x_ref, aret_ref, aconv_ref, amem_ref, ng_ref, wg_refs, bg_ref, wro_ref, wco_ref, wmo_ref,
                 wout_ref, fg_ref, y_ref):
    live = {}
    act_refs = (aret_ref, aconv_ref, amem_ref)
    w_refs = (wro_ref, wco_ref, wmo_ref)

    def gate(branch):
        def run():
            if branch == 0:
                live["h"] = _bf(_rms(x_ref[...], ng_ref[...]))
            lo = branch * D_MODEL
            live["gate"] = jax.nn.sigmoid(_dot(live["h"], wg_refs[branch][...]) + bg_ref[:, lo:lo + D_MODEL])
        return run

    def project(branch):
        def run():
            y_branch = live["gate"] * _dot(act_refs[branch][...], w_refs[branch][...])
            live["merged"] = y_branch if branch == 0 else live["merged"] + y_branch
        return run

    def last():
        y = x_ref[...] + _dot(_bf(live["merged"]), wout_ref[...])
        y_ref[...] = _rms(y, fg_ref[...])

    stages = []
    for branch in range(N_BRANCH):
        stages += [gate(branch), project(branch)]
    return stages + [last]


def _sample_stages(z_ref, cos_ref, sin_ref, state_ref, cstate_ref, mk_ref, mv_ref, rng_ref, cw_ref, cbias_ref,
                   aret_ref, aconv_ref, amem_ref, nstate_ref, nconv_ref, cbuf, inner_s, qdec_s, kdec_s, dec_ref):
    group, seq, _ = z_ref.shape
    pairs = [(g, hd) for g in range(group) for hd in range(HEADS)]
    live = {}

    def col(g, off, width=HEAD_DIM):
        return z_ref[g, :, off:off + width]

    def ret_inputs():
        cos = cos_ref[...]
        sin = sin_ref[...]
        live["q"] = {p: _bf(_rope(col(p[0], OFF_RQ + p[1] * HEAD_DIM), cos, sin)) for p in pairs}
        live["k"] = {p: _rope(col(p[0], OFF_RK + p[1] * HEAD_DIM), cos, sin) * QK_SCALE for p in pairs}
        live["v"] = {p: _bf(col(p[0], OFF_RV + p[1] * HEAD_DIM)) for p in pairs}

    def ret_products():
        live["scores"], live["cross"] = {}, {}
        for p in pairs:
            g, hd = p
            state = state_ref[g, hd]
            k = live["k"][p]
            live["scores"][p] = _bf(_dot_nt(live["q"][p], _bf(k)) * inner_s[hd])
            live["cross"][p] = _dot(live["q"][p], _bf(state))
            nstate_ref[g, hd] = (state * dec_ref[HEADS + hd]
                                 + _dot_tn(_bf(k * _lanes_twice(kdec_s[hd])), live["v"][p]))

    def ret_outputs():
        for p in pairs:
            g, hd = p
            lo = hd * HEAD_DIM
            o = _dot(live["scores"][p], live["v"][p]) + live["cross"][p] * _lanes_twice(qdec_s[hd])
            o = _rms(o, rng_ref[:, lo:lo + HEAD_DIM])
            aret_ref[g, :, lo:lo + HEAD_DIM] = _bf(o * _silu(col(g, OFF_RG + lo)))

    def conv():
        for g in range(group):
            base = g * (SUBLANES + seq)
            cbuf[base:base + SUBLANES, :] = jnp.zeros((SUBLANES, D_MODEL), F32)
            cbuf[base + SUBLANES - (CONV_WIDTH - 1):base + SUBLANES, :] = cstate_ref[g]
            pre, a_conv = _conv_branch(col(g, OFF_CU, D_MODEL), col(g, OFF_CB, D_MODEL), col(g, OFF_CC, D_MODEL),
                                       col(g, OFF_CG, D_MODEL), cbuf.at[base:base + SUBLANES + seq], cw_ref,
                                       cbias_ref, seq, slice(0, D_MODEL))
            nconv_ref[g] = pre[seq - (CONV_WIDTH - 1):, :]
            aconv_ref[g] = _bf(a_conv)

    def mem_scores():
        live["ms"] = []
        for g in range(group):
            q_all = jnp.concatenate([col(g, OFF_MQ + hd * HEAD_DIM) for hd in range(HEADS)], axis=0)
            mk2d = _bf(mk_ref[g].reshape(MEM_LEN * HEADS, HEAD_DIM))
            s = _dot_nt(_bf(q_all), mk2d) * QK_SCALE
            row_head = lax.broadcasted_iota(jnp.int32, s.shape, 0) // seq
            col_head = lax.broadcasted_iota(jnp.int32, s.shape, 1) % HEADS
            live["ms"].append(jnp.where(row_head == col_head, s, -jnp.inf))

    def mem_probs():
        live["mp"] = [_bf(_softmax_rows(s)) for s in live["ms"]]

    def mem_outputs():
        for g in range(group):
            mv2d = _bf(mv_ref[g].reshape(MEM_LEN * HEADS, HEAD_DIM))
            om_all = _dot(live["mp"][g], mv2d)
            for hd in range(HEADS):
                lo = hd * HEAD_DIM
                om = om_all[hd * seq:(hd + 1) * seq, :]
                amem_ref[g, :, lo:lo + HEAD_DIM] = _bf(om * _silu(col(g, OFF_MG + lo)))

    return [ret_inputs, ret_products, mem_scores, ret_outputs, mem_probs, conv, mem_outputs]


def _tail_body(x_ref, aret_ref, aconv_ref, amem_ref, ng_ref, wg0_ref, wg1_ref, wg2_ref, bg_ref,
               wro_ref, wco_ref, wmo_ref, wout_ref, fg_ref, y_ref):
    for stage in _tail_stages(x_ref, aret_ref, aconv_ref, amem_ref, ng_ref, (wg0_ref, wg1_ref, wg2_ref), bg_ref,
                              wro_ref, wco_ref, wmo_ref, wout_ref, fg_ref, y_ref):
        stage()


def _tail_with_sample_body(x_ref, aret_ref, aconv_ref, amem_ref, ng_ref, wg0_ref, wg1_ref, wg2_ref, bg_ref,
                           wro_ref, wco_ref, wmo_ref, wout_ref, fg_ref,
                           dec_ref, z_ref, cos_ref, sin_ref, state_ref, cstate_ref, mk_ref, mv_ref, rng_ref,
                           cw_ref, cbias_ref,
                           y_ref, saret_ref, saconv_ref, samem_ref, nstate_ref, nconv_ref,
                           cbuf, inner_s, qdec_s, kdec_s):
    @pl.when(pl.program_id(0) == 0)
    def _():
        _fill_decay(dec_ref, inner_s, qdec_s, kdec_s, z_ref.shape[1])

    tail = _tail_stages(x_ref, aret_ref, aconv_ref, amem_ref, ng_ref, (wg0_ref, wg1_ref, wg2_ref), bg_ref,
                        wro_ref, wco_ref, wmo_ref, wout_ref, fg_ref, y_ref)
    sample = _sample_stages(z_ref, cos_ref, sin_ref, state_ref, cstate_ref, mk_ref, mv_ref, rng_ref, cw_ref,
                            cbias_ref, saret_ref, saconv_ref, samem_ref, nstate_ref, nconv_ref,
                            cbuf, inner_s, qdec_s, kdec_s, dec_ref)
    assert len(sample) == len(tail)
    for sample_stage, tail_stage in zip(sample, tail):
        sample_stage()
        tail_stage()


def _tail_specs(tile, ng, win_bf, bg, wro, wco, wmo, wout, fg):
    tok = pl.BlockSpec((tile, D_MODEL), lambda i: (i, 0))

    def gate_cols(branch):
        blk = OFF_GATE // D_MODEL + branch
        return pl.BlockSpec((D_MODEL, D_MODEL), lambda i: (0, blk), pipeline_mode=pl.Buffered(1))

    in_specs = [tok, tok, tok, tok,
                _resident(ng.shape), gate_cols(0), gate_cols(1), gate_cols(2), _resident(bg.shape),
                _resident(wro.shape), _resident(wco.shape), _resident(wmo.shape), _resident(wout.shape),
                _resident(fg.shape)]
    return tok, in_specs


def _tail(x2d, aret, aconv, amem, ng, win_bf, bg, wro, wco, wmo, wout, fg):
    tok, in_specs = _tail_specs(TAIL_TILE, ng, win_bf, bg, wro, wco, wmo, wout, fg)
    return pl.pallas_call(
        _tail_body,
        grid=(x2d.shape[0] // TAIL_TILE,),
        in_specs=in_specs,
        out_specs=tok,
        out_shape=jax.ShapeDtypeStruct(x2d.shape, F32),
        compiler_params=pltpu.CompilerParams(dimension_semantics=("arbitrary",),
                                             vmem_limit_bytes=VMEM_LIMIT_BYTES),
        name="tail",
    )(x2d, aret, aconv, amem, ng, win_bf, win_bf, win_bf, bg, wro, wco, wmo, wout, fg)


def _tail_with_sample(x2d, aret, aconv, amem, ng, win_bf, bg, wro, wco, wmo, wout, fg,
                      dec, z, cos, sin, state, cstate, mk, mv, rng, cw, cbias):
    steps = x2d.shape[0] // TAIL_TILE
    nb, seq, _ = z.shape
    grp = nb // steps
    assert grp * steps == nb
    tok, in_specs = _tail_specs(TAIL_TILE, ng, win_bf, bg, wro, wco, wmo, wout, fg)

    def per_req(*tail):
        zeros = (0,) * len(tail)
        return pl.BlockSpec((grp,) + tail, lambda i: (i,) + zeros)

    def const(shape):
        zeros = (0,) * len(shape)
        return pl.BlockSpec(shape, lambda i: zeros)

    act = jax.ShapeDtypeStruct((nb, seq, D_MODEL), BF16)
    return pl.pallas_call(
        _tail_with_sample_body,
        grid=(steps,),
        in_specs=in_specs + [pl.BlockSpec(memory_space=pltpu.SMEM),
                             per_req(seq, OFF_GATE), const(cos.shape), const(sin.shape),
                             per_req(HEADS, HEAD_DIM, HEAD_DIM), per_req(CONV_WIDTH - 1, D_MODEL),
                             per_req(MEM_LEN, HEADS, HEAD_DIM), per_req(MEM_LEN, HEADS, HEAD_DIM),
                             const(rng.shape), const(cw.shape), const(cbias.shape)],
        out_specs=[tok, per_req(seq, D_MODEL), per_req(seq, D_MODEL), per_req(seq, D_MODEL),
                   per_req(HEADS, HEAD_DIM, HEAD_DIM), per_req(CONV_WIDTH - 1, D_MODEL)],
        out_shape=[jax.ShapeDtypeStruct(x2d.shape, F32), act, act, act,
                   jax.ShapeDtypeStruct(state.shape, F32),
                   jax.ShapeDtypeStruct(cstate.shape, F32)],
        scratch_shapes=[pltpu.VMEM((grp * (SUBLANES + seq), D_MODEL), F32),
                        pltpu.VMEM((HEADS, seq, seq), F32),
                        pltpu.VMEM((HEADS, seq, ROT_HALF), F32),
                        pltpu.VMEM((HEADS, seq, ROT_HALF), F32)],
        compiler_params=pltpu.CompilerParams(dimension_semantics=("arbitrary",),
                                             vmem_limit_bytes=VMEM_LIMIT_BYTES),
        name="tail_with_sample",
    )(x2d, aret, aconv, amem, ng, win_bf, win_bf, win_bf, bg, wro, wco, wmo, wout, fg,
      dec, z, cos, sin, state, cstate, mk, mv, rng, cw, cbias)


def _rope_tables(pos):
    inv = ROPE_BASE ** (-jnp.arange(ROT_HALF, dtype=F32) / ROT_HALF)
    ang = pos.astype(F32)[:, None] * inv[None, :]
    return jnp.cos(ang), jnp.sin(ang)


def _decay_scalars(chunk):
    lg = jnp.log1p(-jnp.exp2(-5.0 - jnp.arange(HEADS, dtype=F32)))
    return jnp.concatenate([lg, jnp.exp(lg * chunk)])


def kernel(x_prompt, x_sample, state_ret, state_conv, cache_mem_k, cache_mem_v, mem_prompt, norm_g, w_in, b_gate, ret_norm_g, conv_w, conv_b, w_ret_o, w_conv_o, w_mem_o, w_out, mem_norm_g, w_mem_kv, final_norm_g):
    assert norm_g.shape[0] == 1, "single-layer trunk"
    nbp, seq_p, _ = x_prompt.shape
    nbs, seq_s, _ = x_sample.shape
    assert seq_p % PROMPT_TILE == 0 and (nbp * seq_p) % TAIL_TILE == 0
    assert seq_s == SUBLANES and (nbs * seq_s) % TAIL_TILE == 0

    ng = norm_g[0][None, :]
    win_bf = _bf(w_in[0])
    bg = b_gate[0][None, :]
    rng = ret_norm_g[0].reshape(1, D_MODEL)
    cw = conv_w[0]
    cbias = conv_b[0][None, :]
    wro, wco, wmo, wout = _bf(w_ret_o[0]), _bf(w_conv_o[0]), _bf(w_mem_o[0]), _bf(w_out[0])
    fg = final_norm_g[None, :]
    tail_weights = (ng, win_bf, bg, wro, wco, wmo, wout, fg)

    cos_p, sin_p = _rope_tables(jnp.arange(seq_p, dtype=jnp.int32))
    cos_s, sin_s = _rope_tables(PAST_LEN + jnp.arange(seq_s, dtype=jnp.int32))

    mk, mv, mk_bf, mv_bf = _memory_kv(mem_prompt, mem_norm_g[0][None, :], _bf(w_mem_kv[0]))
    aret_p, aconv_p, amem_p, ret_p, conv_p = _prompt_branches(
        x_prompt, _decay_scalars(PROMPT_TILE), cos_p, sin_p, mk_bf, mv_bf, ng, win_bf, rng, cw, cbias)

    xs2d = x_sample.reshape(nbs * seq_s, D_MODEL)
    z = _sample_proj(xs2d, ng, win_bf).reshape(nbs, seq_s, OFF_GATE)

    flat_p = lambda a: a.reshape(nbp * seq_p, D_MODEL)
    y_prompt, aret_s, aconv_s, amem_s, ret_s, conv_s = _tail_with_sample(
        flat_p(x_prompt), flat_p(aret_p), flat_p(aconv_p), flat_p(amem_p), *tail_weights,
        _decay_scalars(seq_s), z, cos_s, sin_s, state_ret[0], state_conv[0], cache_mem_k[0], cache_mem_v[0],
        rng, cw, cbias)

    flat_s = lambda a: a.reshape(nbs * seq_s, D_MODEL)
    y_sample = _tail(xs2d, flat_s(aret_s), flat_s(aconv_s), flat_s(amem_s), *tail_weights)

    return (y_prompt.reshape(x_prompt.shape), y_sample.reshape(x_sample.shape),
            ret_p[None], ret_s[None], conv_p[None], conv_s[None], mk[None], mv[None])
```

```python
import jax
import jax.numpy as jnp
from jax import lax
from jax.experimental import pallas as pl
from jax.experimental.pallas import tpu as pltpu

F32 = jnp.float32
BF16 = jnp.bfloat16

D_MODEL = 1024
HEADS = 4
HEAD_DIM = D_MODEL // HEADS
ROT_HALF = HEAD_DIM // 2
ROPE_BASE = 10000.0
CONV_WIDTH = 3
MEM_LEN = 256
N_BRANCH = 3
EPS = 1e-6
PAST_LEN = 16384
QK_SCALE = HEAD_DIM ** -0.5

OFF_RQ, OFF_RK, OFF_RV, OFF_RG = 0, 1024, 2048, 3072
OFF_CU, OFF_CB, OFF_CC, OFF_CG = 4096, 5120, 6144, 7168
OFF_MQ, OFF_MG = 8192, 9216
OFF_GATE = 10240
IN_TOTAL = OFF_GATE + N_BRANCH * D_MODEL

SUBLANES = 8
VMEM_LIMIT_BYTES = 60 * 1024 * 1024

PROMPT_TILE = 512
RET_CHUNK = 256
TAIL_TILE = 256


def _bf(x):
    return x.astype(BF16)


def _dot(a, b):
    return jnp.dot(a, b, preferred_element_type=F32)


def _dot_nt(a, b):
    return lax.dot_general(a, b, (((1,), (1,)), ((), ())), preferred_element_type=F32)


def _dot_tn(a, b):
    return lax.dot_general(a, b, (((0,), (0,)), ((), ())), preferred_element_type=F32)


def _rms(x, g):
    return x * lax.rsqrt(jnp.mean(x * x, axis=-1, keepdims=True) + EPS) * g


def _silu(x):
    return x * jax.nn.sigmoid(x)


def _rope(x, cos, sin):
    x1 = x[:, :ROT_HALF]
    x2 = x[:, ROT_HALF:]
    return jnp.concatenate([x1 * cos - x2 * sin, x1 * sin + x2 * cos], axis=-1)


def _lanes_twice(t):
    return jnp.concatenate([t, t], axis=-1)


def _fill_decay(dec_ref, inner_s, qdec_s, kdec_s, chunk):
    i = lax.broadcasted_iota(jnp.int32, (chunk, chunk), 0)
    j = lax.broadcasted_iota(jnp.int32, (chunk, chunk), 1)
    diff = (i - j).astype(F32)
    r = lax.broadcasted_iota(jnp.int32, (chunk, ROT_HALF), 0).astype(F32)
    for hd in range(HEADS):
        lg = dec_ref[hd]
        inner_s[hd] = jnp.where(diff >= 0, jnp.exp(lg * jnp.maximum(diff, 0.0)), 0.0)
        qdec_s[hd] = jnp.exp(lg * (r + 1.0))
        kdec_s[hd] = jnp.exp(lg * (chunk - 1.0 - r))


def _conv_branch(cu, cb, cc, cg, cbuf, cw_ref, cbias_ref, rows, cols):
    pre = cc * cu
    cbuf[SUBLANES:SUBLANES + rows, cols] = pre
    conv = (cbias_ref[:, cols]
            + cw_ref[0:1, cols] * cbuf[SUBLANES - 2:SUBLANES - 2 + rows, cols]
            + cw_ref[1:2, cols] * cbuf[SUBLANES - 1:SUBLANES - 1 + rows, cols]
            + cw_ref[2:3, cols] * pre)
    return pre, cb * conv * _silu(cg)


def _softmax_rows(s):
    e = jnp.exp(s - jnp.max(s, axis=-1, keepdims=True))
    return e * (1.0 / jnp.sum(e, axis=-1, keepdims=True))


def _resident(shape, block=None):
    zeros = (0,) * len(shape)
    return pl.BlockSpec(block or shape, lambda *_: zeros, pipeline_mode=pl.Buffered(1))


def _memkv_body(mem_ref, g_ref, w_ref, k_ref, v_ref, kb_ref, vb_ref, w_s):
    @pl.when(pl.program_id(0) == 0)
    def _():
        w_s[...] = _bf(w_ref[...])

    kv = _dot(_bf(_rms(mem_ref[...], g_ref[...])), w_s[...])
    k = kv[:, :D_MODEL]
    v = kv[:, D_MODEL:]
    for hd in range(HEADS):
        lo = hd * HEAD_DIM
        k_ref[:, hd, :] = k[:, lo:lo + HEAD_DIM]
        v_ref[:, hd, :] = v[:, lo:lo + HEAD_DIM]
    kb_ref[...] = _bf(k)
    vb_ref[...] = _bf(v)


def _memory_kv(mem, g, w):
    nb = mem.shape[0]
    blk = pl.BlockSpec((None, MEM_LEN, D_MODEL), lambda b: (b, 0, 0))
    blk_heads = pl.BlockSpec((None, MEM_LEN, HEADS, HEAD_DIM), lambda b: (b, 0, 0, 0))
    heads_shape = jax.ShapeDtypeStruct((nb, MEM_LEN, HEADS, HEAD_DIM), F32)
    return pl.pallas_call(
        _memkv_body,
        grid=(nb,),
        in_specs=[blk, _resident(g.shape), _resident(w.shape)],
        out_specs=[blk_heads, blk_heads, blk, blk],
        out_shape=[heads_shape, heads_shape,
                   jax.ShapeDtypeStruct(mem.shape, BF16), jax.ShapeDtypeStruct(mem.shape, BF16)],
        scratch_shapes=[pltpu.VMEM(w.shape, BF16)],
        compiler_params=pltpu.CompilerParams(dimension_semantics=("arbitrary",),
                                             vmem_limit_bytes=VMEM_LIMIT_BYTES),
        name="memory_kv",
    )(mem, g, w)


def _prompt_branch_body(dec_ref, x_ref, cos_ref, sin_ref, mk_ref, mv_ref, ng_ref, win_ref, rng_ref,
                        cw_ref, cbias_ref,
                        h_ref, aret_ref, aconv_ref, amem_ref, state_ref, nconv_ref,
                        cbuf, inner_s, qdec_s, kdec_s):
    b = pl.program_id(0)
    c = pl.program_id(1)
    tile = x_ref.shape[0]

    @pl.when(jnp.logical_and(b == 0, c == 0))
    def _():
        _fill_decay(dec_ref, inner_s, qdec_s, kdec_s, RET_CHUNK)

    @pl.when(c == 0)
    def _():
        state_ref[...] = jnp.zeros(state_ref.shape, F32)
        cbuf[0:SUBLANES, :] = jnp.zeros((SUBLANES, D_MODEL), F32)

    h = _bf(_rms(x_ref[...], ng_ref[...]))
    h_ref[...] = h
    cos = cos_ref[...]
    sin = sin_ref[...]

    def proj(off):
        return _dot(h, win_ref[:, off:off + HEAD_DIM])

    heads = range(HEADS)
    head_cols = [slice(hd * HEAD_DIM, (hd + 1) * HEAD_DIM) for hd in heads]

    def conv_block(hd):
        lo = hd * HEAD_DIM
        cols = head_cols[hd]
        pre, a_conv = _conv_branch(proj(OFF_CU + lo), proj(OFF_CB + lo), proj(OFF_CC + lo), proj(OFF_CG + lo),
                                   cbuf, cw_ref, cbias_ref, tile, cols)
        cbuf[0:SUBLANES, cols] = pre[tile - SUBLANES:, :]
        nconv_ref[:, cols] = pre[tile - (CONV_WIDTH - 1):, :]
        aconv_ref[:, cols] = _bf(a_conv)

    q = [_bf(_rope(proj(OFF_RQ + hd * HEAD_DIM), cos, sin)) for hd in heads]
    k = [_rope(proj(OFF_RK + hd * HEAD_DIM), cos, sin) * QK_SCALE for hd in heads]
    v = [_bf(proj(OFF_RV + hd * HEAD_DIM)) for hd in heads]
    chunks = [slice(j * RET_CHUNK, (j + 1) * RET_CHUNK) for j in range(tile // RET_CHUNK)]
    scores, cross = [], []
    for rows in chunks:
        for hd in heads:
            state = state_ref[hd]
            kc = k[hd][rows]
            scores.append(_bf(_dot_nt(q[hd][rows], _bf(kc)) * inner_s[hd]))
            cross.append(_dot(q[hd][rows], _bf(state)))
            state_ref[hd] = (state * dec_ref[HEADS + hd]
                             + _dot_tn(_bf(kc * _lanes_twice(kdec_s[hd])), v[hd][rows]))
    conv_block(0)
    ret_o = []
    for hd in heads:
        qdec = _lanes_twice(qdec_s[hd])
        ret_o.append(jnp.concatenate(
            [_dot(scores[j * HEADS + hd], v[hd][rows]) + cross[j * HEADS + hd] * qdec
             for j, rows in enumerate(chunks)], axis=0))
    mq =[_bf(proj(OFF_MQ + hd * HEAD_DIM)) for hd in heads]
    conv_block(1)
    mscores = [_dot_nt(mq[hd], mk_ref[:, head_cols[hd]]) * QK_SCALE for hd in heads]
    for hd in heads:
        cols = head_cols[hd]
        aret_ref[:, cols] = _bf(_rms(ret_o[hd], rng_ref[:, cols]) * _silu(proj(OFF_RG + hd * HEAD_DIM)))
    conv_block(2)
    probs = [_bf(_softmax_rows(mscores[hd])) for hd in heads]
    conv_block(3)
    for hd in heads:
        cols = head_cols[hd]
        om = _dot(probs[hd], mv_ref[:, cols])
        amem_ref[:, cols] = _bf(om * _silu(proj(OFF_MG + hd * HEAD_DIM)))


def _prompt_branches(x, dec, cos, sin, mk_bf, mv_bf, ng, win_bf, rng, cw, cbias):
    nb, seq, _ = x.shape
    tile = PROMPT_TILE
    tok = pl.BlockSpec((None, tile, D_MODEL), lambda b, c: (b, c, 0))
    rot = pl.BlockSpec((tile, ROT_HALF), lambda b, c: (c, 0))
    mem = pl.BlockSpec((None, MEM_LEN, D_MODEL), lambda b, c: (b, 0, 0))
    act = jax.ShapeDtypeStruct(x.shape, BF16)
    return pl.pallas_call(
        _prompt_branch_body,
        grid=(nb, seq // tile),
        in_specs=[pl.BlockSpec(memory_space=pltpu.SMEM), tok, rot, rot, mem, mem,
                  _resident(ng.shape), _resident(win_bf.shape, (D_MODEL, OFF_GATE)), _resident(rng.shape),
                  _resident(cw.shape), _resident(cbias.shape)],
        out_specs=[tok, tok, tok, tok,
                   pl.BlockSpec((None, HEADS, HEAD_DIM, HEAD_DIM), lambda b, c: (b, 0, 0, 0)),
                   pl.BlockSpec((None, CONV_WIDTH - 1, D_MODEL), lambda b, c: (b, 0, 0))],
        out_shape=[act, act, act, act,
                   jax.ShapeDtypeStruct((nb, HEADS, HEAD_DIM, HEAD_DIM), F32),
                   jax.ShapeDtypeStruct((nb, CONV_WIDTH - 1, D_MODEL), F32)],
        scratch_shapes=[pltpu.VMEM((SUBLANES + tile, D_MODEL), F32),
                        pltpu.VMEM((HEADS, RET_CHUNK, RET_CHUNK), F32),
                        pltpu.VMEM((HEADS, RET_CHUNK, ROT_HALF), F32),
                        pltpu.VMEM((HEADS, RET_CHUNK, ROT_HALF), F32)],
        compiler_params=pltpu.CompilerParams(dimension_semantics=("arbitrary", "arbitrary"),
                                             vmem_limit_bytes=VMEM_LIMIT_BYTES),
        name="prompt_branches",
    )(dec, x, cos, sin, mk_bf, mv_bf, ng, win_bf, rng, cw, cbias)


def _sample_proj_body(x_ref, ng_ref, w_ref, z_ref, wbf_ref, h_ref):
    n = pl.program_id(0)

    @pl.when(n == 0)
    def _():
        h_ref[...] = _bf(_rms(x_ref[...], ng_ref[...]))

    w = _bf(w_ref[...])
    wbf_ref[...] = w

    @pl.when(n < OFF_GATE // w_ref.shape[1])
    def _():
        z_ref[...] = _dot(h_ref[...], w)


def _sample_proj(x2d, ng, w_in):
    ntok = x2d.shape[0]
    blk = D_MODEL
    z_blocks = OFF_GATE // blk
    return pl.pallas_call(
        _sample_proj_body,
        grid=(IN_TOTAL // blk,),
        in_specs=[_resident(x2d.shape), _resident(ng.shape),
                  pl.BlockSpec((D_MODEL, blk), lambda n: (0, n))],
        out_specs=[pl.BlockSpec((ntok, blk), lambda n: (0, jnp.minimum(n, z_blocks - 1))),
                   pl.BlockSpec((D_MODEL, blk), lambda n: (0, n)),
                   pl.BlockSpec(x2d.shape, lambda n: (0, 0))],
        out_shape=[jax.ShapeDtypeStruct((ntok, OFF_GATE), F32),
                   jax.ShapeDtypeStruct(w_in.shape, BF16),
                   jax.ShapeDtypeStruct(x2d.shape, BF16)],
        compiler_params=pltpu.CompilerParams(dimension_semantics=("arbitrary",),
                                             vmem_limit_bytes=VMEM_LIMIT_BYTES),
        name="sample_proj",
    )(x2d, ng, w_in)


def _tail_stages(x_ref, aret_ref, aconv_ref, amem_ref, h_ref, wg_refs, bg_ref, wro_ref, wco_ref, wmo_ref,
                 wout_ref, fg_ref, y_ref):
    live = {}
    act_refs = (aret_ref, aconv_ref, amem_ref)
    w_refs = (wro_ref, wco_ref, wmo_ref)

    def gate(branch):
        def run():
            lo = branch * D_MODEL
            live["gate"] = jax.nn.sigmoid(_dot(h_ref[...], wg_refs[branch][...]) + bg_ref[:, lo:lo + D_MODEL])
        return run

    def project(branch):
        def run():
            y_branch = live["gate"] * _dot(act_refs[branch][...], w_refs[branch][...])
            live["merged"] = y_branch if branch == 0 else live["merged"] + y_branch
        return run

    def out_project():
        live["y"] = x_ref[...] + _dot(_bf(live["merged"]), wout_ref[...])

    def finish():
        y_ref[...] = _rms(live["y"], fg_ref[...])

    stages = []
    for branch in range(N_BRANCH):
        stages += [gate(branch), project(branch)]
    return stages + [out_project, finish]


def _sample_stages(z_ref, cos_ref, sin_ref, state_ref, cstate_ref, mk_ref, mv_ref, rng_ref, cw_ref, cbias_ref,
                   aret_ref, aconv_ref, amem_ref, nstate_ref, nconv_ref, cbuf, inner_s, qdec_s, kdec_s, dec_ref):
    group, seq, _ = z_ref.shape
    pairs = [(g, hd) for g in range(group) for hd in range(HEADS)]
    live = {}

    def col(g, off, width=HEAD_DIM):
        return z_ref[g, :, off:off + width]

    def ret_inputs():
        cos = cos_ref[...]
        sin = sin_ref[...]
        live["q"] = {p: _bf(_rope(col(p[0], OFF_RQ + p[1] * HEAD_DIM), cos, sin)) for p in pairs}
        live["k"] = {p: _rope(col(p[0], OFF_RK + p[1] * HEAD_DIM), cos, sin) * QK_SCALE for p in pairs}
        live["v"] = {p: _bf(col(p[0], OFF_RV + p[1] * HEAD_DIM)) for p in pairs}

    def ret_products():
        live["scores"], live["cross"] = {}, {}
        for p in pairs:
            g, hd = p
            state = state_ref[g, hd]
            k = live["k"][p]
            live["scores"][p] = _bf(_dot_nt(live["q"][p], _bf(k)) * inner_s[hd])
            live["cross"][p] = _dot(live["q"][p], _bf(state))
            nstate_ref[g, hd] = (state * dec_ref[HEADS + hd]
                                 + _dot_tn(_bf(k * _lanes_twice(kdec_s[hd])), live["v"][p]))

    def ret_outputs():
        for p in pairs:
            g, hd = p
            lo = hd * HEAD_DIM
            o = _dot(live["scores"][p], live["v"][p]) + live["cross"][p] * _lanes_twice(qdec_s[hd])
            o = _rms(o, rng_ref[:, lo:lo + HEAD_DIM])
            aret_ref[g, :, lo:lo + HEAD_DIM] = _bf(o * _silu(col(g, OFF_RG + lo)))

    def conv():
        for g in range(group):
            base = g * (SUBLANES + seq)
            cbuf[base:base + SUBLANES, :] = jnp.zeros((SUBLANES, D_MODEL), F32)
            cbuf[base + SUBLANES - (CONV_WIDTH - 1):base + SUBLANES, :] = cstate_ref[g]
            pre, a_conv = _conv_branch(col(g, OFF_CU, D_MODEL), col(g, OFF_CB, D_MODEL), col(g, OFF_CC, D_MODEL),
                                       col(g, OFF_CG, D_MODEL), cbuf.at[base:base + SUBLANES + seq], cw_ref,
                                       cbias_ref, seq, slice(0, D_MODEL))
            nconv_ref[g] = pre[seq - (CONV_WIDTH - 1):, :]
            aconv_ref[g] = _bf(a_conv)

    def mem_scores():
        live["ms"] = []
        for g in range(group):
            q_all = jnp.concatenate([col(g, OFF_MQ + hd * HEAD_DIM) for hd in range(HEADS)], axis=0)
            mk2d = _bf(mk_ref[g].reshape(MEM_LEN * HEADS, HEAD_DIM))
            s = _dot_nt(_bf(q_all), mk2d) * QK_SCALE
            row_head = lax.broadcasted_iota(jnp.int32, s.shape, 0) // seq
            col_head = lax.broadcasted_iota(jnp.int32, s.shape, 1) % HEADS
            live["ms"].append(jnp.where(row_head == col_head, s, -jnp.inf))

    def mem_probs():
        live["mp"] = [_bf(_softmax_rows(s)) for s in live["ms"]]

    def mem_outputs():
        for g in range(group):
            mv2d = _bf(mv_ref[g].reshape(MEM_LEN * HEADS, HEAD_DIM))
            om_all = _dot(live["mp"][g], mv2d)
            for hd in range(HEADS):
                lo = hd * HEAD_DIM
                om = om_all[hd * seq:(hd + 1) * seq, :]
                amem_ref[g, :, lo:lo + HEAD_DIM] = _bf(om * _silu(col(g, OFF_MG + lo)))

    return [ret_inputs, ret_products, mem_scores, ret_outputs, mem_probs, conv, mem_outputs]


def _tail_body(x_ref, aret_ref, aconv_ref, amem_ref, h_ref, wg0_ref, wg1_ref, wg2_ref, bg_ref,
               wro_ref, wco_ref, wmo_ref, wout_ref, fg_ref, y_ref):
    for stage in _tail_stages(x_ref, aret_ref, aconv_ref, amem_ref, h_ref, (wg0_ref, wg1_ref, wg2_ref), bg_ref,
                              wro_ref, wco_ref, wmo_ref, wout_ref, fg_ref, y_ref):
        stage()


def _tail_with_sample_body(x_ref, aret_ref, aconv_ref, amem_ref, h_ref, wg0_ref, wg1_ref, wg2_ref, bg_ref,
                           wro_ref, wco_ref, wmo_ref, wout_ref, fg_ref,
                           dec_ref, z_ref, cos_ref, sin_ref, state_ref, cstate_ref, mk_ref, mv_ref, rng_ref,
                           cw_ref, cbias_ref,
                           y_ref, saret_ref, saconv_ref, samem_ref, nstate_ref, nconv_ref,
                           cbuf, inner_s, qdec_s, kdec_s):
    @pl.when(pl.program_id(0) == 0)
    def _():
        _fill_decay(dec_ref, inner_s, qdec_s, kdec_s, z_ref.shape[1])

    tail = _tail_stages(x_ref, aret_ref, aconv_ref, amem_ref, h_ref, (wg0_ref, wg1_ref, wg2_ref), bg_ref,
                        wro_ref, wco_ref, wmo_ref, wout_ref, fg_ref, y_ref)
    sample = _sample_stages(z_ref, cos_ref, sin_ref, state_ref, cstate_ref, mk_ref, mv_ref, rng_ref, cw_ref,
                            cbias_ref, saret_ref, saconv_ref, samem_ref, nstate_ref, nconv_ref,
                            cbuf, inner_s, qdec_s, kdec_s, dec_ref)
    assert len(sample) == len(tail) - 1
    for i, tail_stage in enumerate(tail):
        if i < len(sample) - 1:
            sample[i]()
        elif i == len(tail) - 1:
            sample[-1]()
        tail_stage()


def _tail_specs(tile, win_bf, bg, wro, wco, wmo, wout, fg):
    tok = pl.BlockSpec((tile, D_MODEL), lambda i: (i, 0))

    def gate_cols(branch):
        blk = OFF_GATE // D_MODEL + branch
        return pl.BlockSpec((D_MODEL, D_MODEL), lambda i: (0, blk), pipeline_mode=pl.Buffered(1))

    in_specs = [tok, tok, tok, tok,
                tok, gate_cols(0), gate_cols(1), gate_cols(2), _resident(bg.shape),
                _resident(wro.shape), _resident(wco.shape), _resident(wmo.shape), _resident(wout.shape),
                _resident(fg.shape)]
    return tok, in_specs


def _tail(x2d, aret, aconv, amem, h2d, win_bf, bg, wro, wco, wmo, wout, fg):
    tok, in_specs = _tail_specs(TAIL_TILE, win_bf, bg, wro, wco, wmo, wout, fg)
    return pl.pallas_call(
        _tail_body,
        grid=(x2d.shape[0] // TAIL_TILE,),
        in_specs=in_specs,
        out_specs=tok,
        out_shape=jax.ShapeDtypeStruct(x2d.shape, F32),
        compiler_params=pltpu.CompilerParams(dimension_semantics=("arbitrary",),
                                             vmem_limit_bytes=VMEM_LIMIT_BYTES),
        name="tail",
    )(x2d, aret, aconv, amem, h2d, win_bf, win_bf, win_bf, bg, wro, wco, wmo, wout, fg)


def _tail_with_sample(x2d, aret, aconv, amem, h2d, win_bf, bg, wro, wco, wmo, wout, fg,
                      dec, z, cos, sin, state, cstate, mk, mv, rng, cw, cbias):
    steps = x2d.shape[0] // TAIL_TILE
    nb, seq, _ = z.shape
    grp = nb // steps
    assert grp * steps == nb
    tok, in_specs = _tail_specs(TAIL_TILE, win_bf, bg, wro, wco, wmo, wout, fg)

    def per_req(*tail):
        zeros = (0,) * len(tail)
        return pl.BlockSpec((grp,) + tail, lambda i: (i,) + zeros)

    def const(shape):
        zeros = (0,) * len(shape)
        return pl.BlockSpec(shape, lambda i: zeros)

    act = jax.ShapeDtypeStruct((nb, seq, D_MODEL), BF16)
    return pl.pallas_call(
        _tail_with_sample_body,
        grid=(steps,),
        in_specs=in_specs + [pl.BlockSpec(memory_space=pltpu.SMEM),
                             per_req(seq, OFF_GATE), const(cos.shape), const(sin.shape),
                             per_req(HEADS, HEAD_DIM, HEAD_DIM), per_req(CONV_WIDTH - 1, D_MODEL),
                             per_req(MEM_LEN, HEADS, HEAD_DIM), per_req(MEM_LEN, HEADS, HEAD_DIM),
                             const(rng.shape), const(cw.shape), const(cbias.shape)],
        out_specs=[tok, per_req(seq, D_MODEL), per_req(seq, D_MODEL), per_req(seq, D_MODEL),
                   per_req(HEADS, HEAD_DIM, HEAD_DIM), per_req(CONV_WIDTH - 1, D_MODEL)],
        out_shape=[jax.ShapeDtypeStruct(x2d.shape, F32), act, act, act,
                   jax.ShapeDtypeStruct(state.shape, F32),
                   jax.ShapeDtypeStruct(cstate.shape, F32)],
        scratch_shapes=[pltpu.VMEM((grp * (SUBLANES + seq), D_MODEL), F32),
                        pltpu.VMEM((HEADS, seq, seq), F32),
                        pltpu.VMEM((HEADS, seq, ROT_HALF), F32),
                        pltpu.VMEM((HEADS, seq, ROT_HALF), F32)],
        compiler_params=pltpu.CompilerParams(dimension_semantics=("arbitrary",),
                                             vmem_limit_bytes=VMEM_LIMIT_BYTES),
        name="tail_with_sample",
    )(x2d, aret, aconv, amem, h2d, win_bf, win_bf, win_bf, bg, wro, wco, wmo, wout, fg,
      dec, z, cos, sin, state, cstate, mk, mv, rng, cw, cbias)


def _rope_tables(pos):
    inv = ROPE_BASE ** (-jnp.arange(ROT_HALF, dtype=F32) / ROT_HALF)
    ang = pos.astype(F32)[:, None] * inv[None, :]
    return jnp.cos(ang), jnp.sin(ang)


def _decay_scalars(chunk):
    lg = jnp.log1p(-jnp.exp2(-5.0 - jnp.arange(HEADS, dtype=F32)))
    return jnp.concatenate([lg, jnp.exp(lg * chunk)])


def kernel(x_prompt, x_sample, state_ret, state_conv, cache_mem_k, cache_mem_v, mem_prompt, norm_g, w_in, b_gate, ret_norm_g, conv_w, conv_b, w_ret_o, w_conv_o, w_mem_o, w_out, mem_norm_g, w_mem_kv, final_norm_g):
    assert norm_g.shape[0] == 1, "single-layer trunk"
    nbp, seq_p, _ = x_prompt.shape
    nbs, seq_s, _ = x_sample.shape
    assert seq_p % PROMPT_TILE == 0 and PROMPT_TILE % RET_CHUNK == 0 and (nbp * seq_p) % TAIL_TILE == 0
    assert seq_s == SUBLANES and (nbs * seq_s) % TAIL_TILE == 0

    ng = norm_g[0][None, :]
    bg = b_gate[0][None, :]
    rng = ret_norm_g[0].reshape(1, D_MODEL)
    cw = conv_w[0]
    cbias = conv_b[0][None, :]
    wro, wco, wmo, wout = _bf(w_ret_o[0]), _bf(w_conv_o[0]), _bf(w_mem_o[0]), _bf(w_out[0])
    fg = final_norm_g[None, :]

    cos_p, sin_p = _rope_tables(jnp.arange(seq_p, dtype=jnp.int32))
    cos_s, sin_s = _rope_tables(PAST_LEN + jnp.arange(seq_s, dtype=jnp.int32))

    xs2d = x_sample.reshape(nbs * seq_s, D_MODEL)
    z, win_bf, h_s = _sample_proj(xs2d, ng, w_in[0])
    z = z.reshape(nbs, seq_s, OFF_GATE)
    tail_weights = (win_bf, bg, wro, wco, wmo, wout, fg)

    mk, mv, mk_bf, mv_bf = _memory_kv(mem_prompt, mem_norm_g[0][None, :], w_mem_kv[0])
    h_p, aret_p, aconv_p, amem_p, ret_p, conv_p = _prompt_branches(
        x_prompt, _decay_scalars(RET_CHUNK), cos_p, sin_p, mk_bf, mv_bf, ng, win_bf, rng, cw, cbias)

    flat_p = lambda a: a.reshape(nbp * seq_p, D_MODEL)
    y_prompt, aret_s, aconv_s, amem_s, ret_s, conv_s = _tail_with_sample(
        flat_p(x_prompt), flat_p(aret_p), flat_p(aconv_p), flat_p(amem_p), flat_p(h_p), *tail_weights,
        _decay_scalars(seq_s), z, cos_s, sin_s, state_ret[0], state_conv[0], cache_mem_k[0], cache_mem_v[0],
        rng, cw, cbias)

    flat_s = lambda a: a.reshape(nbs * seq_s, D_MODEL)
    y_sample = _tail(xs2d, flat_s(aret_s), flat_s(aconv_s), flat_s(amem_s), h_s, *tail_weights)

    return (y_prompt.reshape(x_prompt.shape), y_sample.reshape(x_sample.shape),
            ret_p[None], ret_s[None], conv_p[None], conv_s[None], mk[None], mv[None])
```

```python
import jax
import jax.numpy as jnp
from jax import lax
from jax.experimental import pallas as pl
from jax.experimental.pallas import tpu as pltpu

F32 = jnp.float32
BF16 = jnp.bfloat16

D_MODEL = 1024
HEADS = 4
HEAD_DIM = D_MODEL // HEADS
ROT_HALF = HEAD_DIM // 2
ROPE_BASE = 10000.0
CONV_WIDTH = 3
MEM_LEN = 256
N_BRANCH = 3
EPS = 1e-6
PAST_LEN = 16384
QK_SCALE = HEAD_DIM ** -0.5

OFF_RQ, OFF_RK, OFF_RV, OFF_RG = 0, 1024, 2048, 3072
OFF_CU, OFF_CB, OFF_CC, OFF_CG = 4096, 5120, 6144, 7168
OFF_MQ, OFF_MG = 8192, 9216
OFF_GATE = 10240
IN_TOTAL = OFF_GATE + N_BRANCH * D_MODEL

SUBLANES = 8
VMEM_LIMIT_BYTES = 60 * 1024 * 1024

PROMPT_TILE = 512
RET_CHUNK = 256
TAIL_TILE = 256


def _bf(x):
    return x.astype(BF16)


def _dot(a, b):
    return jnp.dot(a, b, preferred_element_type=F32)


def _dot_nt(a, b):
    return lax.dot_general(a, b, (((1,), (1,)), ((), ())), preferred_element_type=F32)


def _dot_tn(a, b):
    return lax.dot_general(a, b, (((0,), (0,)), ((), ())), preferred_element_type=F32)


def _rms(x, g):
    return x * lax.rsqrt(jnp.mean(x * x, axis=-1, keepdims=True) + EPS) * g


def _silu(x):
    return x * jax.nn.sigmoid(x)


def _rope(x, cos, sin):
    x1 = x[:, :ROT_HALF]
    x2 = x[:, ROT_HALF:]
    return jnp.concatenate([x1 * cos - x2 * sin, x1 * sin + x2 * cos], axis=-1)


def _lanes_twice(t):
    return jnp.concatenate([t, t], axis=-1)


def _pack_rows(w_bf):
    return pltpu.bitcast(w_bf, jnp.int32)


def _unpack_rows(w_packed):
    return pltpu.bitcast(w_packed, BF16)


def _fill_decay(dec_ref, inner_s, qdec_s, kdec_s, chunk):
    i = lax.broadcasted_iota(jnp.int32, (chunk, chunk), 0)
    j = lax.broadcasted_iota(jnp.int32, (chunk, chunk), 1)
    diff = (i - j).astype(F32)
    r = lax.broadcasted_iota(jnp.int32, (chunk, ROT_HALF), 0).astype(F32)
    for hd in range(HEADS):
        lg = dec_ref[hd]
        inner_s[hd] = jnp.where(diff >= 0, jnp.exp(lg * jnp.maximum(diff, 0.0)), 0.0)
        qdec_s[hd] = jnp.exp(lg * (r + 1.0))
        kdec_s[hd] = jnp.exp(lg * (chunk - 1.0 - r))


def _conv_branch(cu, cb, cc, cg, cbuf, cw_ref, cbias_ref, rows, cols):
    pre = cc * cu
    cbuf[SUBLANES:SUBLANES + rows, cols] = pre
    conv = (cbias_ref[:, cols]
            + cw_ref[0:1, cols] * cbuf[SUBLANES - 2:SUBLANES - 2 + rows, cols]
            + cw_ref[1:2, cols] * cbuf[SUBLANES - 1:SUBLANES - 1 + rows, cols]
            + cw_ref[2:3, cols] * pre)
    return pre, cb * conv * _silu(cg)


def _softmax_rows(s):
    e = jnp.exp(s - jnp.max(s, axis=-1, keepdims=True))
    return e * (1.0 / jnp.sum(e, axis=-1, keepdims=True))


def _resident(shape, block=None):
    zeros = (0,) * len(shape)
    return pl.BlockSpec(block or shape, lambda *_: zeros, pipeline_mode=pl.Buffered(1))


def _pack_weights_body(*refs):
    n = len(refs) // 2
    for w_ref, out_ref in zip(refs[:n], refs[n:]):
        out_ref[...] = _pack_rows(_bf(w_ref[...]))


def _pack_weights(*ws):
    return pl.pallas_call(
        _pack_weights_body,
        out_shape=[jax.ShapeDtypeStruct((w.shape[0] // 2, w.shape[1]), jnp.int32) for w in ws],
        compiler_params=pltpu.CompilerParams(vmem_limit_bytes=VMEM_LIMIT_BYTES),
        name="pack_weights",
    )(*ws)


def _memkv_body(mem_ref, g_ref, w_ref, k_ref, v_ref, kb_ref, vb_ref, w_s):
    @pl.when(pl.program_id(0) == 0)
    def _():
        w_s[...] = _bf(w_ref[...])

    kv = _dot(_bf(_rms(mem_ref[...], g_ref[...])), w_s[...])
    k = kv[:, :D_MODEL]
    v = kv[:, D_MODEL:]
    for hd in range(HEADS):
        lo = hd * HEAD_DIM
        k_ref[:, hd, :] = k[:, lo:lo + HEAD_DIM]
        v_ref[:, hd, :] = v[:, lo:lo + HEAD_DIM]
    kb_ref[...] = _bf(k)
    vb_ref[...] = _bf(v)


def _memory_kv(mem, g, w):
    nb = mem.shape[0]
    blk = pl.BlockSpec((None, MEM_LEN, D_MODEL), lambda b: (b, 0, 0))
    blk_heads = pl.BlockSpec((None, MEM_LEN, HEADS, HEAD_DIM), lambda b: (b, 0, 0, 0))
    heads_shape = jax.ShapeDtypeStruct((nb, MEM_LEN, HEADS, HEAD_DIM), F32)
    return pl.pallas_call(
        _memkv_body,
        grid=(nb,),
        in_specs=[blk, _resident(g.shape), _resident(w.shape)],
        out_specs=[blk_heads, blk_heads, blk, blk],
        out_shape=[heads_shape, heads_shape,
                   jax.ShapeDtypeStruct(mem.shape, BF16), jax.ShapeDtypeStruct(mem.shape, BF16)],
        scratch_shapes=[pltpu.VMEM(w.shape, BF16)],
        compiler_params=pltpu.CompilerParams(dimension_semantics=("arbitrary",),
                                             vmem_limit_bytes=VMEM_LIMIT_BYTES),
        name="memory_kv",
    )(mem, g, w)


def _prompt_branch_body(dec_ref, x_ref, cos_ref, sin_ref, mk_ref, mv_ref, ng_ref, win_ref, rng_ref,
                        cw_ref, cbias_ref,
                        h_ref, aret_ref, aconv_ref, amem_ref, state_ref, nconv_ref,
                        cbuf, inner_s, qdec_s, kdec_s):
    b = pl.program_id(0)
    c = pl.program_id(1)
    tile = x_ref.shape[0]

    @pl.when(jnp.logical_and(b == 0, c == 0))
    def _():
        _fill_decay(dec_ref, inner_s, qdec_s, kdec_s, RET_CHUNK)

    @pl.when(c == 0)
    def _():
        state_ref[...] = jnp.zeros(state_ref.shape, F32)
        cbuf[0:SUBLANES, :] = jnp.zeros((SUBLANES, D_MODEL), F32)

    h = _bf(_rms(x_ref[...], ng_ref[...]))
    h_ref[...] = h
    cos = cos_ref[...]
    sin = sin_ref[...]

    def proj(off):
        return _dot(h, _unpack_rows(win_ref[:, off:off + HEAD_DIM]))

    heads = range(HEADS)
    head_cols = [slice(hd * HEAD_DIM, (hd + 1) * HEAD_DIM) for hd in heads]

    def conv_block(hd):
        lo = hd * HEAD_DIM
        cols = head_cols[hd]
        pre, a_conv = _conv_branch(proj(OFF_CU + lo), proj(OFF_CB + lo), proj(OFF_CC + lo), proj(OFF_CG + lo),
                                   cbuf, cw_ref, cbias_ref, tile, cols)
        cbuf[0:SUBLANES, cols] = pre[tile - SUBLANES:, :]
        nconv_ref[:, cols] = pre[tile - (CONV_WIDTH - 1):, :]
        aconv_ref[:, cols] = _bf(a_conv)

    q = [_bf(_rope(proj(OFF_RQ + hd * HEAD_DIM), cos, sin)) for hd in heads]
    k = [_rope(proj(OFF_RK + hd * HEAD_DIM), cos, sin) * QK_SCALE for hd in heads]
    v = [_bf(proj(OFF_RV + hd * HEAD_DIM)) for hd in heads]
    chunks = [slice(j * RET_CHUNK, (j + 1) * RET_CHUNK) for j in range(tile // RET_CHUNK)]
    scores, cross = [], []
    for rows in chunks:
        for hd in heads:
            state = state_ref[hd]
            kc = k[hd][rows]
            scores.append(_bf(_dot_nt(q[hd][rows], _bf(kc)) * inner_s[hd]))
            cross.append(_dot(q[hd][rows], _bf(state)))
            state_ref[hd] = (state * dec_ref[HEADS + hd]
                             + _dot_tn(_bf(kc * _lanes_twice(kdec_s[hd])), v[hd][rows]))
    conv_block(0)
    ret_o = []
    for hd in heads:
        qdec = _lanes_twice(qdec_s[hd])
        ret_o.append(jnp.concatenate(
            [_dot(scores[j * HEADS + hd], v[hd][rows]) + cross[j * HEADS + hd] * qdec
             for j, rows in enumerate(chunks)], axis=0))
    mq =[_bf(proj(OFF_MQ + hd * HEAD_DIM)) for hd in heads]
    conv_block(1)
    mscores = [_dot_nt(mq[hd], mk_ref[:, head_cols[hd]]) * QK_SCALE for hd in heads]
    for hd in heads:
        cols = head_cols[hd]
        aret_ref[:, cols] = _bf(_rms(ret_o[hd], rng_ref[:, cols]) * _silu(proj(OFF_RG + hd * HEAD_DIM)))
    conv_block(2)
    probs = [_bf(_softmax_rows(mscores[hd])) for hd in heads]
    conv_block(3)
    for hd in heads:
        cols = head_cols[hd]
        om = _dot(probs[hd], mv_ref[:, cols])
        amem_ref[:, cols] = _bf(om * _silu(proj(OFF_MG + hd * HEAD_DIM)))


def _prompt_branches(x, dec, cos, sin, mk_bf, mv_bf, ng, win_pk, rng, cw, cbias):
    nb, seq, _ = x.shape
    tile = PROMPT_TILE
    tok = pl.BlockSpec((None, tile, D_MODEL), lambda b, c: (b, c, 0))
    rot = pl.BlockSpec((tile, ROT_HALF), lambda b, c: (c, 0))
    mem = pl.BlockSpec((None, MEM_LEN, D_MODEL), lambda b, c: (b, 0, 0))
    act = jax.ShapeDtypeStruct(x.shape, BF16)
    return pl.pallas_call(
        _prompt_branch_body,
        grid=(nb, seq // tile),
        in_specs=[pl.BlockSpec(memory_space=pltpu.SMEM), tok, rot, rot, mem, mem,
                  _resident(ng.shape), _resident(win_pk.shape, (D_MODEL // 2, OFF_GATE)), _resident(rng.shape),
                  _resident(cw.shape), _resident(cbias.shape)],
        out_specs=[tok, tok, tok, tok,
                   pl.BlockSpec((None, HEADS, HEAD_DIM, HEAD_DIM), lambda b, c: (b, 0, 0, 0)),
                   pl.BlockSpec((None, CONV_WIDTH - 1, D_MODEL), lambda b, c: (b, 0, 0))],
        out_shape=[act, act, act, act,
                   jax.ShapeDtypeStruct((nb, HEADS, HEAD_DIM, HEAD_DIM), F32),
                   jax.ShapeDtypeStruct((nb, CONV_WIDTH - 1, D_MODEL), F32)],
        scratch_shapes=[pltpu.VMEM((SUBLANES + tile, D_MODEL), F32),
                        pltpu.VMEM((HEADS, RET_CHUNK, RET_CHUNK), F32),
                        pltpu.VMEM((HEADS, RET_CHUNK, ROT_HALF), F32),
                        pltpu.VMEM((HEADS, RET_CHUNK, ROT_HALF), F32)],
        compiler_params=pltpu.CompilerParams(dimension_semantics=("arbitrary", "arbitrary"),
                                             vmem_limit_bytes=VMEM_LIMIT_BYTES),
        name="prompt_branches",
    )(dec, x, cos, sin, mk_bf, mv_bf, ng, win_pk, rng, cw, cbias)


def _sample_proj_body(x_ref, ng_ref, w_ref, z_ref, wpk_ref, h_ref):
    n = pl.program_id(0)

    @pl.when(n == 0)
    def _():
        h_ref[...] = _bf(_rms(x_ref[...], ng_ref[...]))

    w = _bf(w_ref[...])
    wpk_ref[...] = _pack_rows(w)

    @pl.when(n < OFF_GATE // w_ref.shape[1])
    def _():
        z_ref[...] = _dot(h_ref[...], w)


def _sample_proj(x2d, ng, w_in):
    ntok = x2d.shape[0]
    blk = D_MODEL
    z_blocks = OFF_GATE // blk
    return pl.pallas_call(
        _sample_proj_body,
        grid=(IN_TOTAL // blk,),
        in_specs=[_resident(x2d.shape), _resident(ng.shape),
                  pl.BlockSpec((D_MODEL, blk), lambda n: (0, n))],
        out_specs=[pl.BlockSpec((ntok, blk), lambda n: (0, jnp.minimum(n, z_blocks - 1))),
                   pl.BlockSpec((D_MODEL // 2, blk), lambda n: (0, n)),
                   pl.BlockSpec(x2d.shape, lambda n: (0, 0))],
        out_shape=[jax.ShapeDtypeStruct((ntok, OFF_GATE), F32),
                   jax.ShapeDtypeStruct((D_MODEL // 2, IN_TOTAL), jnp.int32),
                   jax.ShapeDtypeStruct(x2d.shape, BF16)],
        compiler_params=pltpu.CompilerParams(dimension_semantics=("arbitrary",),
                                             vmem_limit_bytes=VMEM_LIMIT_BYTES),
        name="sample_proj",
    )(x2d, ng, w_in)


def _tail_stages(x_ref, aret_ref, aconv_ref, amem_ref, h_ref, wg_refs, bg_ref, wro_ref, wco_ref, wmo_ref,
                 wout_ref, fg_ref, y_ref):
    live = {}
    act_refs = (aret_ref, aconv_ref, amem_ref)
    w_refs = (wro_ref, wco_ref, wmo_ref)

    def gate(branch):
        def run():
            lo = branch * D_MODEL
            live["gate"] = jax.nn.sigmoid(_dot(h_ref[...], _unpack_rows(wg_refs[branch][...])) + bg_ref[:, lo:lo + D_MODEL])
        return run

    def project(branch):
        def run():
            y_branch = live["gate"] * _dot(act_refs[branch][...], _unpack_rows(w_refs[branch][...]))
            live["merged"] = y_branch if branch == 0 else live["merged"] + y_branch
        return run

    def out_project():
        live["y"] = x_ref[...] + _dot(_bf(live["merged"]), _unpack_rows(wout_ref[...]))

    def finish():
        y_ref[...] = _rms(live["y"], fg_ref[...])

    stages = []
    for branch in range(N_BRANCH):
        stages += [gate(branch), project(branch)]
    return stages + [out_project, finish]


def _sample_stages(z_ref, cos_ref, sin_ref, state_ref, cstate_ref, mk_ref, mv_ref, rng_ref, cw_ref, cbias_ref,
                   aret_ref, aconv_ref, amem_ref, nstate_ref, nconv_ref, cbuf, inner_s, qdec_s, kdec_s, dec_ref):
    group, seq, _ = z_ref.shape
    pairs = [(g, hd) for g in range(group) for hd in range(HEADS)]
    live = {}

    def col(g, off, width=HEAD_DIM):
        return z_ref[g, :, off:off + width]

    def ret_inputs():
        cos = cos_ref[...]
        sin = sin_ref[...]
        live["q"] = {p: _bf(_rope(col(p[0], OFF_RQ + p[1] * HEAD_DIM), cos, sin)) for p in pairs}
        live["k"] = {p: _rope(col(p[0], OFF_RK + p[1] * HEAD_DIM), cos, sin) * QK_SCALE for p in pairs}
        live["v"] = {p: _bf(col(p[0], OFF_RV + p[1] * HEAD_DIM)) for p in pairs}

    def ret_products():
        live["scores"], live["cross"] = {}, {}
        for p in pairs:
            g, hd = p
            state = state_ref[g, hd]
            k = live["k"][p]
            live["scores"][p] = _bf(_dot_nt(live["q"][p], _bf(k)) * inner_s[hd])
            live["cross"][p] = _dot(live["q"][p], _bf(state))
            nstate_ref[g, hd] = (state * dec_ref[HEADS + hd]
                                 + _dot_tn(_bf(k * _lanes_twice(kdec_s[hd])), live["v"][p]))

    def ret_outputs():
        for p in pairs:
            g, hd = p
            lo = hd * HEAD_DIM
            o = _dot(live["scores"][p], live["v"][p]) + live["cross"][p] * _lanes_twice(qdec_s[hd])
            o = _rms(o, rng_ref[:, lo:lo + HEAD_DIM])
            aret_ref[g, :, lo:lo + HEAD_DIM] = _bf(o * _silu(col(g, OFF_RG + lo)))

    def conv():
        for g in range(group):
            base = g * (SUBLANES + seq)
            cbuf[base:base + SUBLANES, :] = jnp.zeros((SUBLANES, D_MODEL), F32)
            cbuf[base + SUBLANES - (CONV_WIDTH - 1):base + SUBLANES, :] = cstate_ref[g]
            pre, a_conv = _conv_branch(col(g, OFF_CU, D_MODEL), col(g, OFF_CB, D_MODEL), col(g, OFF_CC, D_MODEL),
                                       col(g, OFF_CG, D_MODEL), cbuf.at[base:base + SUBLANES + seq], cw_ref,
                                       cbias_ref, seq, slice(0, D_MODEL))
            nconv_ref[g] = pre[seq - (CONV_WIDTH - 1):, :]
            aconv_ref[g] = _bf(a_conv)

    def mem_scores():
        live["ms"] = []
        for g in range(group):
            q_all = jnp.concatenate([col(g, OFF_MQ + hd * HEAD_DIM) for hd in range(HEADS)], axis=0)
            mk2d = _bf(mk_ref[g].reshape(MEM_LEN * HEADS, HEAD_DIM))
            s = _dot_nt(_bf(q_all), mk2d) * QK_SCALE
            row_head = lax.broadcasted_iota(jnp.int32, s.shape, 0) // seq
            col_head = lax.broadcasted_iota(jnp.int32, s.shape, 1) % HEADS
            live["ms"].append(jnp.where(row_head == col_head, s, -jnp.inf))

    def mem_probs():
        live["mp"] = [_bf(_softmax_rows(s)) for s in live["ms"]]

    def mem_outputs():
        for g in range(group):
            mv2d = _bf(mv_ref[g].reshape(MEM_LEN * HEADS, HEAD_DIM))
            om_all = _dot(live["mp"][g], mv2d)
            for hd in range(HEADS):
                lo = hd * HEAD_DIM
                om = om_all[hd * seq:(hd + 1) * seq, :]
                amem_ref[g, :, lo:lo + HEAD_DIM] = _bf(om * _silu(col(g, OFF_MG + lo)))

    return [ret_inputs, ret_products, mem_scores, ret_outputs, mem_probs, conv, mem_outputs]


def _tail_body(x_ref, aret_ref, aconv_ref, amem_ref, h_ref, wg0_ref, wg1_ref, wg2_ref, bg_ref,
               wro_ref, wco_ref, wmo_ref, wout_ref, fg_ref, y_ref):
    for stage in _tail_stages(x_ref, aret_ref, aconv_ref, amem_ref, h_ref, (wg0_ref, wg1_ref, wg2_ref), bg_ref,
                              wro_ref, wco_ref, wmo_ref, wout_ref, fg_ref, y_ref):
        stage()


def _tail_with_sample_body(x_ref, aret_ref, aconv_ref, amem_ref, h_ref, wg0_ref, wg1_ref, wg2_ref, bg_ref,
                           wro_ref, wco_ref, wmo_ref, wout_ref, fg_ref,
                           dec_ref, z_ref, cos_ref, sin_ref, state_ref, cstate_ref, mk_ref, mv_ref, rng_ref,
                           cw_ref, cbias_ref,
                           y_ref, saret_ref, saconv_ref, samem_ref, nstate_ref, nconv_ref,
                           cbuf, inner_s, qdec_s, kdec_s):
    @pl.when(pl.program_id(0) == 0)
    def _():
        _fill_decay(dec_ref, inner_s, qdec_s, kdec_s, z_ref.shape[1])

    tail = _tail_stages(x_ref, aret_ref, aconv_ref, amem_ref, h_ref, (wg0_ref, wg1_ref, wg2_ref), bg_ref,
                        wro_ref, wco_ref, wmo_ref, wout_ref, fg_ref, y_ref)
    sample = _sample_stages(z_ref, cos_ref, sin_ref, state_ref, cstate_ref, mk_ref, mv_ref, rng_ref, cw_ref,
                            cbias_ref, saret_ref, saconv_ref, samem_ref, nstate_ref, nconv_ref,
                            cbuf, inner_s, qdec_s, kdec_s, dec_ref)
    assert len(sample) == len(tail) - 1
    for i, tail_stage in enumerate(tail):
        if i < len(sample) - 1:
            sample[i]()
        elif i == len(tail) - 1:
            sample[-1]()
        tail_stage()


def _tail_specs(tile, bg, wro, wco, wmo, wout, fg):
    tok = pl.BlockSpec((tile, D_MODEL), lambda i: (i, 0))

    def gate_cols(branch):
        blk = OFF_GATE // D_MODEL + branch
        return pl.BlockSpec((D_MODEL // 2, D_MODEL), lambda i: (0, blk), pipeline_mode=pl.Buffered(1))

    in_specs = [tok, tok, tok, tok,
                tok, gate_cols(0), gate_cols(1), gate_cols(2), _resident(bg.shape),
                _resident(wro.shape), _resident(wco.shape), _resident(wmo.shape), _resident(wout.shape),
                _resident(fg.shape)]
    return tok, in_specs


def _tail(x2d, aret, aconv, amem, h2d, win_pk, bg, wro, wco, wmo, wout, fg):
    tok, in_specs = _tail_specs(TAIL_TILE, bg, wro, wco, wmo, wout, fg)
    return pl.pallas_call(
        _tail_body,
        grid=(x2d.shape[0] // TAIL_TILE,),
        in_specs=in_specs,
        out_specs=tok,
        out_shape=jax.ShapeDtypeStruct(x2d.shape, F32),
        compiler_params=pltpu.CompilerParams(dimension_semantics=("arbitrary",),
                                             vmem_limit_bytes=VMEM_LIMIT_BYTES),
        name="tail",
    )(x2d, aret, aconv, amem, h2d, win_pk, win_pk, win_pk, bg, wro, wco, wmo, wout, fg)


def _tail_with_sample(x2d, aret, aconv, amem, h2d, win_pk, bg, wro, wco, wmo, wout, fg,
                      dec, z, cos, sin, state, cstate, mk, mv, rng, cw, cbias):
    steps = x2d.shape[0] // TAIL_TILE
    nb, seq, _ = z.shape
    grp = nb // steps
    assert grp * steps == nb
    tok, in_specs = _tail_specs(TAIL_TILE, bg, wro, wco, wmo, wout, fg)

    def per_req(*tail):
        zeros = (0,) * len(tail)
        return pl.BlockSpec((grp,) + tail, lambda i: (i,) + zeros)

    def const(shape):
        zeros = (0,) * len(shape)
        return pl.BlockSpec(shape, lambda i: zeros)

    act = jax.ShapeDtypeStruct((nb, seq, D_MODEL), BF16)
    return pl.pallas_call(
        _tail_with_sample_body,
        grid=(steps,),
        in_specs=in_specs + [pl.BlockSpec(memory_space=pltpu.SMEM),
                             per_req(seq, OFF_GATE), const(cos.shape), const(sin.shape),
                             per_req(HEADS, HEAD_DIM, HEAD_DIM), per_req(CONV_WIDTH - 1, D_MODEL),
                             per_req(MEM_LEN, HEADS, HEAD_DIM), per_req(MEM_LEN, HEADS, HEAD_DIM),
                             const(rng.shape), const(cw.shape), const(cbias.shape)],
        out_specs=[tok, per_req(seq, D_MODEL), per_req(seq, D_MODEL), per_req(seq, D_MODEL),
                   per_req(HEADS, HEAD_DIM, HEAD_DIM), per_req(CONV_WIDTH - 1, D_MODEL)],
        out_shape=[jax.ShapeDtypeStruct(x2d.shape, F32), act, act, act,
                   jax.ShapeDtypeStruct(state.shape, F32),
                   jax.ShapeDtypeStruct(cstate.shape, F32)],
        scratch_shapes=[pltpu.VMEM((grp * (SUBLANES + seq), D_MODEL), F32),
                        pltpu.VMEM((HEADS, seq, seq), F32),
                        pltpu.VMEM((HEADS, seq, ROT_HALF), F32),
                        pltpu.VMEM((HEADS, seq, ROT_HALF), F32)],
        compiler_params=pltpu.CompilerParams(dimension_semantics=("arbitrary",),
                                             vmem_limit_bytes=VMEM_LIMIT_BYTES),
        name="tail_with_sample",
    )(x2d, aret, aconv, amem, h2d, win_pk, win_pk, win_pk, bg, wro, wco, wmo, wout, fg,
      dec, z, cos, sin, state, cstate, mk, mv, rng, cw, cbias)


def _rope_tables(pos):
    inv = ROPE_BASE ** (-jnp.arange(ROT_HALF, dtype=F32) / ROT_HALF)
    ang = pos.astype(F32)[:, None] * inv[None, :]
    return jnp.cos(ang), jnp.sin(ang)


def _decay_scalars(chunk):
    lg = jnp.log1p(-jnp.exp2(-5.0 - jnp.arange(HEADS, dtype=F32)))
    return jnp.concatenate([lg, jnp.exp(lg * chunk)])


def kernel(x_prompt, x_sample, state_ret, state_conv, cache_mem_k, cache_mem_v, mem_prompt, norm_g, w_in, b_gate, ret_norm_g, conv_w, conv_b, w_ret_o, w_conv_o, w_mem_o, w_out, mem_norm_g, w_mem_kv, final_norm_g):
    assert norm_g.shape[0] == 1, "single-layer trunk"
    nbp, seq_p, _ = x_prompt.shape
    nbs, seq_s, _ = x_sample.shape
    assert seq_p % PROMPT_TILE == 0 and PROMPT_TILE % RET_CHUNK == 0 and (nbp * seq_p) % TAIL_TILE == 0
    assert seq_s == SUBLANES and (nbs * seq_s) % TAIL_TILE == 0

    ng = norm_g[0][None, :]
    bg = b_gate[0][None, :]
    rng = ret_norm_g[0].reshape(1, D_MODEL)
    cw = conv_w[0]
    cbias = conv_b[0][None, :]
    wro, wco, wmo, wout = _pack_weights(w_ret_o[0], w_conv_o[0], w_mem_o[0], w_out[0])
    fg = final_norm_g[None, :]

    cos_p, sin_p = _rope_tables(jnp.arange(seq_p, dtype=jnp.int32))
    cos_s, sin_s = _rope_tables(PAST_LEN + jnp.arange(seq_s, dtype=jnp.int32))

    xs2d = x_sample.reshape(nbs * seq_s, D_MODEL)
    z, win_pk, h_s = _sample_proj(xs2d, ng, w_in[0])
    z = z.reshape(nbs, seq_s, OFF_GATE)
    tail_weights = (win_pk, bg, wro, wco, wmo, wout, fg)

    mk, mv, mk_bf, mv_bf = _memory_kv(mem_prompt, mem_norm_g[0][None, :], w_mem_kv[0])
    h_p, aret_p, aconv_p, amem_p, ret_p, conv_p = _prompt_branches(
        x_prompt, _decay_scalars(RET_CHUNK), cos_p, sin_p, mk_bf, mv_bf, ng, win_pk, rng, cw, cbias)

    flat_p = lambda a: a.reshape(nbp * seq_p, D_MODEL)
    y_prompt, aret_s, aconv_s, amem_s, ret_s, conv_s = _tail_with_sample(
        flat_p(x_prompt), flat_p(aret_p), flat_p(aconv_p), flat_p(amem_p), flat_p(h_p), *tail_weights,
        _decay_scalars(seq_s), z, cos_s, sin_s, state_ret[0], state_conv[0], cache_mem_k[0], cache_mem_v[0],
        rng, cw, cbias)

    flat_s = lambda a: a.reshape(nbs * seq_s, D_MODEL)
    y_sample = _tail(xs2d, flat_s(aret_s), flat_s(aconv_s), flat_s(amem_s), h_s, *tail_weights)

    return (y_prompt.reshape(x_prompt.shape), y_sample.reshape(x_sample.shape),
            ret_p[None], ret_s[None], conv_p[None], conv_s[None], mk[None], mv[None])
```

```python
import functools

import jax
import jax.numpy as jnp
from jax import lax
from jax.experimental import pallas as pl
from jax.experimental.pallas import tpu as pltpu

F32 = jnp.float32
BF16 = jnp.bfloat16

D_MODEL = 1024
HEADS = 4
HEAD_DIM = D_MODEL // HEADS
ROT_HALF = HEAD_DIM // 2
ROPE_BASE = 10000.0
CONV_WIDTH = 3
MEM_LEN = 256
MEM_PIECE = 64
N_BRANCH = 3
EPS = 1e-6
PAST_LEN = 16384
QK_SCALE = HEAD_DIM ** -0.5

OFF_RQ, OFF_RK, OFF_RV, OFF_RG = 0, 1024, 2048, 3072
OFF_CU, OFF_CB, OFF_CC, OFF_CG = 4096, 5120, 6144, 7168
OFF_MQ, OFF_MG = 8192, 9216
OFF_GATE = 10240
IN_TOTAL = OFF_GATE + N_BRANCH * D_MODEL

SUBLANES = 8
VMEM_LIMIT_BYTES = 60 * 1024 * 1024

PROMPT_TILE = 512
RET_CHUNK = 256
TAIL_TILE = 256
SAMPLE_TAIL_TILE = 512


def _bf(x):
    return x.astype(BF16)


def _dot(a, b):
    return jnp.dot(a, b, preferred_element_type=F32)


def _dot_nt(a, b):
    return lax.dot_general(a, b, (((1,), (1,)), ((), ())), preferred_element_type=F32)


def _dot_tn(a, b):
    return lax.dot_general(a, b, (((0,), (0,)), ((), ())), preferred_element_type=F32)


def _rms(x, g):
    return x * lax.rsqrt(jnp.mean(x * x, axis=-1, keepdims=True) + EPS) * g


def _silu(x):
    return x * jax.nn.sigmoid(x)


def _rope(x, cos, sin):
    x1 = x[:, :ROT_HALF]
    x2 = x[:, ROT_HALF:]
    return jnp.concatenate([x1 * cos - x2 * sin, x1 * sin + x2 * cos], axis=-1)


def _lanes_twice(t):
    return jnp.concatenate([t, t], axis=-1)


def _pack_rows(w_bf):
    return pltpu.bitcast(w_bf, jnp.int32)


def _unpack_rows(w_packed):
    return pltpu.bitcast(w_packed, BF16)


def _fill_decay(dec_ref, inner_s, qdec_s, kdec_s, chunk):
    i = lax.broadcasted_iota(jnp.int32, (chunk, chunk), 0)
    j = lax.broadcasted_iota(jnp.int32, (chunk, chunk), 1)
    diff = (i - j).astype(F32)
    r = lax.broadcasted_iota(jnp.int32, (chunk, ROT_HALF), 0).astype(F32)
    for hd in range(HEADS):
        lg = dec_ref[hd]
        inner_s[hd] = jnp.where(diff >= 0, jnp.exp(lg * jnp.maximum(diff, 0.0)), 0.0)
        qdec_s[hd] = jnp.exp(lg * (r + 1.0))
        kdec_s[hd] = jnp.exp(lg * (chunk - 1.0 - r))


def _conv_branch(cu, cb, cc, cg, cbuf, cw_ref, cbias_ref, rows, cols):
    pre = cc * cu
    cbuf[SUBLANES:SUBLANES + rows, cols] = pre
    conv = (cbias_ref[:, cols]
            + cw_ref[0:1, cols] * cbuf[SUBLANES - 2:SUBLANES - 2 + rows, cols]
            + cw_ref[1:2, cols] * cbuf[SUBLANES - 1:SUBLANES - 1 + rows, cols]
            + cw_ref[2:3, cols] * pre)
    return pre, cb * conv * _silu(cg)


def _softmax_rows(s):
    e = jnp.exp(s - jnp.max(s, axis=-1, keepdims=True))
    return e * (1.0 / jnp.sum(e, axis=-1, keepdims=True))


def _resident(shape, block=None):
    zeros = (0,) * len(shape)
    return pl.BlockSpec(block or shape, lambda *_: zeros, pipeline_mode=pl.Buffered(1))


def _pack_weights_body(*refs):
    n = len(refs) // 2
    for w_ref, out_ref in zip(refs[:n], refs[n:]):
        out_ref[...] = _pack_rows(_bf(w_ref[...]))


def _pack_weights(*ws):
    return pl.pallas_call(
        _pack_weights_body,
        out_shape=[jax.ShapeDtypeStruct((w.shape[0] // 2, w.shape[1]), jnp.int32) for w in ws],
        compiler_params=pltpu.CompilerParams(vmem_limit_bytes=VMEM_LIMIT_BYTES),
        name="pack_weights",
    )(*ws)


def _prompt_branch_body(dec_ref, x_ref, cos_ref, sin_ref, mk_ref, mv_ref, ng_ref, win_ref, rng_ref,
                        cw_ref, cbias_ref,
                        h_ref, aret_ref, aconv_ref, amem_ref, state_ref, nconv_ref,
                        cbuf, inner_s, qdec_s, kdec_s):
    b = pl.program_id(0)
    c = pl.program_id(1)
    tile = x_ref.shape[0]

    @pl.when(jnp.logical_and(b == 0, c == 0))
    def _():
        _fill_decay(dec_ref, inner_s, qdec_s, kdec_s, RET_CHUNK)

    @pl.when(c == 0)
    def _():
        state_ref[...] = jnp.zeros(state_ref.shape, F32)
        cbuf[0:SUBLANES, :] = jnp.zeros((SUBLANES, D_MODEL), F32)

    h = _bf(_rms(x_ref[...], ng_ref[...]))
    h_ref[...] = h
    cos = cos_ref[...]
    sin = sin_ref[...]

    def proj(off):
        return _dot(h, _unpack_rows(win_ref[:, off:off + HEAD_DIM]))

    heads = range(HEADS)
    head_cols = [slice(hd * HEAD_DIM, (hd + 1) * HEAD_DIM) for hd in heads]

    def conv_block(hd):
        lo = hd * HEAD_DIM
        cols = head_cols[hd]
        pre, a_conv = _conv_branch(proj(OFF_CU + lo), proj(OFF_CB + lo), proj(OFF_CC + lo), proj(OFF_CG + lo),
                                   cbuf, cw_ref, cbias_ref, tile, cols)
        cbuf[0:SUBLANES, cols] = pre[tile - SUBLANES:, :]
        nconv_ref[:, cols] = pre[tile - (CONV_WIDTH - 1):, :]
        aconv_ref[:, cols] = _bf(a_conv)

    q = [_bf(_rope(proj(OFF_RQ + hd * HEAD_DIM), cos, sin)) for hd in heads]
    k = [_rope(proj(OFF_RK + hd * HEAD_DIM), cos, sin) * QK_SCALE for hd in heads]
    v = [_bf(proj(OFF_RV + hd * HEAD_DIM)) for hd in heads]
    chunks = [slice(j * RET_CHUNK, (j + 1) * RET_CHUNK) for j in range(tile // RET_CHUNK)]
    scores, cross = [], []
    for rows in chunks:
        for hd in heads:
            state = state_ref[hd]
            kc = k[hd][rows]
            scores.append(_bf(_dot_nt(q[hd][rows], _bf(kc)) * inner_s[hd]))
            cross.append(_dot(q[hd][rows], _bf(state)))
            state_ref[hd] = (state * dec_ref[HEADS + hd]
                             + _dot_tn(_bf(kc * _lanes_twice(kdec_s[hd])), v[hd][rows]))
    conv_block(0)
    ret_o = []
    for hd in heads:
        qdec = _lanes_twice(qdec_s[hd])
        ret_o.append(jnp.concatenate(
            [_dot(scores[j * HEADS + hd], v[hd][rows]) + cross[j * HEADS + hd] * qdec
             for j, rows in enumerate(chunks)], axis=0))
    mq =[_bf(proj(OFF_MQ + hd * HEAD_DIM)) for hd in heads]
    conv_block(1)
    mscores = [_dot_nt(mq[hd], mk_ref[:, head_cols[hd]]) * QK_SCALE for hd in heads]
    for hd in heads:
        cols = head_cols[hd]
        aret_ref[:, cols] = _bf(_rms(ret_o[hd], rng_ref[:, cols]) * _silu(proj(OFF_RG + hd * HEAD_DIM)))
    conv_block(2)
    probs = [_bf(_softmax_rows(mscores[hd])) for hd in heads]
    conv_block(3)
    for hd in heads:
        cols = head_cols[hd]
        om = _dot(probs[hd], mv_ref[:, cols])
        amem_ref[:, cols] = _bf(om * _silu(proj(OFF_MG + hd * HEAD_DIM)))


def _prompt_branches(x, dec, cos, sin, mk_bf, mv_bf, ng, win_pk, rng, cw, cbias):
    nb, seq, _ = x.shape
    tile = PROMPT_TILE
    tok = pl.BlockSpec((None, tile, D_MODEL), lambda b, c: (b, c, 0))
    rot = pl.BlockSpec((tile, ROT_HALF), lambda b, c: (c, 0))
    mem = pl.BlockSpec((None, MEM_LEN, D_MODEL), lambda b, c: (b, 0, 0))
    act = jax.ShapeDtypeStruct(x.shape, BF16)
    return pl.pallas_call(
        _prompt_branch_body,
        grid=(nb, seq // tile),
        in_specs=[pl.BlockSpec(memory_space=pltpu.SMEM), tok, rot, rot, mem, mem,
                  _resident(ng.shape), _resident(win_pk.shape, (D_MODEL // 2, OFF_GATE)), _resident(rng.shape),
                  _resident(cw.shape), _resident(cbias.shape)],
        out_specs=[tok, tok, tok, tok,
                   pl.BlockSpec((None, HEADS, HEAD_DIM, HEAD_DIM), lambda b, c: (b, 0, 0, 0)),
                   pl.BlockSpec((None, CONV_WIDTH - 1, D_MODEL), lambda b, c: (b, 0, 0))],
        out_shape=[act, act, act, act,
                   jax.ShapeDtypeStruct((nb, HEADS, HEAD_DIM, HEAD_DIM), F32),
                   jax.ShapeDtypeStruct((nb, CONV_WIDTH - 1, D_MODEL), F32)],
        scratch_shapes=[pltpu.VMEM((SUBLANES + tile, D_MODEL), F32),
                        pltpu.VMEM((HEADS, RET_CHUNK, RET_CHUNK), F32),
                        pltpu.VMEM((HEADS, RET_CHUNK, ROT_HALF), F32),
                        pltpu.VMEM((HEADS, RET_CHUNK, ROT_HALF), F32)],
        compiler_params=pltpu.CompilerParams(dimension_semantics=("arbitrary", "arbitrary"),
                                             vmem_limit_bytes=VMEM_LIMIT_BYTES),
        name="prompt_branches",
    )(dec, x, cos, sin, mk_bf, mv_bf, ng, win_pk, rng, cw, cbias)


def _prep_body(x_ref, ng_ref, w_ref, mem_ref, mg_ref, wkv_ref,
               z_ref, wpk_ref, h_ref, k_ref, v_ref, kb_ref, vb_ref, wkv_s, *, n_mem):
    n = pl.program_id(0)

    @pl.when(n == 0)
    def _():
        h_ref[...] = _bf(_rms(x_ref[...], ng_ref[...]))
        wkv_s[...] = _bf(wkv_ref[...])

    w = _bf(w_ref[...])
    wpk_ref[...] = _pack_rows(w)

    @pl.when(n < OFF_GATE // w_ref.shape[1])
    def _():
        z_ref[...] = _dot(h_ref[...], w)

    @pl.when(n < n_mem)
    def _():
        kv = _dot(_bf(_rms(mem_ref[...], mg_ref[...])), wkv_s[...])
        k = kv[:, :D_MODEL]
        v = kv[:, D_MODEL:]
        for hd in range(HEADS):
            lo = hd * HEAD_DIM
            k_ref[:, hd, :] = k[:, lo:lo + HEAD_DIM]
            v_ref[:, hd, :] = v[:, lo:lo + HEAD_DIM]
        kb_ref[...] = _bf(k)
        vb_ref[...] = _bf(v)


def _prep(x2d, ng, w_in, mem, mem_g, w_mem_kv):
    ntok = x2d.shape[0]
    n_mem = mem.shape[0]
    blk = D_MODEL
    steps = IN_TOTAL // blk
    z_blocks = OFF_GATE // blk
    assert n_mem <= steps

    def mem_blk(*tail):
        zeros = (0,) * len(tail)
        return pl.BlockSpec((None,) + tail, lambda n: (jnp.minimum(n, n_mem - 1),) + zeros)

    heads_shape = jax.ShapeDtypeStruct((n_mem, MEM_LEN, HEADS, HEAD_DIM), F32)
    return pl.pallas_call(
        functools.partial(_prep_body, n_mem=n_mem),
        grid=(steps,),
        in_specs=[_resident(x2d.shape), _resident(ng.shape),
                  pl.BlockSpec((D_MODEL, blk), lambda n: (0, n)),
                  mem_blk(MEM_LEN, D_MODEL), _resident(mem_g.shape), _resident(w_mem_kv.shape)],
        out_specs=[pl.BlockSpec((ntok, blk), lambda n: (0, jnp.minimum(n, z_blocks - 1))),
                   pl.BlockSpec((D_MODEL // 2, blk), lambda n: (0, n)),
                   pl.BlockSpec(x2d.shape, lambda n: (0, 0)),
                   mem_blk(MEM_LEN, HEADS, HEAD_DIM), mem_blk(MEM_LEN, HEADS, HEAD_DIM),
                   mem_blk(MEM_LEN, D_MODEL), mem_blk(MEM_LEN, D_MODEL)],
        out_shape=[jax.ShapeDtypeStruct((ntok, OFF_GATE), F32),
                   jax.ShapeDtypeStruct((D_MODEL // 2, IN_TOTAL), jnp.int32),
                   jax.ShapeDtypeStruct(x2d.shape, BF16),
                   heads_shape, heads_shape,
                   jax.ShapeDtypeStruct(mem.shape, BF16), jax.ShapeDtypeStruct(mem.shape, BF16)],
        scratch_shapes=[pltpu.VMEM(w_mem_kv.shape, BF16)],
        compiler_params=pltpu.CompilerParams(dimension_semantics=("arbitrary",),
                                             vmem_limit_bytes=VMEM_LIMIT_BYTES),
        name="prep",
    )(x2d, ng, w_in, mem, mem_g, w_mem_kv)


def _tail_stages(x_ref, aret_ref, aconv_ref, amem_ref, h_ref, wg_refs, bg_ref, wro_ref, wco_ref, wmo_ref,
                 wout_ref, fg_ref, y_ref):
    live = {}
    act_refs = (aret_ref, aconv_ref, amem_ref)
    w_refs = (wro_ref, wco_ref, wmo_ref)

    def gate(branch):
        def run(between=()):
            nblk = max(len(between), 1)
            width = D_MODEL // nblk
            parts = []
            for i in range(nblk):
                cols = slice(i * width, (i + 1) * width)
                logits = _dot(h_ref[...], _unpack_rows(wg_refs[branch][:, cols]))
                parts.append(jax.nn.sigmoid(logits + bg_ref[:, branch * D_MODEL + i * width:
                                                            branch * D_MODEL + (i + 1) * width]))
                if between:
                    between[i]()
            live["gate"] = jnp.concatenate(parts, axis=-1)
        return run

    def project(branch):
        def run(between=()):
            nblk = max(len(between), 1)
            width = D_MODEL // nblk
            parts = []
            for i in range(nblk):
                cols = slice(i * width, (i + 1) * width)
                parts.append(live["gate"][:, cols] * _dot(act_refs[branch][...], _unpack_rows(w_refs[branch][:, cols])))
                if between:
                    between[i]()
            y_branch = jnp.concatenate(parts, axis=-1)
            live["merged"] = y_branch if branch == 0 else live["merged"] + y_branch
        return run

    def out_project():
        live["y"] = x_ref[...] + _dot(_bf(live["merged"]), _unpack_rows(wout_ref[...]))

    def finish():
        y_ref[...] = _rms(live["y"], fg_ref[...])

    stages = []
    for branch in range(N_BRANCH):
        stages += [gate(branch), project(branch)]
    return stages + [out_project, finish]


def _sample_stages(z_ref, cos_ref, sin_ref, state_ref, cstate_ref, mk_ref, mv_ref, rng_ref, cw_ref, cbias_ref,
                   aret_ref, aconv_ref, amem_ref, nstate_ref, nconv_ref, cbuf, inner_s, qdec_s, kdec_s, dec_ref):
    group, seq, _ = z_ref.shape
    pairs = [(g, hd) for g in range(group) for hd in range(HEADS)]
    live = {}

    def col(g, off, width=HEAD_DIM):
        return z_ref[g, :, off:off + width]

    def ret_inputs():
        cos = cos_ref[...]
        sin = sin_ref[...]
        live["q"] = {p: _bf(_rope(col(p[0], OFF_RQ + p[1] * HEAD_DIM), cos, sin)) for p in pairs}
        live["k"] = {p: _rope(col(p[0], OFF_RK + p[1] * HEAD_DIM), cos, sin) * QK_SCALE for p in pairs}
        live["v"] = {p: _bf(col(p[0], OFF_RV + p[1] * HEAD_DIM)) for p in pairs}

    def ret_products():
        live["scores"], live["cross"] = {}, {}
        for p in pairs:
            g, hd = p
            state = state_ref[g, hd]
            k = live["k"][p]
            live["scores"][p] = _bf(_dot_nt(live["q"][p], _bf(k)) * inner_s[hd])
            live["cross"][p] = _dot(live["q"][p], _bf(state))
            nstate_ref[g, hd] = (state * dec_ref[HEADS + hd]
                                 + _dot_tn(_bf(k * _lanes_twice(kdec_s[hd])), live["v"][p]))

    def ret_outputs():
        for p in pairs:
            g, hd = p
            lo = hd * HEAD_DIM
            o = _dot(live["scores"][p], live["v"][p]) + live["cross"][p] * _lanes_twice(qdec_s[hd])
            o = _rms(o, rng_ref[:, lo:lo + HEAD_DIM])
            aret_ref[g, :, lo:lo + HEAD_DIM] = _bf(o * _silu(col(g, OFF_RG + lo)))

    def conv():
        for g in range(group):
            base = g * (SUBLANES + seq)
            cbuf[base:base + SUBLANES, :] = jnp.zeros((SUBLANES, D_MODEL), F32)
            cbuf[base + SUBLANES - (CONV_WIDTH - 1):base + SUBLANES, :] = cstate_ref[g]
            pre, a_conv = _conv_branch(col(g, OFF_CU, D_MODEL), col(g, OFF_CB, D_MODEL), col(g, OFF_CC, D_MODEL),
                                       col(g, OFF_CG, D_MODEL), cbuf.at[base:base + SUBLANES + seq], cw_ref,
                                       cbias_ref, seq, slice(0, D_MODEL))
            nconv_ref[g] = pre[seq - (CONV_WIDTH - 1):, :]
            aconv_ref[g] = _bf(a_conv)

    def mem_score_piece(g, j):
        def run():
            if j == 0:
                live.setdefault("ms", {})[g] = []
                live.setdefault("mq", {})[g] = _bf(
                    jnp.concatenate([col(g, OFF_MQ + hd * HEAD_DIM) for hd in range(HEADS)], axis=0))
            keys = mk_ref[g, j * MEM_PIECE:(j + 1) * MEM_PIECE].reshape(MEM_PIECE * HEADS, HEAD_DIM)
            live["ms"][g].append(_dot_nt(live["mq"][g], _bf(keys)) * QK_SCALE)
        return run

    def mem_probs():
        live["mp"] = []
        for g in range(group):
            s = jnp.concatenate(live["ms"][g], axis=-1)
            row_head = lax.broadcasted_iota(jnp.int32, s.shape, 0) // seq
            col_head = lax.broadcasted_iota(jnp.int32, s.shape, 1) % HEADS
            live["mp"].append(_bf(_softmax_rows(jnp.where(row_head == col_head, s, -jnp.inf))))

    def mem_outputs():
        for g in range(group):
            mv2d = _bf(mv_ref[g].reshape(MEM_LEN * HEADS, HEAD_DIM))
            om_all = _dot(live["mp"][g], mv2d)
            for hd in range(HEADS):
                lo = hd * HEAD_DIM
                om = om_all[hd * seq:(hd + 1) * seq, :]
                amem_ref[g, :, lo:lo + HEAD_DIM] = _bf(om * _silu(col(g, OFF_MG + lo)))

    pieces = [[mem_score_piece(g, j) for j in range(MEM_LEN // MEM_PIECE)] for g in range(group)]
    assert group == 2
    return [(ret_inputs, ()), (ret_products, ()), (None, pieces[0]), (ret_outputs, pieces[1]),
            (mem_probs, ()), (conv, ()), (mem_outputs, ())]


def _tail_body(x_ref, aret_ref, aconv_ref, amem_ref, h_ref, wg0_ref, wg1_ref, wg2_ref, bg_ref,
               wro_ref, wco_ref, wmo_ref, wout_ref, fg_ref, y_ref):
    for stage in _tail_stages(x_ref, aret_ref, aconv_ref, amem_ref, h_ref, (wg0_ref, wg1_ref, wg2_ref), bg_ref,
                              wro_ref, wco_ref, wmo_ref, wout_ref, fg_ref, y_ref):
        stage()


def _tail_with_sample_body(x_ref, aret_ref, aconv_ref, amem_ref, h_ref, wg0_ref, wg1_ref, wg2_ref, bg_ref,
                           wro_ref, wco_ref, wmo_ref, wout_ref, fg_ref,
                           dec_ref, z_ref, cos_ref, sin_ref, state_ref, cstate_ref, mk_ref, mv_ref, rng_ref,
                           cw_ref, cbias_ref,
                           y_ref, saret_ref, saconv_ref, samem_ref, nstate_ref, nconv_ref,
                           cbuf, inner_s, qdec_s, kdec_s):
    @pl.when(pl.program_id(0) == 0)
    def _():
        _fill_decay(dec_ref, inner_s, qdec_s, kdec_s, z_ref.shape[1])

    tail = _tail_stages(x_ref, aret_ref, aconv_ref, amem_ref, h_ref, (wg0_ref, wg1_ref, wg2_ref), bg_ref,
                        wro_ref, wco_ref, wmo_ref, wout_ref, fg_ref, y_ref)
    sample = _sample_stages(z_ref, cos_ref, sin_ref, state_ref, cstate_ref, mk_ref, mv_ref, rng_ref, cw_ref,
                            cbias_ref, saret_ref, saconv_ref, samem_ref, nstate_ref, nconv_ref,
                            cbuf, inner_s, qdec_s, kdec_s, dec_ref)
    assert len(sample) == len(tail) - 1
    for i, tail_stage in enumerate(tail):
        before, between = sample[i] if i < len(sample) - 1 else sample[-1] if i == len(tail) - 1 else (None, ())
        if before is not None:
            before()
        if between:
            tail_stage(between=between)
        else:
            tail_stage()


def _tail_specs(tile, bg, wro, wco, wmo, wout, fg):
    tok = pl.BlockSpec((tile, D_MODEL), lambda i: (i, 0))

    def gate_cols(branch):
        blk = OFF_GATE // D_MODEL + branch
        return pl.BlockSpec((D_MODEL // 2, D_MODEL), lambda i: (0, blk), pipeline_mode=pl.Buffered(1))

    in_specs = [tok, tok, tok, tok,
                tok, gate_cols(0), gate_cols(1), gate_cols(2), _resident(bg.shape),
                _resident(wro.shape), _resident(wco.shape), _resident(wmo.shape), _resident(wout.shape),
                _resident(fg.shape)]
    return tok, in_specs


def _tail(x2d, aret, aconv, amem, h2d, win_pk, bg, wro, wco, wmo, wout, fg):
    tok, in_specs = _tail_specs(SAMPLE_TAIL_TILE, bg, wro, wco, wmo, wout, fg)
    return pl.pallas_call(
        _tail_body,
        grid=(x2d.shape[0] // SAMPLE_TAIL_TILE,),
        in_specs=in_specs,
        out_specs=tok,
        out_shape=jax.ShapeDtypeStruct(x2d.shape, F32),
        compiler_params=pltpu.CompilerParams(dimension_semantics=("arbitrary",),
                                             vmem_limit_bytes=VMEM_LIMIT_BYTES),
        name="tail",
    )(x2d, aret, aconv, amem, h2d, win_pk, win_pk, win_pk, bg, wro, wco, wmo, wout, fg)


def _tail_with_sample(x2d, aret, aconv, amem, h2d, win_pk, bg, wro, wco, wmo, wout, fg,
                      dec, z, cos, sin, state, cstate, mk, mv, rng, cw, cbias):
    steps = x2d.shape[0] // TAIL_TILE
    nb, seq, _ = z.shape
    grp = nb // steps
    assert grp * steps == nb
    tok, in_specs = _tail_specs(TAIL_TILE, bg, wro, wco, wmo, wout, fg)

    def per_req(*tail):
        zeros = (0,) * len(tail)
        return pl.BlockSpec((grp,) + tail, lambda i: (i,) + zeros)

    def const(shape):
        zeros = (0,) * len(shape)
        return pl.BlockSpec(shape, lambda i: zeros)

    act = jax.ShapeDtypeStruct((nb, seq, D_MODEL), BF16)
    return pl.pallas_call(
        _tail_with_sample_body,
        grid=(steps,),
        in_specs=in_specs + [pl.BlockSpec(memory_space=pltpu.SMEM),
                             per_req(seq, OFF_GATE), const(cos.shape), const(sin.shape),
                             per_req(HEADS, HEAD_DIM, HEAD_DIM), per_req(CONV_WIDTH - 1, D_MODEL),
                             per_req(MEM_LEN, HEADS, HEAD_DIM), per_req(MEM_LEN, HEADS, HEAD_DIM),
                             const(rng.shape), const(cw.shape), const(cbias.shape)],
        out_specs=[tok, per_req(seq, D_MODEL), per_req(seq, D_MODEL), per_req(seq, D_MODEL),
                   per_req(HEADS, HEAD_DIM, HEAD_DIM), per_req(CONV_WIDTH - 1, D_MODEL)],
        out_shape=[jax.ShapeDtypeStruct(x2d.shape, F32), act, act, act,
                   jax.ShapeDtypeStruct(state.shape, F32),
                   jax.ShapeDtypeStruct(cstate.shape, F32)],
        scratch_shapes=[pltpu.VMEM((grp * (SUBLANES + seq), D_MODEL), F32),
                        pltpu.VMEM((HEADS, seq, seq), F32),
                        pltpu.VMEM((HEADS, seq, ROT_HALF), F32),
                        pltpu.VMEM((HEADS, seq, ROT_HALF), F32)],
        compiler_params=pltpu.CompilerParams(dimension_semantics=("arbitrary",),
                                             vmem_limit_bytes=VMEM_LIMIT_BYTES),
        name="tail_with_sample",
    )(x2d, aret, aconv, amem, h2d, win_pk, win_pk, win_pk, bg, wro, wco, wmo, wout, fg,
      dec, z, cos, sin, state, cstate, mk, mv, rng, cw, cbias)


def _rope_tables(pos):
    inv = ROPE_BASE ** (-jnp.arange(ROT_HALF, dtype=F32) / ROT_HALF)
    ang = pos.astype(F32)[:, None] * inv[None, :]
    return jnp.cos(ang), jnp.sin(ang)


def _decay_scalars(chunk):
    lg = jnp.log1p(-jnp.exp2(-5.0 - jnp.arange(HEADS, dtype=F32)))
    return jnp.concatenate([lg, jnp.exp(lg * chunk)])


def kernel(x_prompt, x_sample, state_ret, state_conv, cache_mem_k, cache_mem_v, mem_prompt, norm_g, w_in, b_gate, ret_norm_g, conv_w, conv_b, w_ret_o, w_conv_o, w_mem_o, w_out, mem_norm_g, w_mem_kv, final_norm_g):
    assert norm_g.shape[0] == 1, "single-layer trunk"
    nbp, seq_p, _ = x_prompt.shape
    nbs, seq_s, _ = x_sample.shape
    assert seq_p % PROMPT_TILE == 0 and PROMPT_TILE % RET_CHUNK == 0 and (nbp * seq_p) % TAIL_TILE == 0
    assert seq_s == SUBLANES and (nbs * seq_s) % SAMPLE_TAIL_TILE == 0

    ng = norm_g[0][None, :]
    bg = b_gate[0][None, :]
    rng = ret_norm_g[0].reshape(1, D_MODEL)
    cw = conv_w[0]
    cbias = conv_b[0][None, :]
    wro, wco, wmo, wout = _pack_weights(w_ret_o[0], w_conv_o[0], w_mem_o[0], w_out[0])
    fg = final_norm_g[None, :]

    cos_p, sin_p = _rope_tables(jnp.arange(seq_p, dtype=jnp.int32))
    cos_s, sin_s = _rope_tables(PAST_LEN + jnp.arange(seq_s, dtype=jnp.int32))

    xs2d = x_sample.reshape(nbs * seq_s, D_MODEL)
    z, win_pk, h_s, mk, mv, mk_bf, mv_bf = _prep(xs2d, ng, w_in[0], mem_prompt, mem_norm_g[0][None, :],
                                                 w_mem_kv[0])
    z = z.reshape(nbs, seq_s, OFF_GATE)
    tail_weights = (win_pk, bg, wro, wco, wmo, wout, fg)

    h_p, aret_p, aconv_p, amem_p, ret_p, conv_p = _prompt_branches(
        x_prompt, _decay_scalars(RET_CHUNK), cos_p, sin_p, mk_bf, mv_bf, ng, win_pk, rng, cw, cbias)

    flat_p = lambda a: a.reshape(nbp * seq_p, D_MODEL)
    y_prompt, aret_s, aconv_s, amem_s, ret_s, conv_s = _tail_with_sample(
        flat_p(x_prompt), flat_p(aret_p), flat_p(aconv_p), flat_p(amem_p), flat_p(h_p), *tail_weights,
        _decay_scalars(seq_s), z, cos_s, sin_s, state_ret[0], state_conv[0], cache_mem_k[0], cache_mem_v[0],
        rng, cw, cbias)

    flat_s = lambda a: a.reshape(nbs * seq_s, D_MODEL)
    y_sample = _tail(xs2d, flat_s(aret_s), flat_s(aconv_s), flat_s(amem_s), h_s, *tail_weights)

    return (y_prompt.reshape(x_prompt.shape), y_sample.reshape(x_sample.shape),
            ret_p[None], ret_s[None], conv_p[None], conv_s[None], mk[None], mv[None])
```

```python
import functools

import jax
import jax.numpy as jnp
from jax import lax
from jax.experimental import pallas as pl
from jax.experimental.pallas import tpu as pltpu

F32 = jnp.float32
BF16 = jnp.bfloat16

D_MODEL = 1024
HEADS = 4
HEAD_DIM = D_MODEL // HEADS
ROT_HALF = HEAD_DIM // 2
ROPE_BASE = 10000.0
CONV_WIDTH = 3
MEM_LEN = 256
MEM_PIECE = 64
N_BRANCH = 3
EPS = 1e-6
PAST_LEN = 16384
QK_SCALE = HEAD_DIM ** -0.5

OFF_RQ, OFF_RK, OFF_RV, OFF_RG = 0, 1024, 2048, 3072
OFF_CU, OFF_CB, OFF_CC, OFF_CG = 4096, 5120, 6144, 7168
OFF_MQ, OFF_MG = 8192, 9216
OFF_GATE = 10240
IN_TOTAL = OFF_GATE + N_BRANCH * D_MODEL

SUBLANES = 8
VMEM_LIMIT_BYTES = 60 * 1024 * 1024

PROMPT_TILE = 512
RET_CHUNK = 256
TAIL_TILE = 256
SAMPLE_TAIL_TILE = 512


def _bf(x):
    return x.astype(BF16)


def _dot(a, b):
    return jnp.dot(a, b, preferred_element_type=F32)


def _dot_nt(a, b):
    return lax.dot_general(a, b, (((1,), (1,)), ((), ())), preferred_element_type=F32)


def _dot_tn(a, b):
    return lax.dot_general(a, b, (((0,), (0,)), ((), ())), preferred_element_type=F32)


def _rms(x, g):
    return x * lax.rsqrt(jnp.mean(x * x, axis=-1, keepdims=True) + EPS) * g


def _silu(x):
    return x * jax.nn.sigmoid(x)


def _rope(x, cos, sin):
    x1 = x[:, :ROT_HALF]
    x2 = x[:, ROT_HALF:]
    return jnp.concatenate([x1 * cos - x2 * sin, x1 * sin + x2 * cos], axis=-1)


def _lanes_twice(t):
    return jnp.concatenate([t, t], axis=-1)


def _pack_rows(w_bf):
    return pltpu.bitcast(w_bf, jnp.int32)


def _unpack_rows(w_packed):
    return pltpu.bitcast(w_packed, BF16)


def _fill_decay(dec_ref, inner_s, qdec_s, kdec_s, chunk):
    i = lax.broadcasted_iota(jnp.int32, (chunk, chunk), 0)
    j = lax.broadcasted_iota(jnp.int32, (chunk, chunk), 1)
    diff = (i - j).astype(F32)
    r = lax.broadcasted_iota(jnp.int32, (chunk, ROT_HALF), 0).astype(F32)
    for hd in range(HEADS):
        lg = dec_ref[hd]
        inner_s[hd] = jnp.where(diff >= 0, jnp.exp(lg * jnp.maximum(diff, 0.0)), 0.0)
        qdec_s[hd] = jnp.exp(lg * (r + 1.0))
        kdec_s[hd] = jnp.exp(lg * (chunk - 1.0 - r))


def _conv_branch(cu, cb, cc, cg, cbuf, cw_ref, cbias_ref, rows, cols):
    pre = cc * cu
    cbuf[SUBLANES:SUBLANES + rows, cols] = pre
    conv = (cbias_ref[:, cols]
            + cw_ref[0:1, cols] * cbuf[SUBLANES - 2:SUBLANES - 2 + rows, cols]
            + cw_ref[1:2, cols] * cbuf[SUBLANES - 1:SUBLANES - 1 + rows, cols]
            + cw_ref[2:3, cols] * pre)
    return pre, cb * conv * _silu(cg)


def _softmax_rows(s):
    e = jnp.exp(s - jnp.max(s, axis=-1, keepdims=True))
    return e * (1.0 / jnp.sum(e, axis=-1, keepdims=True))


def _resident(shape, block=None):
    zeros = (0,) * len(shape)
    return pl.BlockSpec(block or shape, lambda *_: zeros, pipeline_mode=pl.Buffered(1))


def _pack_weights_body(*refs):
    n = len(refs) // 2
    for w_ref, out_ref in zip(refs[:n], refs[n:]):
        out_ref[...] = _pack_rows(_bf(w_ref[...]))


def _pack_weights(*ws):
    return pl.pallas_call(
        _pack_weights_body,
        out_shape=[jax.ShapeDtypeStruct((w.shape[0] // 2, w.shape[1]), jnp.int32) for w in ws],
        compiler_params=pltpu.CompilerParams(vmem_limit_bytes=VMEM_LIMIT_BYTES),
        name="pack_weights",
    )(*ws)


def _prompt_branch_body(dec_ref, x_ref, cos_ref, sin_ref, mk_ref, mv_ref, ng_ref, win_ref, rng_ref,
                        cw_ref, cbias_ref,
                        h_ref, aret_ref, aconv_ref, amem_ref, state_ref, nconv_ref,
                        cbuf, inner_s, qdec_s, kdec_s):
    b = pl.program_id(0)
    c = pl.program_id(1)
    tile = x_ref.shape[0]

    @pl.when(jnp.logical_and(b == 0, c == 0))
    def _():
        _fill_decay(dec_ref, inner_s, qdec_s, kdec_s, RET_CHUNK)

    @pl.when(c == 0)
    def _():
        state_ref[...] = jnp.zeros(state_ref.shape, F32)
        cbuf[0:SUBLANES, :] = jnp.zeros((SUBLANES, D_MODEL), F32)

    h = _bf(_rms(x_ref[...], ng_ref[...]))
    h_ref[...] = h
    cos = cos_ref[...]
    sin = sin_ref[...]

    def proj(off):
        return _dot(h, _unpack_rows(win_ref[:, off:off + HEAD_DIM]))

    heads = range(HEADS)
    head_cols = [slice(hd * HEAD_DIM, (hd + 1) * HEAD_DIM) for hd in heads]

    def conv_block(hd):
        lo = hd * HEAD_DIM
        cols = head_cols[hd]
        pre, a_conv = _conv_branch(proj(OFF_CU + lo), proj(OFF_CB + lo), proj(OFF_CC + lo), proj(OFF_CG + lo),
                                   cbuf, cw_ref, cbias_ref, tile, cols)
        cbuf[0:SUBLANES, cols] = pre[tile - SUBLANES:, :]
        nconv_ref[:, cols] = pre[tile - (CONV_WIDTH - 1):, :]
        aconv_ref[:, cols] = _bf(a_conv)

    q = [_bf(_rope(proj(OFF_RQ + hd * HEAD_DIM), cos, sin)) for hd in heads]
    k = [_rope(proj(OFF_RK + hd * HEAD_DIM), cos, sin) * QK_SCALE for hd in heads]
    v = [_bf(proj(OFF_RV + hd * HEAD_DIM)) for hd in heads]
    chunks = [slice(j * RET_CHUNK, (j + 1) * RET_CHUNK) for j in range(tile // RET_CHUNK)]
    scores, cross = [], []
    for rows in chunks:
        for hd in heads:
            state = state_ref[hd]
            kc = k[hd][rows]
            scores.append(_bf(_dot_nt(q[hd][rows], _bf(kc)) * inner_s[hd]))
            cross.append(_dot(q[hd][rows], _bf(state)))
            state_ref[hd] = (state * dec_ref[HEADS + hd]
                             + _dot_tn(_bf(kc * _lanes_twice(kdec_s[hd])), v[hd][rows]))
    conv_block(0)
    ret_o = []
    for hd in heads:
        qdec = _lanes_twice(qdec_s[hd])
        ret_o.append(jnp.concatenate(
            [_dot(scores[j * HEADS + hd], v[hd][rows]) + cross[j * HEADS + hd] * qdec
             for j, rows in enumerate(chunks)], axis=0))
    mq = [_bf(proj(OFF_MQ + hd * HEAD_DIM)) for hd in heads]
    conv_block(1)
    mscores = [_dot_nt(mq[hd], mk_ref[:, head_cols[hd]]) * QK_SCALE for hd in heads]
    for hd in heads:
        cols = head_cols[hd]
        aret_ref[:, cols] = _bf(_rms(ret_o[hd], rng_ref[:, cols]) * _silu(proj(OFF_RG + hd * HEAD_DIM)))
    conv_block(2)
    probs = [_bf(_softmax_rows(mscores[hd])) for hd in heads]
    conv_block(3)
    for hd in heads:
        cols = head_cols[hd]
        om = _dot(probs[hd], mv_ref[:, cols])
        amem_ref[:, cols] = _bf(om * _silu(proj(OFF_MG + hd * HEAD_DIM)))


def _prompt_branches(x, dec, cos, sin, mk_bf, mv_bf, ng, win_pk, rng, cw, cbias):
    nb, seq, _ = x.shape
    tile = PROMPT_TILE
    tok = pl.BlockSpec((None, tile, D_MODEL), lambda b, c: (b, c, 0))
    rot = pl.BlockSpec((tile, ROT_HALF), lambda b, c: (c, 0))
    mem = pl.BlockSpec((None, MEM_LEN, D_MODEL), lambda b, c: (b, 0, 0))
    act = jax.ShapeDtypeStruct(x.shape, BF16)
    return pl.pallas_call(
        _prompt_branch_body,
        grid=(nb, seq // tile),
        in_specs=[pl.BlockSpec(memory_space=pltpu.SMEM), tok, rot, rot, mem, mem,
                  _resident(ng.shape), _resident(win_pk.shape, (D_MODEL // 2, OFF_GATE)), _resident(rng.shape),
                  _resident(cw.shape), _resident(cbias.shape)],
        out_specs=[tok, tok, tok, tok,
                   pl.BlockSpec((None, HEADS, HEAD_DIM, HEAD_DIM), lambda b, c: (b, 0, 0, 0)),
                   pl.BlockSpec((None, CONV_WIDTH - 1, D_MODEL), lambda b, c: (b, 0, 0))],
        out_shape=[act, act, act, act,
                   jax.ShapeDtypeStruct((nb, HEADS, HEAD_DIM, HEAD_DIM), F32),
                   jax.ShapeDtypeStruct((nb, CONV_WIDTH - 1, D_MODEL), F32)],
        scratch_shapes=[pltpu.VMEM((SUBLANES + tile, D_MODEL), F32),
                        pltpu.VMEM((HEADS, RET_CHUNK, RET_CHUNK), F32),
                        pltpu.VMEM((HEADS, RET_CHUNK, ROT_HALF), F32),
                        pltpu.VMEM((HEADS, RET_CHUNK, ROT_HALF), F32)],
        compiler_params=pltpu.CompilerParams(dimension_semantics=("arbitrary", "arbitrary"),
                                             vmem_limit_bytes=VMEM_LIMIT_BYTES),
        name="prompt_branches",
    )(dec, x, cos, sin, mk_bf, mv_bf, ng, win_pk, rng, cw, cbias)


def _prep_body(x_ref, ng_ref, w_ref, mem_ref, mg_ref, wkv_ref,
               z_ref, wpk_ref, h_ref, k_ref, v_ref, kb_ref, vb_ref, wkv_s, *, n_mem):
    n = pl.program_id(0)

    @pl.when(n == 0)
    def _():
        h_ref[...] = _bf(_rms(x_ref[...], ng_ref[...]))
        wkv_s[...] = _bf(wkv_ref[...])

    w = _bf(w_ref[...])
    wpk_ref[...] = _pack_rows(w)

    @pl.when(n < OFF_GATE // w_ref.shape[1])
    def _():
        z_ref[...] = _dot(h_ref[...], w)

    @pl.when(n < n_mem)
    def _():
        kv = _dot(_bf(_rms(mem_ref[...], mg_ref[...])), wkv_s[...])
        k = kv[:, :D_MODEL]
        v = kv[:, D_MODEL:]
        for hd in range(HEADS):
            lo = hd * HEAD_DIM
            k_ref[:, hd, :] = k[:, lo:lo + HEAD_DIM]
            v_ref[:, hd, :] = v[:, lo:lo + HEAD_DIM]
        kb_ref[...] = _bf(k)
        vb_ref[...] = _bf(v)


def _prep(x2d, ng, w_in, mem, mem_g, w_mem_kv):
    ntok = x2d.shape[0]
    n_mem = mem.shape[0]
    blk = D_MODEL
    steps = IN_TOTAL // blk
    z_blocks = OFF_GATE // blk
    assert n_mem <= steps

    def mem_blk(*tail):
        zeros = (0,) * len(tail)
        return pl.BlockSpec((None,) + tail, lambda n: (jnp.minimum(n, n_mem - 1),) + zeros)

    heads_shape = jax.ShapeDtypeStruct((n_mem, MEM_LEN, HEADS, HEAD_DIM), F32)
    return pl.pallas_call(
        functools.partial(_prep_body, n_mem=n_mem),
        grid=(steps,),
        in_specs=[_resident(x2d.shape), _resident(ng.shape),
                  pl.BlockSpec((D_MODEL, blk), lambda n: (0, n)),
                  mem_blk(MEM_LEN, D_MODEL), _resident(mem_g.shape), _resident(w_mem_kv.shape)],
        out_specs=[pl.BlockSpec((ntok, blk), lambda n: (0, jnp.minimum(n, z_blocks - 1))),
                   pl.BlockSpec((D_MODEL // 2, blk), lambda n: (0, n)),
                   pl.BlockSpec(x2d.shape, lambda n: (0, 0)),
                   mem_blk(MEM_LEN, HEADS, HEAD_DIM), mem_blk(MEM_LEN, HEADS, HEAD_DIM),
                   mem_blk(MEM_LEN, D_MODEL), mem_blk(MEM_LEN, D_MODEL)],
        out_shape=[jax.ShapeDtypeStruct((ntok, OFF_GATE), F32),
                   jax.ShapeDtypeStruct((D_MODEL // 2, IN_TOTAL), jnp.int32),
                   jax.ShapeDtypeStruct(x2d.shape, BF16),
                   heads_shape, heads_shape,
                   jax.ShapeDtypeStruct(mem.shape, BF16), jax.ShapeDtypeStruct(mem.shape, BF16)],
        scratch_shapes=[pltpu.VMEM(w_mem_kv.shape, BF16)],
        compiler_params=pltpu.CompilerParams(dimension_semantics=("arbitrary",),
                                             vmem_limit_bytes=VMEM_LIMIT_BYTES),
        name="prep",
    )(x2d, ng, w_in, mem, mem_g, w_mem_kv)


def _tail_stages(x_ref, aret_ref, aconv_ref, amem_ref, h_ref, wg_refs, bg_ref, wro_ref, wco_ref, wmo_ref,
                 wout_ref, fg_ref, y_ref, carry_ref=None):
    live = {}
    act_refs = (aret_ref, aconv_ref, amem_ref)
    w_refs = (wro_ref, wco_ref, wmo_ref)

    def gate(branch):
        def run(between=()):
            nblk = max(len(between), 1)
            width = D_MODEL // nblk
            parts = []
            for i in range(nblk):
                cols = slice(i * width, (i + 1) * width)
                logits = _dot(h_ref[...], _unpack_rows(wg_refs[branch][:, cols]))
                parts.append(jax.nn.sigmoid(logits + bg_ref[:, branch * D_MODEL + i * width:
                                                            branch * D_MODEL + (i + 1) * width]))
                if between:
                    between[i]()
            live["gate"] = jnp.concatenate(parts, axis=-1)
        return run

    def project(branch):
        def run(between=()):
            nblk = max(len(between), 1)
            width = D_MODEL // nblk
            parts = []
            for i in range(nblk):
                cols = slice(i * width, (i + 1) * width)
                parts.append(live["gate"][:, cols] * _dot(act_refs[branch][...], _unpack_rows(w_refs[branch][:, cols])))
                if between:
                    between[i]()
            y_branch = jnp.concatenate(parts, axis=-1)
            live["merged"] = y_branch if branch == 0 else live["merged"] + y_branch
        return run

    def out_project():
        y = x_ref[...] + _dot(_bf(live["merged"]), _unpack_rows(wout_ref[...]))
        if carry_ref is None:
            live["y"] = y
        else:
            carry_ref[...] = y

    def finish():
        y_ref[...] = _rms(live["y"] if carry_ref is None else carry_ref[...], fg_ref[...])

    stages = []
    for branch in range(N_BRANCH):
        stages += [gate(branch), project(branch)]
    return stages + [out_project, finish]


def _sample_stages(z_ref, cos_ref, sin_ref, state_ref, cstate_ref, mk_ref, mv_ref, rng_ref, cw_ref, cbias_ref,
                   aret_ref, aconv_ref, amem_ref, nstate_ref, nconv_ref, cbuf, inner_s, qdec_s, kdec_s, dec_ref):
    group, seq, _ = z_ref.shape
    pairs = [(g, hd) for g in range(group) for hd in range(HEADS)]
    live = {}

    def col(g, off, width=HEAD_DIM):
        return z_ref[g, :, off:off + width]

    def ret_inputs():
        cos = cos_ref[...]
        sin = sin_ref[...]
        live["q"] = {p: _bf(_rope(col(p[0], OFF_RQ + p[1] * HEAD_DIM), cos, sin)) for p in pairs}
        live["k"] = {p: _rope(col(p[0], OFF_RK + p[1] * HEAD_DIM), cos, sin) * QK_SCALE for p in pairs}
        live["v"] = {p: _bf(col(p[0], OFF_RV + p[1] * HEAD_DIM)) for p in pairs}

    def ret_piece(p):
        def run():
            g, hd = p
            state = state_ref[g, hd]
            k = live["k"][p]
            live.setdefault("scores", {})[p] = _bf(_dot_nt(live["q"][p], _bf(k)) * inner_s[hd])
            live.setdefault("cross", {})[p] = _dot(live["q"][p], _bf(state))
            nstate_ref[g, hd] = (state * dec_ref[HEADS + hd]
                                 + _dot_tn(_bf(k * _lanes_twice(kdec_s[hd])), live["v"][p]))
        return run

    def ret_outputs():
        for p in pairs:
            g, hd = p
            lo = hd * HEAD_DIM
            o = _dot(live["scores"][p], live["v"][p]) + live["cross"][p] * _lanes_twice(qdec_s[hd])
            o = _rms(o, rng_ref[:, lo:lo + HEAD_DIM])
            aret_ref[g, :, lo:lo + HEAD_DIM] = _bf(o * _silu(col(g, OFF_RG + lo)))

    def conv():
        for g in range(group):
            base = g * (SUBLANES + seq)
            cbuf[base:base + SUBLANES, :] = jnp.zeros((SUBLANES, D_MODEL), F32)
            cbuf[base + SUBLANES - (CONV_WIDTH - 1):base + SUBLANES, :] = cstate_ref[g]
            pre, a_conv = _conv_branch(col(g, OFF_CU, D_MODEL), col(g, OFF_CB, D_MODEL), col(g, OFF_CC, D_MODEL),
                                       col(g, OFF_CG, D_MODEL), cbuf.at[base:base + SUBLANES + seq], cw_ref,
                                       cbias_ref, seq, slice(0, D_MODEL))
            nconv_ref[g] = pre[seq - (CONV_WIDTH - 1):, :]
            aconv_ref[g] = _bf(a_conv)

    def mem_score_piece(g, j):
        def run():
            if j == 0:
                live.setdefault("ms", {})[g] = []
                live.setdefault("mq", {})[g] = _bf(
                    jnp.concatenate([col(g, OFF_MQ + hd * HEAD_DIM) for hd in range(HEADS)], axis=0))
            keys = mk_ref[g, j * MEM_PIECE:(j + 1) * MEM_PIECE].reshape(MEM_PIECE * HEADS, HEAD_DIM)
            live["ms"][g].append(_dot_nt(live["mq"][g], _bf(keys)) * QK_SCALE)
        return run

    def mem_probs():
        live["mp"] = []
        for g in range(group):
            s = jnp.concatenate(live["ms"][g], axis=-1)
            row_head = lax.broadcasted_iota(jnp.int32, s.shape, 0) // seq
            col_head = lax.broadcasted_iota(jnp.int32, s.shape, 1) % HEADS
            live["mp"].append(_bf(_softmax_rows(jnp.where(row_head == col_head, s, -jnp.inf))))

    def mem_value_piece(g, j):
        def run():
            rows = slice(j * MEM_PIECE * HEADS, (j + 1) * MEM_PIECE * HEADS)
            values = mv_ref[g, j * MEM_PIECE:(j + 1) * MEM_PIECE].reshape(MEM_PIECE * HEADS, HEAD_DIM)
            part = _dot(live["mp"][g][:, rows], _bf(values))
            acc = live.setdefault("om", {})
            acc[g] = part if j == 0 else acc[g] + part
        return run

    def mem_outputs():
        for g in range(group):
            for hd in range(HEADS):
                lo = hd * HEAD_DIM
                om = live["om"][g][hd * seq:(hd + 1) * seq, :]
                amem_ref[g, :, lo:lo + HEAD_DIM] = _bf(om * _silu(col(g, OFF_MG + lo)))

    assert group == 2
    pieces = range(MEM_LEN // MEM_PIECE)
    def run_all(stages):
        def run():
            for stage in stages:
                stage()
        return run

    ret_products = run_all([ret_piece(p) for p in pairs])
    mem_values = run_all([mem_value_piece(1, j) for j in pieces] + [mem_outputs])
    return [(ret_inputs, ()),
            (ret_products, ()),
            (None, [mem_score_piece(0, j) for j in pieces]),
            (ret_outputs, [mem_score_piece(1, j) for j in pieces]),
            (mem_probs, [mem_value_piece(0, j) for j in pieces]),
            (conv, ()),
            (None, ()),
            (mem_values, ())]


def _tail_body(x_ref, aret_ref, aconv_ref, amem_ref, h_ref, wg0_ref, wg1_ref, wg2_ref, bg_ref,
               wro_ref, wco_ref, wmo_ref, wout_ref, fg_ref, y_ref):
    for stage in _tail_stages(x_ref, aret_ref, aconv_ref, amem_ref, h_ref, (wg0_ref, wg1_ref, wg2_ref), bg_ref,
                              wro_ref, wco_ref, wmo_ref, wout_ref, fg_ref, y_ref):
        stage()


def _tail_with_sample_body(x_ref, aret_ref, aconv_ref, amem_ref, h_ref, wg0_ref, wg1_ref, wg2_ref, bg_ref,
                           wro_ref, wco_ref, wmo_ref, wout_ref, fg_ref,
                           dec_ref, z_ref, cos_ref, sin_ref, state_ref, cstate_ref, mk_ref, mv_ref, rng_ref,
                           cw_ref, cbias_ref,
                           y_ref, saret_ref, saconv_ref, samem_ref, nstate_ref, nconv_ref,
                           cbuf, inner_s, qdec_s, kdec_s, carry_s):
    i = pl.program_id(0)
    last = pl.num_programs(0) - 1

    def run(finish_previous, main):
        tail = _tail_stages(x_ref, aret_ref, aconv_ref, amem_ref, h_ref, (wg0_ref, wg1_ref, wg2_ref), bg_ref,
                            wro_ref, wco_ref, wmo_ref, wout_ref, fg_ref, y_ref, carry_s)
        *main_stages, finish = tail
        if not main:
            finish()
            return
        sample = _sample_stages(z_ref, cos_ref, sin_ref, state_ref, cstate_ref, mk_ref, mv_ref, rng_ref, cw_ref,
                                cbias_ref, saret_ref, saconv_ref, samem_ref, nstate_ref, nconv_ref,
                                cbuf, inner_s, qdec_s, kdec_s, dec_ref)
        stages = [finish if finish_previous else (lambda: None)] + main_stages
        assert len(sample) == len(stages)
        for (before, between), stage in zip(sample, stages):
            if before is not None:
                before()
            if between:
                stage(between=between)
            else:
                stage()

    @pl.when(i == 0)
    def _():
        _fill_decay(dec_ref, inner_s, qdec_s, kdec_s, z_ref.shape[1])
        run(finish_previous=False, main=True)

    @pl.when(jnp.logical_and(i > 0, i < last))
    def _():
        run(finish_previous=True, main=True)

    @pl.when(i == last)
    def _():
        run(finish_previous=True, main=False)


def _tail_specs(tile, bg, wro, wco, wmo, wout, fg, tile_index=lambda i: i):
    tok = pl.BlockSpec((tile, D_MODEL), lambda i: (tile_index(i), 0))

    def gate_cols(branch):
        blk = OFF_GATE // D_MODEL + branch
        return pl.BlockSpec((D_MODEL // 2, D_MODEL), lambda i: (0, blk), pipeline_mode=pl.Buffered(1))

    in_specs = [tok, tok, tok, tok,
                tok, gate_cols(0), gate_cols(1), gate_cols(2), _resident(bg.shape),
                _resident(wro.shape), _resident(wco.shape), _resident(wmo.shape), _resident(wout.shape),
                _resident(fg.shape)]
    return tok, in_specs


def _tail(x2d, aret, aconv, amem, h2d, win_pk, bg, wro, wco, wmo, wout, fg):
    tok, in_specs = _tail_specs(SAMPLE_TAIL_TILE, bg, wro, wco, wmo, wout, fg)
    return pl.pallas_call(
        _tail_body,
        grid=(x2d.shape[0] // SAMPLE_TAIL_TILE,),
        in_specs=in_specs,
        out_specs=tok,
        out_shape=jax.ShapeDtypeStruct(x2d.shape, F32),
        compiler_params=pltpu.CompilerParams(dimension_semantics=("arbitrary",),
                                             vmem_limit_bytes=VMEM_LIMIT_BYTES),
        name="tail",
    )(x2d, aret, aconv, amem, h2d, win_pk, win_pk, win_pk, bg, wro, wco, wmo, wout, fg)


def _tail_with_sample(x2d, aret, aconv, amem, h2d, win_pk, bg, wro, wco, wmo, wout, fg,
                      dec, z, cos, sin, state, cstate, mk, mv, rng, cw, cbias):
    steps = x2d.shape[0] // TAIL_TILE
    nb, seq, _ = z.shape
    grp = nb // steps
    assert grp * steps == nb

    def current(i):
        return jnp.minimum(i, steps - 1)

    _, in_specs = _tail_specs(TAIL_TILE, bg, wro, wco, wmo, wout, fg, tile_index=current)
    y_spec = pl.BlockSpec((TAIL_TILE, D_MODEL), lambda i: (jnp.maximum(i - 1, 0), 0))

    def per_req(*tail):
        zeros = (0,) * len(tail)
        return pl.BlockSpec((grp,) + tail, lambda i: (current(i),) + zeros)

    def const(shape):
        zeros = (0,) * len(shape)
        return pl.BlockSpec(shape, lambda i: zeros)

    act = jax.ShapeDtypeStruct((nb, seq, D_MODEL), BF16)
    return pl.pallas_call(
        _tail_with_sample_body,
        grid=(steps + 1,),
        in_specs=in_specs + [pl.BlockSpec(memory_space=pltpu.SMEM),
                             per_req(seq, OFF_GATE), const(cos.shape), const(sin.shape),
                             per_req(HEADS, HEAD_DIM, HEAD_DIM), per_req(CONV_WIDTH - 1, D_MODEL),
                             per_req(MEM_LEN, HEADS, HEAD_DIM), per_req(MEM_LEN, HEADS, HEAD_DIM),
                             const(rng.shape), const(cw.shape), const(cbias.shape)],
        out_specs=[y_spec, per_req(seq, D_MODEL), per_req(seq, D_MODEL), per_req(seq, D_MODEL),
                   per_req(HEADS, HEAD_DIM, HEAD_DIM), per_req(CONV_WIDTH - 1, D_MODEL)],
        out_shape=[jax.ShapeDtypeStruct(x2d.shape, F32), act, act, act,
                   jax.ShapeDtypeStruct(state.shape, F32),
                   jax.ShapeDtypeStruct(cstate.shape, F32)],
        scratch_shapes=[pltpu.VMEM((grp * (SUBLANES + seq), D_MODEL), F32),
                        pltpu.VMEM((HEADS, seq, seq), F32),
                        pltpu.VMEM((HEADS, seq, ROT_HALF), F32),
                        pltpu.VMEM((HEADS, seq, ROT_HALF), F32),
                        pltpu.VMEM((TAIL_TILE, D_MODEL), F32)],
        compiler_params=pltpu.CompilerParams(dimension_semantics=("arbitrary",),
                                             vmem_limit_bytes=VMEM_LIMIT_BYTES),
        name="tail_with_sample",
    )(x2d, aret, aconv, amem, h2d, win_pk, win_pk, win_pk, bg, wro, wco, wmo, wout, fg,
      dec, z, cos, sin, state, cstate, mk, mv, rng, cw, cbias)


def _rope_tables(pos):
    inv = ROPE_BASE ** (-jnp.arange(ROT_HALF, dtype=F32) / ROT_HALF)
    ang = pos.astype(F32)[:, None] * inv[None, :]
    return jnp.cos(ang), jnp.sin(ang)


def _decay_scalars(chunk):
    lg = jnp.log1p(-jnp.exp2(-5.0 - jnp.arange(HEADS, dtype=F32)))
    return jnp.concatenate([lg, jnp.exp(lg * chunk)])


def kernel(x_prompt, x_sample, state_ret, state_conv, cache_mem_k, cache_mem_v, mem_prompt, norm_g, w_in, b_gate, ret_norm_g, conv_w, conv_b, w_ret_o, w_conv_o, w_mem_o, w_out, mem_norm_g, w_mem_kv, final_norm_g):
    assert norm_g.shape[0] == 1, "single-layer trunk"
    nbp, seq_p, _ = x_prompt.shape
    nbs, seq_s, _ = x_sample.shape
    assert seq_p % PROMPT_TILE == 0 and PROMPT_TILE % RET_CHUNK == 0 and (nbp * seq_p) % TAIL_TILE == 0
    assert seq_s == SUBLANES and (nbs * seq_s) % SAMPLE_TAIL_TILE == 0

    ng = norm_g[0][None, :]
    bg = b_gate[0][None, :]
    rng = ret_norm_g[0].reshape(1, D_MODEL)
    cw = conv_w[0]
    cbias = conv_b[0][None, :]
    wro, wco, wmo, wout = _pack_weights(w_ret_o[0], w_conv_o[0], w_mem_o[0], w_out[0])
    fg = final_norm_g[None, :]

    cos_p, sin_p = _rope_tables(jnp.arange(seq_p, dtype=jnp.int32))
    cos_s, sin_s = _rope_tables(PAST_LEN + jnp.arange(seq_s, dtype=jnp.int32))

    xs2d = x_sample.reshape(nbs * seq_s, D_MODEL)
    z, win_pk, h_s, mk, mv, mk_bf, mv_bf = _prep(xs2d, ng, w_in[0], mem_prompt, mem_norm_g[0][None, :],
                                                 w_mem_kv[0])
    z = z.reshape(nbs, seq_s, OFF_GATE)
    tail_weights = (win_pk, bg, wro, wco, wmo, wout, fg)

    h_p, aret_p, aconv_p, amem_p, ret_p, conv_p = _prompt_branches(
        x_prompt, _decay_scalars(RET_CHUNK), cos_p, sin_p, mk_bf, mv_bf, ng, win_pk, rng, cw, cbias)

    flat_p = lambda a: a.reshape(nbp * seq_p, D_MODEL)
    y_prompt, aret_s, aconv_s, amem_s, ret_s, conv_s = _tail_with_sample(
        flat_p(x_prompt), flat_p(aret_p), flat_p(aconv_p), flat_p(amem_p), flat_p(h_p), *tail_weights,
        _decay_scalars(seq_s), z, cos_s, sin_s, state_ret[0], state_conv[0], cache_mem_k[0], cache_mem_v[0],
        rng, cw, cbias)

    flat_s = lambda a: a.reshape(nbs * seq_s, D_MODEL)
    y_sample = _tail(xs2d, flat_s(aret_s), flat_s(aconv_s), flat_s(amem_s), h_s, *tail_weights)

    return (y_prompt.reshape(x_prompt.shape), y_sample.reshape(x_sample.shape),
            ret_p[None], ret_s[None], conv_p[None], conv_s[None], mk[None], mv[None])
```

```python
import functools

import jax
import jax.numpy as jnp
from jax import lax
from jax.experimental import pallas as pl
from jax.experimental.pallas import tpu as pltpu

F32 = jnp.float32
BF16 = jnp.bfloat16

D_MODEL = 1024
HEADS = 4
HEAD_DIM = D_MODEL // HEADS
ROT_HALF = HEAD_DIM // 2
ROPE_BASE = 10000.0
CONV_WIDTH = 3
MEM_LEN = 256
MEM_PIECE = 64
N_BRANCH = 3
EPS = 1e-6
PAST_LEN = 16384
QK_SCALE = HEAD_DIM ** -0.5

OFF_RQ, OFF_RK, OFF_RV, OFF_RG = 0, 1024, 2048, 3072
OFF_CU, OFF_CB, OFF_CC, OFF_CG = 4096, 5120, 6144, 7168
OFF_MQ, OFF_MG = 8192, 9216
OFF_GATE = 10240
IN_TOTAL = OFF_GATE + N_BRANCH * D_MODEL

SUBLANES = 8
VMEM_LIMIT_BYTES = 60 * 1024 * 1024

PROMPT_TILE = 512
RET_CHUNK = 256
TAIL_TILE = 256
SAMPLE_TAIL_TILE = 512


def _bf(x):
    return x.astype(BF16)


def _dot(a, b):
    return jnp.dot(a, b, preferred_element_type=F32)


def _dot_nt(a, b):
    return lax.dot_general(a, b, (((1,), (1,)), ((), ())), preferred_element_type=F32)


def _dot_tn(a, b):
    return lax.dot_general(a, b, (((0,), (0,)), ((), ())), preferred_element_type=F32)


def _rms(x, g):
    return x * lax.rsqrt(jnp.mean(x * x, axis=-1, keepdims=True) + EPS) * g


def _silu(x):
    return x * jax.nn.sigmoid(x)


def _rope(x, cos, sin):
    x1 = x[:, :ROT_HALF]
    x2 = x[:, ROT_HALF:]
    return jnp.concatenate([x1 * cos - x2 * sin, x1 * sin + x2 * cos], axis=-1)


def _lanes_twice(t):
    return jnp.concatenate([t, t], axis=-1)


def _pack_rows(w_bf):
    return pltpu.bitcast(w_bf, jnp.int32)


def _unpack_rows(w_packed):
    return pltpu.bitcast(w_packed, BF16)


def _fill_decay(dec_ref, inner_s, qdec_s, kdec_s, chunk):
    i = lax.broadcasted_iota(jnp.int32, (chunk, chunk), 0)
    j = lax.broadcasted_iota(jnp.int32, (chunk, chunk), 1)
    diff = (i - j).astype(F32)
    r = lax.broadcasted_iota(jnp.int32, (chunk, ROT_HALF), 0).astype(F32)
    for hd in range(HEADS):
        lg = dec_ref[hd]
        inner_s[hd] = jnp.where(diff >= 0, jnp.exp(lg * jnp.maximum(diff, 0.0)), 0.0)
        qdec_s[hd] = jnp.exp(lg * (r + 1.0))
        kdec_s[hd] = jnp.exp(lg * (chunk - 1.0 - r))


def _conv_branch(cu, cb, cc, cg, cbuf, cw_ref, cbias_ref, rows, cols):
    pre = cc * cu
    cbuf[SUBLANES:SUBLANES + rows, cols] = pre
    conv = (cbias_ref[:, cols]
            + cw_ref[0:1, cols] * cbuf[SUBLANES - 2:SUBLANES - 2 + rows, cols]
            + cw_ref[1:2, cols] * cbuf[SUBLANES - 1:SUBLANES - 1 + rows, cols]
            + cw_ref[2:3, cols] * pre)
    return pre, cb * conv * _silu(cg)


def _softmax_rows(s):
    e = jnp.exp(s - jnp.max(s, axis=-1, keepdims=True))
    return e * (1.0 / jnp.sum(e, axis=-1, keepdims=True))


def _resident(shape, block=None):
    zeros = (0,) * len(shape)
    return pl.BlockSpec(block or shape, lambda *_: zeros, pipeline_mode=pl.Buffered(1))


def _pack_weights_body(*refs):
    n = len(refs) // 2
    for w_ref, out_ref in zip(refs[:n], refs[n:]):
        out_ref[...] = _pack_rows(_bf(w_ref[...]))


def _pack_weights(*ws):
    return pl.pallas_call(
        _pack_weights_body,
        out_shape=[jax.ShapeDtypeStruct((w.shape[0] // 2, w.shape[1]), jnp.int32) for w in ws],
        compiler_params=pltpu.CompilerParams(vmem_limit_bytes=VMEM_LIMIT_BYTES),
        name="pack_weights",
    )(*ws)


def _prompt_branch_body(dec_ref, x_ref, cos_ref, sin_ref, mk_ref, mv_ref, ng_ref, win_ref, rng_ref,
                        cw_ref, cbias_ref,
                        aret_ref, aconv_ref, amem_ref, state_ref, nconv_ref,
                        cbuf, inner_s, qdec_s, kdec_s):
    b = pl.program_id(0)
    c = pl.program_id(1)
    tile = x_ref.shape[0]

    @pl.when(jnp.logical_and(b == 0, c == 0))
    def _():
        _fill_decay(dec_ref, inner_s, qdec_s, kdec_s, RET_CHUNK)

    @pl.when(c == 0)
    def _():
        state_ref[...] = jnp.zeros(state_ref.shape, F32)
        cbuf[0:SUBLANES, :] = jnp.zeros((SUBLANES, D_MODEL), F32)

    h = _bf(_rms(x_ref[...], ng_ref[...]))
    cos = cos_ref[...]
    sin = sin_ref[...]

    def proj(off):
        return _dot(h, _unpack_rows(win_ref[:, off:off + HEAD_DIM]))

    heads = range(HEADS)
    head_cols = [slice(hd * HEAD_DIM, (hd + 1) * HEAD_DIM) for hd in heads]

    def conv_block(hd):
        lo = hd * HEAD_DIM
        cols = head_cols[hd]
        pre, a_conv = _conv_branch(proj(OFF_CU + lo), proj(OFF_CB + lo), proj(OFF_CC + lo), proj(OFF_CG + lo),
                                   cbuf, cw_ref, cbias_ref, tile, cols)
        cbuf[0:SUBLANES, cols] = pre[tile - SUBLANES:, :]
        nconv_ref[:, cols] = pre[tile - (CONV_WIDTH - 1):, :]
        aconv_ref[:, cols] = _bf(a_conv)

    q = [_bf(_rope(proj(OFF_RQ + hd * HEAD_DIM), cos, sin)) for hd in heads]
    k = [_rope(proj(OFF_RK + hd * HEAD_DIM), cos, sin) * QK_SCALE for hd in heads]
    v = [_bf(proj(OFF_RV + hd * HEAD_DIM)) for hd in heads]
    chunks = [slice(j * RET_CHUNK, (j + 1) * RET_CHUNK) for j in range(tile // RET_CHUNK)]
    scores, cross = [], []
    for rows in chunks:
        for hd in heads:
            state = state_ref[hd]
            kc = k[hd][rows]
            scores.append(_bf(_dot_nt(q[hd][rows], _bf(kc)) * inner_s[hd]))
            cross.append(_dot(q[hd][rows], _bf(state)))
            state_ref[hd] = (state * dec_ref[HEADS + hd]
                             + _dot_tn(_bf(kc * _lanes_twice(kdec_s[hd])), v[hd][rows]))
    conv_block(0)
    ret_o = []
    for hd in heads:
        qdec = _lanes_twice(qdec_s[hd])
        ret_o.append(jnp.concatenate(
            [_dot(scores[j * HEADS + hd], v[hd][rows]) + cross[j * HEADS + hd] * qdec
             for j, rows in enumerate(chunks)], axis=0))
    mq = [_bf(proj(OFF_MQ + hd * HEAD_DIM)) for hd in heads]
    conv_block(1)
    mscores = [_dot_nt(mq[hd], mk_ref[:, head_cols[hd]]) * QK_SCALE for hd in heads]
    for hd in heads:
        cols = head_cols[hd]
        aret_ref[:, cols] = _bf(_rms(ret_o[hd], rng_ref[:, cols]) * _silu(proj(OFF_RG + hd * HEAD_DIM)))
    conv_block(2)
    probs = [_bf(_softmax_rows(mscores[hd])) for hd in heads]
    conv_block(3)
    for hd in heads:
        cols = head_cols[hd]
        om = _dot(probs[hd], mv_ref[:, cols])
        amem_ref[:, cols] = _bf(om * _silu(proj(OFF_MG + hd * HEAD_DIM)))


def _prompt_branches(x, dec, cos, sin, mk_bf, mv_bf, ng, win_pk, rng, cw, cbias):
    nb, seq, _ = x.shape
    tile = PROMPT_TILE
    tok = pl.BlockSpec((None, tile, D_MODEL), lambda b, c: (b, c, 0))
    rot = pl.BlockSpec((tile, ROT_HALF), lambda b, c: (c, 0))
    mem = pl.BlockSpec((None, MEM_LEN, D_MODEL), lambda b, c: (b, 0, 0))
    act = jax.ShapeDtypeStruct(x.shape, BF16)
    return pl.pallas_call(
        _prompt_branch_body,
        grid=(nb, seq // tile),
        in_specs=[pl.BlockSpec(memory_space=pltpu.SMEM), tok, rot, rot, mem, mem,
                  _resident(ng.shape), _resident(win_pk.shape, (D_MODEL // 2, OFF_GATE)), _resident(rng.shape),
                  _resident(cw.shape), _resident(cbias.shape)],
        out_specs=[tok, tok, tok,
                   pl.BlockSpec((None, HEADS, HEAD_DIM, HEAD_DIM), lambda b, c: (b, 0, 0, 0)),
                   pl.BlockSpec((None, CONV_WIDTH - 1, D_MODEL), lambda b, c: (b, 0, 0))],
        out_shape=[act, act, act,
                   jax.ShapeDtypeStruct((nb, HEADS, HEAD_DIM, HEAD_DIM), F32),
                   jax.ShapeDtypeStruct((nb, CONV_WIDTH - 1, D_MODEL), F32)],
        scratch_shapes=[pltpu.VMEM((SUBLANES + tile, D_MODEL), F32),
                        pltpu.VMEM((HEADS, RET_CHUNK, RET_CHUNK), F32),
                        pltpu.VMEM((HEADS, RET_CHUNK, ROT_HALF), F32),
                        pltpu.VMEM((HEADS, RET_CHUNK, ROT_HALF), F32)],
        compiler_params=pltpu.CompilerParams(dimension_semantics=("arbitrary", "arbitrary"),
                                             vmem_limit_bytes=VMEM_LIMIT_BYTES),
        name="prompt_branches",
    )(dec, x, cos, sin, mk_bf, mv_bf, ng, win_pk, rng, cw, cbias)


def _prep_body(x_ref, ng_ref, w_ref, mem_ref, mg_ref, wkv_ref,
               z_ref, wpk_ref, k_ref, v_ref, kb_ref, vb_ref, h_s, wkv_s, *, n_mem):
    n = pl.program_id(0)

    @pl.when(n == 0)
    def _():
        h_s[...] = _bf(_rms(x_ref[...], ng_ref[...]))
        wkv_s[...] = _bf(wkv_ref[...])

    w = _bf(w_ref[...])
    wpk_ref[...] = _pack_rows(w)

    @pl.when(n < OFF_GATE // w_ref.shape[1])
    def _():
        z_ref[...] = _dot(h_s[...], w)

    @pl.when(n < n_mem)
    def _():
        kv = _dot(_bf(_rms(mem_ref[...], mg_ref[...])), wkv_s[...])
        k = kv[:, :D_MODEL]
        v = kv[:, D_MODEL:]
        for hd in range(HEADS):
            lo = hd * HEAD_DIM
            k_ref[:, hd, :] = k[:, lo:lo + HEAD_DIM]
            v_ref[:, hd, :] = v[:, lo:lo + HEAD_DIM]
        kb_ref[...] = _bf(k)
        vb_ref[...] = _bf(v)


def _prep(x2d, ng, w_in, mem, mem_g, w_mem_kv):
    ntok = x2d.shape[0]
    n_mem = mem.shape[0]
    blk = D_MODEL
    steps = IN_TOTAL // blk
    z_blocks = OFF_GATE // blk
    assert n_mem <= steps

    def mem_blk(*tail):
        zeros = (0,) * len(tail)
        return pl.BlockSpec((None,) + tail, lambda n: (jnp.minimum(n, n_mem - 1),) + zeros)

    heads_shape = jax.ShapeDtypeStruct((n_mem, MEM_LEN, HEADS, HEAD_DIM), F32)
    return pl.pallas_call(
        functools.partial(_prep_body, n_mem=n_mem),
        grid=(steps,),
        in_specs=[_resident(x2d.shape), _resident(ng.shape),
                  pl.BlockSpec((D_MODEL, blk), lambda n: (0, n)),
                  mem_blk(MEM_LEN, D_MODEL), _resident(mem_g.shape), _resident(w_mem_kv.shape)],
        out_specs=[pl.BlockSpec((ntok, blk), lambda n: (0, jnp.minimum(n, z_blocks - 1))),
                   pl.BlockSpec((D_MODEL // 2, blk), lambda n: (0, n)),
                   mem_blk(MEM_LEN, HEADS, HEAD_DIM), mem_blk(MEM_LEN, HEADS, HEAD_DIM),
                   mem_blk(MEM_LEN, D_MODEL), mem_blk(MEM_LEN, D_MODEL)],
        out_shape=[jax.ShapeDtypeStruct((ntok, OFF_GATE), F32),
                   jax.ShapeDtypeStruct((D_MODEL // 2, IN_TOTAL), jnp.int32),
                   heads_shape, heads_shape,
                   jax.ShapeDtypeStruct(mem.shape, BF16), jax.ShapeDtypeStruct(mem.shape, BF16)],
        scratch_shapes=[pltpu.VMEM(x2d.shape, BF16), pltpu.VMEM(w_mem_kv.shape, BF16)],
        compiler_params=pltpu.CompilerParams(dimension_semantics=("arbitrary",),
                                             vmem_limit_bytes=VMEM_LIMIT_BYTES),
        name="prep",
    )(x2d, ng, w_in, mem, mem_g, w_mem_kv)


def _tail_stages(x_ref, aret_ref, aconv_ref, amem_ref, ng_ref, wg_refs, bg_ref, wro_ref, wco_ref, wmo_ref,
                 wout_ref, fg_ref, y_ref):
    live = {}
    act_refs = (aret_ref, aconv_ref, amem_ref)
    w_refs = (wro_ref, wco_ref, wmo_ref)

    def gate(branch):
        def run(between=()):
            if branch == 0:
                live["h"] = _bf(_rms(x_ref[...], ng_ref[...]))
            nblk = max(len(between), 1)
            width = D_MODEL // nblk
            parts = []
            for i in range(nblk):
                cols = slice(i * width, (i + 1) * width)
                logits = _dot(live["h"], _unpack_rows(wg_refs[branch][:, cols]))
                parts.append(jax.nn.sigmoid(logits + bg_ref[:, branch * D_MODEL + i * width:
                                                            branch * D_MODEL + (i + 1) * width]))
                if between:
                    between[i]()
            live["gate"] = jnp.concatenate(parts, axis=-1)
        return run

    def project(branch):
        def run(between=()):
            nblk = max(len(between), 1)
            width = D_MODEL // nblk
            parts = []
            for i in range(nblk):
                cols = slice(i * width, (i + 1) * width)
                parts.append(live["gate"][:, cols] * _dot(act_refs[branch][...], _unpack_rows(w_refs[branch][:, cols])))
                if between:
                    between[i]()
            y_branch = jnp.concatenate(parts, axis=-1)
            live["merged"] = y_branch if branch == 0 else live["merged"] + y_branch
        return run

    def out_project():
        live["y"] = x_ref[...] + _dot(_bf(live["merged"]), _unpack_rows(wout_ref[...]))

    def finish():
        y_ref[...] = _rms(live["y"], fg_ref[...])

    stages = []
    for branch in range(N_BRANCH):
        stages += [gate(branch), project(branch)]
    return stages + [out_project, finish]


def _sample_stages(z_ref, cos_ref, sin_ref, state_ref, cstate_ref, mk_ref, mv_ref, rng_ref, cw_ref, cbias_ref,
                   aret_ref, aconv_ref, amem_ref, nstate_ref, nconv_ref, cbuf, inner_s, qdec_s, kdec_s, dec_ref):
    group, seq, _ = z_ref.shape
    pairs = [(g, hd) for g in range(group) for hd in range(HEADS)]
    live = {}

    def col(g, off, width=HEAD_DIM):
        return z_ref[g, :, off:off + width]

    def ret_inputs():
        cos = cos_ref[...]
        sin = sin_ref[...]
        live["q"] = {p: _bf(_rope(col(p[0], OFF_RQ + p[1] * HEAD_DIM), cos, sin)) for p in pairs}
        live["k"] = {p: _rope(col(p[0], OFF_RK + p[1] * HEAD_DIM), cos, sin) * QK_SCALE for p in pairs}
        live["v"] = {p: _bf(col(p[0], OFF_RV + p[1] * HEAD_DIM)) for p in pairs}

    def ret_products():
        live["scores"], live["cross"] = {}, {}
        for p in pairs:
            g, hd = p
            state = state_ref[g, hd]
            k = live["k"][p]
            live["scores"][p] = _bf(_dot_nt(live["q"][p], _bf(k)) * inner_s[hd])
            live["cross"][p] = _dot(live["q"][p], _bf(state))
            nstate_ref[g, hd] = (state * dec_ref[HEADS + hd]
                                 + _dot_tn(_bf(k * _lanes_twice(kdec_s[hd])), live["v"][p]))

    def ret_outputs():
        for p in pairs:
            g, hd = p
            lo = hd * HEAD_DIM
            o = _dot(live["scores"][p], live["v"][p]) + live["cross"][p] * _lanes_twice(qdec_s[hd])
            o = _rms(o, rng_ref[:, lo:lo + HEAD_DIM])
            aret_ref[g, :, lo:lo + HEAD_DIM] = _bf(o * _silu(col(g, OFF_RG + lo)))

    def conv():
        for g in range(group):
            base = g * (SUBLANES + seq)
            cbuf[base:base + SUBLANES, :] = jnp.zeros((SUBLANES, D_MODEL), F32)
            cbuf[base + SUBLANES - (CONV_WIDTH - 1):base + SUBLANES, :] = cstate_ref[g]
            pre, a_conv = _conv_branch(col(g, OFF_CU, D_MODEL), col(g, OFF_CB, D_MODEL), col(g, OFF_CC, D_MODEL),
                                       col(g, OFF_CG, D_MODEL), cbuf.at[base:base + SUBLANES + seq], cw_ref,
                                       cbias_ref, seq, slice(0, D_MODEL))
            nconv_ref[g] = pre[seq - (CONV_WIDTH - 1):, :]
            aconv_ref[g] = _bf(a_conv)

    def mem_score_piece(g, j):
        def run():
            if j == 0:
                live.setdefault("ms", {})[g] = []
                live.setdefault("mq", {})[g] = _bf(
                    jnp.concatenate([col(g, OFF_MQ + hd * HEAD_DIM) for hd in range(HEADS)], axis=0))
            keys = mk_ref[g, j * MEM_PIECE:(j + 1) * MEM_PIECE].reshape(MEM_PIECE * HEADS, HEAD_DIM)
            live["ms"][g].append(_dot_nt(live["mq"][g], _bf(keys)) * QK_SCALE)
        return run

    def mem_probs():
        live["mp"] = []
        for g in range(group):
            s = jnp.concatenate(live["ms"][g], axis=-1)
            row_head = lax.broadcasted_iota(jnp.int32, s.shape, 0) // seq
            col_head = lax.broadcasted_iota(jnp.int32, s.shape, 1) % HEADS
            live["mp"].append(_bf(_softmax_rows(jnp.where(row_head == col_head, s, -jnp.inf))))

    def mem_outputs():
        for g in range(group):
            mv2d = _bf(mv_ref[g].reshape(MEM_LEN * HEADS, HEAD_DIM))
            om_all = _dot(live["mp"][g], mv2d)
            for hd in range(HEADS):
                lo = hd * HEAD_DIM
                om = om_all[hd * seq:(hd + 1) * seq, :]
                amem_ref[g, :, lo:lo + HEAD_DIM] = _bf(om * _silu(col(g, OFF_MG + lo)))

    pieces = [[mem_score_piece(g, j) for j in range(MEM_LEN // MEM_PIECE)] for g in range(group)]
    assert group == 2
    return [(ret_inputs, ()), (ret_products, ()), (None, pieces[0]), (ret_outputs, pieces[1]),
            (mem_probs, ()), (conv, ()), (mem_outputs, ())]


def _tail_body(x_ref, aret_ref, aconv_ref, amem_ref, ng_ref, wg0_ref, wg1_ref, wg2_ref, bg_ref,
               wro_ref, wco_ref, wmo_ref, wout_ref, fg_ref, y_ref):
    for stage in _tail_stages(x_ref, aret_ref, aconv_ref, amem_ref, ng_ref, (wg0_ref, wg1_ref, wg2_ref), bg_ref,
                              wro_ref, wco_ref, wmo_ref, wout_ref, fg_ref, y_ref):
        stage()


def _tail_with_sample_body(x_ref, aret_ref, aconv_ref, amem_ref, ng_ref, wg0_ref, wg1_ref, wg2_ref, bg_ref,
                           wro_ref, wco_ref, wmo_ref, wout_ref, fg_ref,
                           dec_ref, z_ref, cos_ref, sin_ref, state_ref, cstate_ref, mk_ref, mv_ref, rng_ref,
                           cw_ref, cbias_ref,
                           y_ref, saret_ref, saconv_ref, samem_ref, nstate_ref, nconv_ref,
                           cbuf, inner_s, qdec_s, kdec_s):
    @pl.when(pl.program_id(0) == 0)
    def _():
        _fill_decay(dec_ref, inner_s, qdec_s, kdec_s, z_ref.shape[1])

    tail = _tail_stages(x_ref, aret_ref, aconv_ref, amem_ref, ng_ref, (wg0_ref, wg1_ref, wg2_ref), bg_ref,
                        wro_ref, wco_ref, wmo_ref, wout_ref, fg_ref, y_ref)
    sample = _sample_stages(z_ref, cos_ref, sin_ref, state_ref, cstate_ref, mk_ref, mv_ref, rng_ref, cw_ref,
                            cbias_ref, saret_ref, saconv_ref, samem_ref, nstate_ref, nconv_ref,
                            cbuf, inner_s, qdec_s, kdec_s, dec_ref)
    assert len(sample) == len(tail) - 1
    for i, tail_stage in enumerate(tail):
        before, between = sample[i] if i < len(sample) - 1 else sample[-1] if i == len(tail) - 1 else (None, ())
        if before is not None:
            before()
        if between:
            tail_stage(between=between)
        else:
            tail_stage()


def _tail_specs(tile, ng, bg, wro, wco, wmo, wout, fg):
    tok = pl.BlockSpec((tile, D_MODEL), lambda i: (i, 0))

    def gate_cols(branch):
        blk = OFF_GATE // D_MODEL + branch
        return pl.BlockSpec((D_MODEL // 2, D_MODEL), lambda i: (0, blk), pipeline_mode=pl.Buffered(1))

    in_specs = [tok, tok, tok, tok,
                _resident(ng.shape), gate_cols(0), gate_cols(1), gate_cols(2), _resident(bg.shape),
                _resident(wro.shape), _resident(wco.shape), _resident(wmo.shape), _resident(wout.shape),
                _resident(fg.shape)]
    return tok, in_specs


def _tail(x2d, aret, aconv, amem, ng, win_pk, bg, wro, wco, wmo, wout, fg):
    tok, in_specs = _tail_specs(SAMPLE_TAIL_TILE, ng, bg, wro, wco, wmo, wout, fg)
    return pl.pallas_call(
        _tail_body,
        grid=(x2d.shape[0] // SAMPLE_TAIL_TILE,),
        in_specs=in_specs,
        out_specs=tok,
        out_shape=jax.ShapeDtypeStruct(x2d.shape, F32),
        compiler_params=pltpu.CompilerParams(dimension_semantics=("arbitrary",),
                                             vmem_limit_bytes=VMEM_LIMIT_BYTES),
        name="tail",
    )(x2d, aret, aconv, amem, ng, win_pk, win_pk, win_pk, bg, wro, wco, wmo, wout, fg)


def _tail_with_sample(x2d, aret, aconv, amem, ng, win_pk, bg, wro, wco, wmo, wout, fg,
                      dec, z, cos, sin, state, cstate, mk, mv, rng, cw, cbias):
    steps = x2d.shape[0] // TAIL_TILE
    nb, seq, _ = z.shape
    grp = nb // steps
    assert grp * steps == nb
    tok, in_specs = _tail_specs(TAIL_TILE, ng, bg, wro, wco, wmo, wout, fg)

    def per_req(*tail):
        zeros = (0,) * len(tail)
        return pl.BlockSpec((grp,) + tail, lambda i: (i,) + zeros)

    def const(shape):
        zeros = (0,) * len(shape)
        return pl.BlockSpec(shape, lambda i: zeros)

    act = jax.ShapeDtypeStruct((nb, seq, D_MODEL), BF16)
    return pl.pallas_call(
        _tail_with_sample_body,
        grid=(steps,),
        in_specs=in_specs + [pl.BlockSpec(memory_space=pltpu.SMEM),
                             per_req(seq, OFF_GATE), const(cos.shape), const(sin.shape),
                             per_req(HEADS, HEAD_DIM, HEAD_DIM), per_req(CONV_WIDTH - 1, D_MODEL),
                             per_req(MEM_LEN, HEADS, HEAD_DIM), per_req(MEM_LEN, HEADS, HEAD_DIM),
                             const(rng.shape), const(cw.shape), const(cbias.shape)],
        out_specs=[tok, per_req(seq, D_MODEL), per_req(seq, D_MODEL), per_req(seq, D_MODEL),
                   per_req(HEADS, HEAD_DIM, HEAD_DIM), per_req(CONV_WIDTH - 1, D_MODEL)],
        out_shape=[jax.ShapeDtypeStruct(x2d.shape, F32), act, act, act,
                   jax.ShapeDtypeStruct(state.shape, F32),
                   jax.ShapeDtypeStruct(cstate.shape, F32)],
        scratch_shapes=[pltpu.VMEM((grp * (SUBLANES + seq), D_MODEL), F32),
                        pltpu.VMEM((HEADS, seq, seq), F32),
                        pltpu.VMEM((HEADS, seq, ROT_HALF), F32),
                        pltpu.VMEM((HEADS, seq, ROT_HALF), F32)],
        compiler_params=pltpu.CompilerParams(dimension_semantics=("arbitrary",),
                                             vmem_limit_bytes=VMEM_LIMIT_BYTES),
        name="tail_with_sample",
    )(x2d, aret, aconv, amem, ng, win_pk, win_pk, win_pk, bg, wro, wco, wmo, wout, fg,
      dec, z, cos, sin, state, cstate, mk, mv, rng, cw, cbias)


def _rope_tables(pos):
    inv = ROPE_BASE ** (-jnp.arange(ROT_HALF, dtype=F32) / ROT_HALF)
    ang = pos.astype(F32)[:, None] * inv[None, :]
    return jnp.cos(ang), jnp.sin(ang)


def _decay_scalars(chunk):
    lg = jnp.log1p(-jnp.exp2(-5.0 - jnp.arange(HEADS, dtype=F32)))
    return jnp.concatenate([lg, jnp.exp(lg * chunk)])


def kernel(x_prompt, x_sample, state_ret, state_conv, cache_mem_k, cache_mem_v, mem_prompt, norm_g, w_in, b_gate, ret_norm_g, conv_w, conv_b, w_ret_o, w_conv_o, w_mem_o, w_out, mem_norm_g, w_mem_kv, final_norm_g):
    assert norm_g.shape[0] == 1, "single-layer trunk"
    nbp, seq_p, _ = x_prompt.shape
    nbs, seq_s, _ = x_sample.shape
    assert seq_p % PROMPT_TILE == 0 and PROMPT_TILE % RET_CHUNK == 0 and (nbp * seq_p) % TAIL_TILE == 0
    assert seq_s == SUBLANES and (nbs * seq_s) % SAMPLE_TAIL_TILE == 0

    ng = norm_g[0][None, :]
    bg = b_gate[0][None, :]
    rng = ret_norm_g[0].reshape(1, D_MODEL)
    cw = conv_w[0]
    cbias = conv_b[0][None, :]
    wro, wco, wmo, wout = _pack_weights(w_ret_o[0], w_conv_o[0], w_mem_o[0], w_out[0])
    fg = final_norm_g[None, :]

    cos_p, sin_p = _rope_tables(jnp.arange(seq_p, dtype=jnp.int32))
    cos_s, sin_s = _rope_tables(PAST_LEN + jnp.arange(seq_s, dtype=jnp.int32))

    xs2d = x_sample.reshape(nbs * seq_s, D_MODEL)
    z, win_pk, mk, mv, mk_bf, mv_bf = _prep(xs2d, ng, w_in[0], mem_prompt, mem_norm_g[0][None, :],
                                            w_mem_kv[0])
    z = z.reshape(nbs, seq_s, OFF_GATE)
    tail_weights = (ng, win_pk, bg, wro, wco, wmo, wout, fg)

    aret_p, aconv_p, amem_p, ret_p, conv_p = _prompt_branches(
        x_prompt, _decay_scalars(RET_CHUNK), cos_p, sin_p, mk_bf, mv_bf, ng, win_pk, rng, cw, cbias)

    flat_p = lambda a: a.reshape(nbp * seq_p, D_MODEL)
    y_prompt, aret_s, aconv_s, amem_s, ret_s, conv_s = _tail_with_sample(
        flat_p(x_prompt), flat_p(aret_p), flat_p(aconv_p), flat_p(amem_p), *tail_weights,
        _decay_scalars(seq_s), z, cos_s, sin_s, state_ret[0], state_conv[0], cache_mem_k[0], cache_mem_v[0],
        rng, cw, cbias)

    flat_s = lambda a: a.reshape(nbs * seq_s, D_MODEL)
    y_sample = _tail(xs2d, flat_s(aret_s), flat_s(aconv_s), flat_s(amem_s), *tail_weights)

    return (y_prompt.reshape(x_prompt.shape), y_sample.reshape(x_sample.shape),
            ret_p[None], ret_s[None], conv_p[None], conv_s[None], mk[None], mv[None])
```

```python
import functools

import jax
import jax.numpy as jnp
from jax import lax
from jax.experimental import pallas as pl
from jax.experimental.pallas import tpu as pltpu

F32 = jnp.float32
BF16 = jnp.bfloat16

D_MODEL = 1024
HEADS = 4
HEAD_DIM = D_MODEL // HEADS
ROT_HALF = HEAD_DIM // 2
ROPE_BASE = 10000.0
CONV_WIDTH = 3
MEM_LEN = 256
MEM_PIECE = 64
N_BRANCH = 3
EPS = 1e-6
PAST_LEN = 16384
QK_SCALE = HEAD_DIM ** -0.5

OFF_RQ, OFF_RK, OFF_RV, OFF_RG = 0, 1024, 2048, 3072
OFF_CU, OFF_CB, OFF_CC, OFF_CG = 4096, 5120, 6144, 7168
OFF_MQ, OFF_MG = 8192, 9216
OFF_GATE = 10240
IN_TOTAL = OFF_GATE + N_BRANCH * D_MODEL

SUBLANES = 8
VMEM_LIMIT_BYTES = 60 * 1024 * 1024

PROMPT_TILE = 512
RET_CHUNK = 256
TAIL_TILE = 256


def _bf(x):
    return x.astype(BF16)


def _dot(a, b):
    return jnp.dot(a, b, preferred_element_type=F32)


def _dot_nt(a, b):
    return lax.dot_general(a, b, (((1,), (1,)), ((), ())), preferred_element_type=F32)


def _dot_tn(a, b):
    return lax.dot_general(a, b, (((0,), (0,)), ((), ())), preferred_element_type=F32)


def _rms(x, g):
    return x * lax.rsqrt(jnp.mean(x * x, axis=-1, keepdims=True) + EPS) * g


def _silu(x):
    return x * jax.nn.sigmoid(x)


def _rope(x, cos, sin):
    x1 = x[:, :ROT_HALF]
    x2 = x[:, ROT_HALF:]
    return jnp.concatenate([x1 * cos - x2 * sin, x1 * sin + x2 * cos], axis=-1)


def _lanes_twice(t):
    return jnp.concatenate([t, t], axis=-1)


def _pack_rows(w_bf):
    return pltpu.bitcast(w_bf, jnp.int32)


def _unpack_rows(w_packed):
    return pltpu.bitcast(w_packed, BF16)


def _fill_decay(dec_ref, inner_s, qdec_s, kdec_s, chunk):
    i = lax.broadcasted_iota(jnp.int32, (chunk, chunk), 0)
    j = lax.broadcasted_iota(jnp.int32, (chunk, chunk), 1)
    diff = (i - j).astype(F32)
    r = lax.broadcasted_iota(jnp.int32, (chunk, ROT_HALF), 0).astype(F32)
    for hd in range(HEADS):
        lg = dec_ref[hd]
        inner_s[hd] = jnp.where(diff >= 0, jnp.exp(lg * jnp.maximum(diff, 0.0)), 0.0)
        qdec_s[hd] = jnp.exp(lg * (r + 1.0))
        kdec_s[hd] = jnp.exp(lg * (chunk - 1.0 - r))


def _conv_branch(cu, cb, cc, cg, cbuf, cw_ref, cbias_ref, rows, cols):
    pre = cc * cu
    cbuf[SUBLANES:SUBLANES + rows, cols] = pre
    conv = (cbias_ref[:, cols]
            + cw_ref[0:1, cols] * cbuf[SUBLANES - 2:SUBLANES - 2 + rows, cols]
            + cw_ref[1:2, cols] * cbuf[SUBLANES - 1:SUBLANES - 1 + rows, cols]
            + cw_ref[2:3, cols] * pre)
    return pre, cb * conv * _silu(cg)


def _softmax_rows(s):
    e = jnp.exp(s - jnp.max(s, axis=-1, keepdims=True))
    return e * (1.0 / jnp.sum(e, axis=-1, keepdims=True))


def _resident(shape, block=None):
    zeros = (0,) * len(shape)
    return pl.BlockSpec(block or shape, lambda *_: zeros, pipeline_mode=pl.Buffered(1))


def _pack_weights_body(*refs):
    n = len(refs) // 2
    for w_ref, out_ref in zip(refs[:n], refs[n:]):
        out_ref[...] = _pack_rows(_bf(w_ref[...]))


def _pack_weights(*ws):
    return pl.pallas_call(
        _pack_weights_body,
        out_shape=[jax.ShapeDtypeStruct((w.shape[0] // 2, w.shape[1]), jnp.int32) for w in ws],
        compiler_params=pltpu.CompilerParams(vmem_limit_bytes=VMEM_LIMIT_BYTES),
        name="pack_weights",
    )(*ws)


def _prompt_branch_body(dec_ref, x_ref, cos_ref, sin_ref, mk_ref, mv_ref, ng_ref, win_ref, rng_ref,
                        cw_ref, cbias_ref,
                        aret_ref, aconv_ref, amem_ref, state_ref, nconv_ref,
                        cbuf, inner_s, qdec_s, kdec_s):
    b = pl.program_id(0)
    c = pl.program_id(1)
    tile = x_ref.shape[0]

    @pl.when(jnp.logical_and(b == 0, c == 0))
    def _():
        _fill_decay(dec_ref, inner_s, qdec_s, kdec_s, RET_CHUNK)

    @pl.when(c == 0)
    def _():
        state_ref[...] = jnp.zeros(state_ref.shape, F32)
        cbuf[0:SUBLANES, :] = jnp.zeros((SUBLANES, D_MODEL), F32)

    h = _bf(_rms(x_ref[...], ng_ref[...]))
    cos = cos_ref[...]
    sin = sin_ref[...]

    def proj(off):
        return _dot(h, _unpack_rows(win_ref[:, off:off + HEAD_DIM]))

    heads = range(HEADS)
    head_cols = [slice(hd * HEAD_DIM, (hd + 1) * HEAD_DIM) for hd in heads]

    def conv_block(hd):
        lo = hd * HEAD_DIM
        cols = head_cols[hd]
        pre, a_conv = _conv_branch(proj(OFF_CU + lo), proj(OFF_CB + lo), proj(OFF_CC + lo), proj(OFF_CG + lo),
                                   cbuf, cw_ref, cbias_ref, tile, cols)
        cbuf[0:SUBLANES, cols] = pre[tile - SUBLANES:, :]
        nconv_ref[:, cols] = pre[tile - (CONV_WIDTH - 1):, :]
        aconv_ref[:, cols] = _bf(a_conv)

    q = [_bf(_rope(proj(OFF_RQ + hd * HEAD_DIM), cos, sin)) for hd in heads]
    k = [_rope(proj(OFF_RK + hd * HEAD_DIM), cos, sin) * QK_SCALE for hd in heads]
    v = [_bf(proj(OFF_RV + hd * HEAD_DIM)) for hd in heads]
    chunks = [slice(j * RET_CHUNK, (j + 1) * RET_CHUNK) for j in range(tile // RET_CHUNK)]
    scores, cross = [], []
    for rows in chunks:
        for hd in heads:
            state = state_ref[hd]
            kc = k[hd][rows]
            scores.append(_bf(_dot_nt(q[hd][rows], _bf(kc)) * inner_s[hd]))
            cross.append(_dot(q[hd][rows], _bf(state)))
            state_ref[hd] = (state * dec_ref[HEADS + hd]
                             + _dot_tn(_bf(kc * _lanes_twice(kdec_s[hd])), v[hd][rows]))
    conv_block(0)
    ret_o = []
    for hd in heads:
        qdec = _lanes_twice(qdec_s[hd])
        ret_o.append(jnp.concatenate(
            [_dot(scores[j * HEADS + hd], v[hd][rows]) + cross[j * HEADS + hd] * qdec
             for j, rows in enumerate(chunks)], axis=0))
    mq = [_bf(proj(OFF_MQ + hd * HEAD_DIM)) for hd in heads]
    conv_block(1)
    mscores = [_dot_nt(mq[hd], mk_ref[:, head_cols[hd]]) * QK_SCALE for hd in heads]
    for hd in heads:
        cols = head_cols[hd]
        aret_ref[:, cols] = _bf(_rms(ret_o[hd], rng_ref[:, cols]) * _silu(proj(OFF_RG + hd * HEAD_DIM)))
    conv_block(2)
    probs = [_bf(_softmax_rows(mscores[hd])) for hd in heads]
    conv_block(3)
    for hd in heads:
        cols = head_cols[hd]
        om = _dot(probs[hd], mv_ref[:, cols])
        amem_ref[:, cols] = _bf(om * _silu(proj(OFF_MG + hd * HEAD_DIM)))


def _prompt_branches(x, dec, cos, sin, mk_bf, mv_bf, ng, win_pk, rng, cw, cbias):
    nb, seq, _ = x.shape
    tile = PROMPT_TILE
    tok = pl.BlockSpec((None, tile, D_MODEL), lambda b, c: (b, c, 0))
    rot = pl.BlockSpec((tile, ROT_HALF), lambda b, c: (c, 0))
    mem = pl.BlockSpec((None, MEM_LEN, D_MODEL), lambda b, c: (b, 0, 0))
    act = jax.ShapeDtypeStruct(x.shape, BF16)
    return pl.pallas_call(
        _prompt_branch_body,
        grid=(nb, seq // tile),
        in_specs=[pl.BlockSpec(memory_space=pltpu.SMEM), tok, rot, rot, mem, mem,
                  _resident(ng.shape), _resident(win_pk.shape, (D_MODEL // 2, OFF_GATE)), _resident(rng.shape),
                  _resident(cw.shape), _resident(cbias.shape)],
        out_specs=[tok, tok, tok,
                   pl.BlockSpec((None, HEADS, HEAD_DIM, HEAD_DIM), lambda b, c: (b, 0, 0, 0)),
                   pl.BlockSpec((None, CONV_WIDTH - 1, D_MODEL), lambda b, c: (b, 0, 0))],
        out_shape=[act, act, act,
                   jax.ShapeDtypeStruct((nb, HEADS, HEAD_DIM, HEAD_DIM), F32),
                   jax.ShapeDtypeStruct((nb, CONV_WIDTH - 1, D_MODEL), F32)],
        scratch_shapes=[pltpu.VMEM((SUBLANES + tile, D_MODEL), F32),
                        pltpu.VMEM((HEADS, RET_CHUNK, RET_CHUNK), F32),
                        pltpu.VMEM((HEADS, RET_CHUNK, ROT_HALF), F32),
                        pltpu.VMEM((HEADS, RET_CHUNK, ROT_HALF), F32)],
        compiler_params=pltpu.CompilerParams(dimension_semantics=("arbitrary", "arbitrary"),
                                             vmem_limit_bytes=VMEM_LIMIT_BYTES),
        name="prompt_branches",
    )(dec, x, cos, sin, mk_bf, mv_bf, ng, win_pk, rng, cw, cbias)


def _prep_body(x_ref, ng_ref, w_ref, mem_ref, mg_ref, wkv_ref,
               z_ref, wpk_ref, k_ref, v_ref, kb_ref, vb_ref, h_s, wkv_s, *, n_mem):
    n = pl.program_id(0)

    @pl.when(n == 0)
    def _():
        h_s[...] = _bf(_rms(x_ref[...], ng_ref[...]))
        wkv_s[...] = _bf(wkv_ref[...])

    w = _bf(w_ref[...])
    wpk_ref[...] = _pack_rows(w)

    @pl.when(n < OFF_GATE // w_ref.shape[1])
    def _():
        z_ref[...] = _dot(h_s[...], w)

    @pl.when(n < n_mem)
    def _():
        kv = _dot(_bf(_rms(mem_ref[...], mg_ref[...])), wkv_s[...])
        k = kv[:, :D_MODEL]
        v = kv[:, D_MODEL:]
        for hd in range(HEADS):
            lo = hd * HEAD_DIM
            k_ref[:, hd, :] = k[:, lo:lo + HEAD_DIM]
            v_ref[:, hd, :] = v[:, lo:lo + HEAD_DIM]
        kb_ref[...] = _bf(k)
        vb_ref[...] = _bf(v)


def _prep(x2d, ng, w_in, mem, mem_g, w_mem_kv):
    ntok = x2d.shape[0]
    n_mem = mem.shape[0]
    blk = D_MODEL
    steps = IN_TOTAL // blk
    z_blocks = OFF_GATE // blk
    assert n_mem <= steps

    def mem_blk(*tail):
        zeros = (0,) * len(tail)
        return pl.BlockSpec((None,) + tail, lambda n: (jnp.minimum(n, n_mem - 1),) + zeros)

    heads_shape = jax.ShapeDtypeStruct((n_mem, MEM_LEN, HEADS, HEAD_DIM), F32)
    return pl.pallas_call(
        functools.partial(_prep_body, n_mem=n_mem),
        grid=(steps,),
        in_specs=[_resident(x2d.shape), _resident(ng.shape),
                  pl.BlockSpec((D_MODEL, blk), lambda n: (0, n)),
                  mem_blk(MEM_LEN, D_MODEL), _resident(mem_g.shape), _resident(w_mem_kv.shape)],
        out_specs=[pl.BlockSpec((ntok, blk), lambda n: (0, jnp.minimum(n, z_blocks - 1))),
                   pl.BlockSpec((D_MODEL // 2, blk), lambda n: (0, n)),
                   mem_blk(MEM_LEN, HEADS, HEAD_DIM), mem_blk(MEM_LEN, HEADS, HEAD_DIM),
                   mem_blk(MEM_LEN, D_MODEL), mem_blk(MEM_LEN, D_MODEL)],
        out_shape=[jax.ShapeDtypeStruct((ntok, OFF_GATE), F32),
                   jax.ShapeDtypeStruct((D_MODEL // 2, IN_TOTAL), jnp.int32),
                   heads_shape, heads_shape,
                   jax.ShapeDtypeStruct(mem.shape, BF16), jax.ShapeDtypeStruct(mem.shape, BF16)],
        scratch_shapes=[pltpu.VMEM(x2d.shape, BF16), pltpu.VMEM(w_mem_kv.shape, BF16)],
        compiler_params=pltpu.CompilerParams(dimension_semantics=("arbitrary",),
                                             vmem_limit_bytes=VMEM_LIMIT_BYTES),
        name="prep",
    )(x2d, ng, w_in, mem, mem_g, w_mem_kv)


def _tail_stages(x_ref, acts, ng_ref, wg_refs, bg_ref, wro_ref, wco_ref, wmo_ref, wout_ref, fg_ref, y_ref):
    live = {}
    w_refs = (wro_ref, wco_ref, wmo_ref)

    def gate(branch):
        def run(between=()):
            if branch == 0:
                live["h"] = _bf(_rms(x_ref[...], ng_ref[...]))
            nblk = max(len(between), 1)
            width = D_MODEL // nblk
            parts = []
            for i in range(nblk):
                cols = slice(i * width, (i + 1) * width)
                logits = _dot(live["h"], _unpack_rows(wg_refs[branch][:, cols]))
                parts.append(jax.nn.sigmoid(logits + bg_ref[:, branch * D_MODEL + i * width:
                                                            branch * D_MODEL + (i + 1) * width]))
                if between:
                    between[i]()
            live["gate"] = jnp.concatenate(parts, axis=-1)
        return run

    def project(branch):
        def run(between=()):
            nblk = max(len(between), 1)
            width = D_MODEL // nblk
            parts = []
            for i in range(nblk):
                cols = slice(i * width, (i + 1) * width)
                parts.append(live["gate"][:, cols] * _dot(acts[branch](), _unpack_rows(w_refs[branch][:, cols])))
                if between:
                    between[i]()
            y_branch = jnp.concatenate(parts, axis=-1)
            live["merged"] = y_branch if branch == 0 else live["merged"] + y_branch
        return run

    def out_project():
        live["y"] = x_ref[...] + _dot(_bf(live["merged"]), _unpack_rows(wout_ref[...]))

    def finish():
        y_ref[...] = _rms(live["y"], fg_ref[...])

    stages = []
    for branch in range(N_BRANCH):
        stages += [gate(branch), project(branch)]
    return stages + [out_project, finish]


def _sample_stages(z_ref, cos_ref, sin_ref, state_ref, cstate_ref, mk_ref, mv_ref, rng_ref, cw_ref, cbias_ref,
                   store_act, nstate_ref, nconv_ref, cbuf, inner_s, qdec_s, kdec_s, dec_ref):
    group, seq, _ = z_ref.shape
    pairs = [(g, hd) for g in range(group) for hd in range(HEADS)]
    live = {}

    def col(g, off, width=HEAD_DIM):
        return z_ref[g, :, off:off + width]

    def ret_inputs():
        cos = cos_ref[...]
        sin = sin_ref[...]
        live["q"] = {p: _bf(_rope(col(p[0], OFF_RQ + p[1] * HEAD_DIM), cos, sin)) for p in pairs}
        live["k"] = {p: _rope(col(p[0], OFF_RK + p[1] * HEAD_DIM), cos, sin) * QK_SCALE for p in pairs}
        live["v"] = {p: _bf(col(p[0], OFF_RV + p[1] * HEAD_DIM)) for p in pairs}

    def ret_products():
        live["scores"], live["cross"] = {}, {}
        for p in pairs:
            g, hd = p
            state = state_ref[g, hd]
            k = live["k"][p]
            live["scores"][p] = _bf(_dot_nt(live["q"][p], _bf(k)) * inner_s[hd])
            live["cross"][p] = _dot(live["q"][p], _bf(state))
            nstate_ref[g, hd] = (state * dec_ref[HEADS + hd]
                                 + _dot_tn(_bf(k * _lanes_twice(kdec_s[hd])), live["v"][p]))

    def ret_outputs():
        for p in pairs:
            g, hd = p
            lo = hd * HEAD_DIM
            o = _dot(live["scores"][p], live["v"][p]) + live["cross"][p] * _lanes_twice(qdec_s[hd])
            o = _rms(o, rng_ref[:, lo:lo + HEAD_DIM])
            store_act(0, g, slice(lo, lo + HEAD_DIM), o * _silu(col(g, OFF_RG + lo)))

    def conv():
        for g in range(group):
            base = g * (SUBLANES + seq)
            cbuf[base:base + SUBLANES, :] = jnp.zeros((SUBLANES, D_MODEL), F32)
            cbuf[base + SUBLANES - (CONV_WIDTH - 1):base + SUBLANES, :] = cstate_ref[g]
            pre, a_conv = _conv_branch(col(g, OFF_CU, D_MODEL), col(g, OFF_CB, D_MODEL), col(g, OFF_CC, D_MODEL),
                                       col(g, OFF_CG, D_MODEL), cbuf.at[base:base + SUBLANES + seq], cw_ref,
                                       cbias_ref, seq, slice(0, D_MODEL))
            nconv_ref[g] = pre[seq - (CONV_WIDTH - 1):, :]
            store_act(1, g, slice(0, D_MODEL), a_conv)

    def mem_score_piece(g, j):
        def run():
            if j == 0:
                live.setdefault("ms", {})[g] = []
                live.setdefault("mq", {})[g] = _bf(
                    jnp.concatenate([col(g, OFF_MQ + hd * HEAD_DIM) for hd in range(HEADS)], axis=0))
            keys = mk_ref[g, j * MEM_PIECE:(j + 1) * MEM_PIECE].reshape(MEM_PIECE * HEADS, HEAD_DIM)
            live["ms"][g].append(_dot_nt(live["mq"][g], _bf(keys)) * QK_SCALE)
        return run

    def mem_probs():
        live["mp"] = []
        for g in range(group):
            s = jnp.concatenate(live["ms"][g], axis=-1)
            row_head = lax.broadcasted_iota(jnp.int32, s.shape, 0) // seq
            col_head = lax.broadcasted_iota(jnp.int32, s.shape, 1) % HEADS
            live["mp"].append(_bf(_softmax_rows(jnp.where(row_head == col_head, s, -jnp.inf))))

    def mem_outputs():
        for g in range(group):
            mv2d = _bf(mv_ref[g].reshape(MEM_LEN * HEADS, HEAD_DIM))
            om_all = _dot(live["mp"][g], mv2d)
            for hd in range(HEADS):
                lo = hd * HEAD_DIM
                om = om_all[hd * seq:(hd + 1) * seq, :]
                store_act(2, g, slice(lo, lo + HEAD_DIM), om * _silu(col(g, OFF_MG + lo)))

    pieces = [[mem_score_piece(g, j) for j in range(MEM_LEN // MEM_PIECE)] for g in range(group)]
    assert group == 2
    return [(ret_inputs, ()), (ret_products, ()), (None, pieces[0]), (ret_outputs, pieces[1]),
            (mem_probs, ()), (conv, ()), (mem_outputs, ())]


def _tail_with_sample_body(x_ref, aret_ref, aconv_ref, amem_ref, ng_ref, wg0_ref, wg1_ref, wg2_ref, bg_ref,
                           wro_ref, wco_ref, wmo_ref, wout_ref, fg_ref,
                           dec_ref, z_ref, cos_ref, sin_ref, state_ref, cstate_ref, mk_ref, mv_ref, rng_ref,
                           cw_ref, cbias_ref, xs_ref,
                           y_ref, nstate_ref, nconv_ref, ys_ref,
                           cbuf, inner_s, qdec_s, kdec_s, sact_s):
    i = pl.program_id(0)
    group, seq, _ = z_ref.shape
    steps_per_tile = xs_ref.shape[0] // (group * seq)
    slot = lax.rem(i, steps_per_tile)

    @pl.when(i == 0)
    def _():
        _fill_decay(dec_ref, inner_s, qdec_s, kdec_s, seq)

    def store_act(branch, g, cols, value):
        row = pl.multiple_of((slot * group + g) * seq, seq)
        sact_s[branch, pl.ds(row, seq), cols] = value

    weights = ((wg0_ref, wg1_ref, wg2_ref), bg_ref, wro_ref, wco_ref, wmo_ref, wout_ref, fg_ref)
    prompt_acts = [lambda ref=ref: ref[...] for ref in (aret_ref, aconv_ref, amem_ref)]
    tail = _tail_stages(x_ref, prompt_acts, ng_ref, *weights, y_ref)
    sample = _sample_stages(z_ref, cos_ref, sin_ref, state_ref, cstate_ref, mk_ref, mv_ref, rng_ref, cw_ref,
                            cbias_ref, store_act, nstate_ref, nconv_ref, cbuf, inner_s, qdec_s, kdec_s, dec_ref)
    assert len(sample) == len(tail) - 1
    for j, tail_stage in enumerate(tail):
        before, between = sample[j] if j < len(sample) - 1 else sample[-1] if j == len(tail) - 1 else (None, ())
        if before is not None:
            before()
        if between:
            tail_stage(between=between)
        else:
            tail_stage()

    @pl.when(slot == steps_per_tile - 1)
    def _():
        sample_acts = [lambda branch=branch: _bf(sact_s[branch]) for branch in range(N_BRANCH)]
        for stage in _tail_stages(xs_ref, sample_acts, ng_ref, *weights, ys_ref):
            stage()


def _tail_specs(tile, ng, bg, wro, wco, wmo, wout, fg):
    tok = pl.BlockSpec((tile, D_MODEL), lambda i: (i, 0))

    def gate_cols(branch):
        blk = OFF_GATE // D_MODEL + branch
        return pl.BlockSpec((D_MODEL // 2, D_MODEL), lambda i: (0, blk), pipeline_mode=pl.Buffered(1))

    in_specs = [tok, tok, tok, tok,
                _resident(ng.shape), gate_cols(0), gate_cols(1), gate_cols(2), _resident(bg.shape),
                _resident(wro.shape), _resident(wco.shape), _resident(wmo.shape), _resident(wout.shape),
                _resident(fg.shape)]
    return tok, in_specs


def _tail_with_sample(x2d, aret, aconv, amem, ng, win_pk, bg, wro, wco, wmo, wout, fg,
                      dec, z, cos, sin, state, cstate, mk, mv, rng, cw, cbias, xs2d):
    steps = x2d.shape[0] // TAIL_TILE
    nb, seq, _ = z.shape
    grp = nb // steps
    assert grp * steps == nb and TAIL_TILE % (grp * seq) == 0
    steps_per_tile = TAIL_TILE // (grp * seq)
    tok, in_specs = _tail_specs(TAIL_TILE, ng, bg, wro, wco, wmo, wout, fg)
    sample_tok = pl.BlockSpec((TAIL_TILE, D_MODEL), lambda i: (i // steps_per_tile, 0))

    def per_req(*tail):
        zeros = (0,) * len(tail)
        return pl.BlockSpec((grp,) + tail, lambda i: (i,) + zeros)

    def const(shape):
        zeros = (0,) * len(shape)
        return pl.BlockSpec(shape, lambda i: zeros)

    return pl.pallas_call(
        _tail_with_sample_body,
        grid=(steps,),
        in_specs=in_specs + [pl.BlockSpec(memory_space=pltpu.SMEM),
                             per_req(seq, OFF_GATE), const(cos.shape), const(sin.shape),
                             per_req(HEADS, HEAD_DIM, HEAD_DIM), per_req(CONV_WIDTH - 1, D_MODEL),
                             per_req(MEM_LEN, HEADS, HEAD_DIM), per_req(MEM_LEN, HEADS, HEAD_DIM),
                             const(rng.shape), const(cw.shape), const(cbias.shape), sample_tok],
        out_specs=[tok, per_req(HEADS, HEAD_DIM, HEAD_DIM), per_req(CONV_WIDTH - 1, D_MODEL), sample_tok],
        out_shape=[jax.ShapeDtypeStruct(x2d.shape, F32),
                   jax.ShapeDtypeStruct(state.shape, F32),
                   jax.ShapeDtypeStruct(cstate.shape, F32),
                   jax.ShapeDtypeStruct(xs2d.shape, F32)],
        scratch_shapes=[pltpu.VMEM((grp * (SUBLANES + seq), D_MODEL), F32),
                        pltpu.VMEM((HEADS, seq, seq), F32),
                        pltpu.VMEM((HEADS, seq, ROT_HALF), F32),
                        pltpu.VMEM((HEADS, seq, ROT_HALF), F32),
                        pltpu.VMEM((N_BRANCH, TAIL_TILE, D_MODEL), F32)],
        compiler_params=pltpu.CompilerParams(dimension_semantics=("arbitrary",),
                                             vmem_limit_bytes=VMEM_LIMIT_BYTES),
        name="tail_with_sample",
    )(x2d, aret, aconv, amem, ng, win_pk, win_pk, win_pk, bg, wro, wco, wmo, wout, fg,
      dec, z, cos, sin, state, cstate, mk, mv, rng, cw, cbias, xs2d)


def _rope_tables(pos):
    inv = ROPE_BASE ** (-jnp.arange(ROT_HALF, dtype=F32) / ROT_HALF)
    ang = pos.astype(F32)[:, None] * inv[None, :]
    return jnp.cos(ang), jnp.sin(ang)


def _decay_scalars(chunk):
    lg = jnp.log1p(-jnp.exp2(-5.0 - jnp.arange(HEADS, dtype=F32)))
    return jnp.concatenate([lg, jnp.exp(lg * chunk)])


def kernel(x_prompt, x_sample, state_ret, state_conv, cache_mem_k, cache_mem_v, mem_prompt, norm_g, w_in, b_gate, ret_norm_g, conv_w, conv_b, w_ret_o, w_conv_o, w_mem_o, w_out, mem_norm_g, w_mem_kv, final_norm_g):
    assert norm_g.shape[0] == 1, "single-layer trunk"
    nbp, seq_p, _ = x_prompt.shape
    nbs, seq_s, _ = x_sample.shape
    assert seq_p % PROMPT_TILE == 0 and PROMPT_TILE % RET_CHUNK == 0 and (nbp * seq_p) % TAIL_TILE == 0
    assert seq_s == SUBLANES and (nbs * seq_s) % TAIL_TILE == 0

    ng = norm_g[0][None, :]
    bg = b_gate[0][None, :]
    rng = ret_norm_g[0].reshape(1, D_MODEL)
    cw = conv_w[0]
    cbias = conv_b[0][None, :]
    wro, wco, wmo, wout = _pack_weights(w_ret_o[0], w_conv_o[0], w_mem_o[0], w_out[0])
    fg = final_norm_g[None, :]

    cos_p, sin_p = _rope_tables(jnp.arange(seq_p, dtype=jnp.int32))
    cos_s, sin_s = _rope_tables(PAST_LEN + jnp.arange(seq_s, dtype=jnp.int32))

    xs2d = x_sample.reshape(nbs * seq_s, D_MODEL)
    z, win_pk, mk, mv, mk_bf, mv_bf = _prep(xs2d, ng, w_in[0], mem_prompt, mem_norm_g[0][None, :],
                                            w_mem_kv[0])
    z = z.reshape(nbs, seq_s, OFF_GATE)
    tail_weights = (ng, win_pk, bg, wro, wco, wmo, wout, fg)

    aret_p, aconv_p, amem_p, ret_p, conv_p = _prompt_branches(
        x_prompt, _decay_scalars(RET_CHUNK), cos_p, sin_p, mk_bf, mv_bf, ng, win_pk, rng, cw, cbias)

    flat_p = lambda a: a.reshape(nbp * seq_p, D_MODEL)
    y_prompt, ret_s, conv_s, y_sample = _tail_with_sample(
        flat_p(x_prompt), flat_p(aret_p), flat_p(aconv_p), flat_p(amem_p), *tail_weights,
        _decay_scalars(seq_s), z, cos_s, sin_s, state_ret[0], state_conv[0], cache_mem_k[0], cache_mem_v[0],
        rng, cw, cbias, xs2d)

    return (y_prompt.reshape(x_prompt.shape), y_sample.reshape(x_sample.shape),
            ret_p[None], ret_s[None], conv_p[None], conv_s[None], mk[None], mv[None])
```

```python
import functools

import jax
import jax.numpy as jnp
from jax import lax
from jax.experimental import pallas as pl
from jax.experimental.pallas import tpu as pltpu

F32 = jnp.float32
BF16 = jnp.bfloat16

D_MODEL = 1024
HEADS = 4
HEAD_DIM = D_MODEL // HEADS
ROT_HALF = HEAD_DIM // 2
ROPE_BASE = 10000.0
CONV_WIDTH = 3
MEM_LEN = 256
MEM_PIECE = 64
N_BRANCH = 3
EPS = 1e-6
PAST_LEN = 16384
QK_SCALE = HEAD_DIM ** -0.5

OFF_RQ, OFF_RK, OFF_RV, OFF_RG = 0, 1024, 2048, 3072
OFF_CU, OFF_CB, OFF_CC, OFF_CG = 4096, 5120, 6144, 7168
OFF_MQ, OFF_MG = 8192, 9216
OFF_GATE = 10240
IN_TOTAL = OFF_GATE + N_BRANCH * D_MODEL

SUBLANES = 8
VMEM_LIMIT_BYTES = 60 * 1024 * 1024

PROMPT_TILE = 512
RET_CHUNK = 256
TAIL_TILE = 256


def _bf(x):
    return x.astype(BF16)


def _dot(a, b):
    return jnp.dot(a, b, preferred_element_type=F32)


def _dot_nt(a, b):
    return lax.dot_general(a, b, (((1,), (1,)), ((), ())), preferred_element_type=F32)


def _dot_tn(a, b):
    return lax.dot_general(a, b, (((0,), (0,)), ((), ())), preferred_element_type=F32)


def _rms(x, g):
    return x * lax.rsqrt(jnp.mean(x * x, axis=-1, keepdims=True) + EPS) * g


def _silu(x):
    return x * jax.nn.sigmoid(x)


def _rope(x, cos, sin):
    x1 = x[:, :ROT_HALF]
    x2 = x[:, ROT_HALF:]
    return jnp.concatenate([x1 * cos - x2 * sin, x1 * sin + x2 * cos], axis=-1)


def _lanes_twice(t):
    return jnp.concatenate([t, t], axis=-1)


def _pack_rows(w_bf):
    return pltpu.bitcast(w_bf, jnp.int32)


def _unpack_rows(w_packed):
    return pltpu.bitcast(w_packed, BF16)


def _fill_decay(dec_ref, inner_s, qdec_s, kdec_s, chunk):
    i = lax.broadcasted_iota(jnp.int32, (chunk, chunk), 0)
    j = lax.broadcasted_iota(jnp.int32, (chunk, chunk), 1)
    diff = (i - j).astype(F32)
    r = lax.broadcasted_iota(jnp.int32, (chunk, ROT_HALF), 0).astype(F32)
    for hd in range(HEADS):
        lg = dec_ref[hd]
        inner_s[hd] = jnp.where(diff >= 0, jnp.exp(lg * jnp.maximum(diff, 0.0)), 0.0)
        qdec_s[hd] = jnp.exp(lg * (r + 1.0))
        kdec_s[hd] = jnp.exp(lg * (chunk - 1.0 - r))


def _conv_branch(cu, cb, cc, cg, cbuf, cw_ref, cbias_ref, rows, cols):
    pre = cc * cu
    cbuf[SUBLANES:SUBLANES + rows, cols] = pre
    conv = (cbias_ref[:, cols]
            + cw_ref[0:1, cols] * cbuf[SUBLANES - 2:SUBLANES - 2 + rows, cols]
            + cw_ref[1:2, cols] * cbuf[SUBLANES - 1:SUBLANES - 1 + rows, cols]
            + cw_ref[2:3, cols] * pre)
    return pre, cb * conv * _silu(cg)


def _softmax_rows(s):
    e = jnp.exp(s - jnp.max(s, axis=-1, keepdims=True))
    return e * (1.0 / jnp.sum(e, axis=-1, keepdims=True))


def _resident(shape, block=None):
    zeros = (0,) * len(shape)
    return pl.BlockSpec(block or shape, lambda *_: zeros, pipeline_mode=pl.Buffered(1))


def _pack_weights_body(*refs):
    n = len(refs) // 2
    for w_ref, out_ref in zip(refs[:n], refs[n:]):
        out_ref[...] = _pack_rows(_bf(w_ref[...]))


def _pack_weights(*ws):
    return pl.pallas_call(
        _pack_weights_body,
        out_shape=[jax.ShapeDtypeStruct((w.shape[0] // 2, w.shape[1]), jnp.int32) for w in ws],
        compiler_params=pltpu.CompilerParams(vmem_limit_bytes=VMEM_LIMIT_BYTES),
        name="pack_weights",
    )(*ws)


def _prompt_branch_body(dec_ref, x_ref, cos_ref, sin_ref, mk_ref, mv_ref, ng_ref, win_ref, rng_ref,
                        cw_ref, cbias_ref,
                        aret_ref, aconv_ref, amem_ref, state_ref, nconv_ref,
                        cbuf, inner_s, qdec_s, kdec_s):
    b = pl.program_id(0)
    c = pl.program_id(1)
    tile = x_ref.shape[0]

    @pl.when(jnp.logical_and(b == 0, c == 0))
    def _():
        _fill_decay(dec_ref, inner_s, qdec_s, kdec_s, RET_CHUNK)

    @pl.when(c == 0)
    def _():
        state_ref[...] = jnp.zeros(state_ref.shape, F32)
        cbuf[0:SUBLANES, :] = jnp.zeros((SUBLANES, D_MODEL), F32)

    h = _bf(_rms(x_ref[...], ng_ref[...]))
    cos = cos_ref[...]
    sin = sin_ref[...]

    def proj(off):
        return _dot(h, _unpack_rows(win_ref[:, off:off + HEAD_DIM]))

    heads = range(HEADS)
    head_cols = [slice(hd * HEAD_DIM, (hd + 1) * HEAD_DIM) for hd in heads]

    def conv_block(hd):
        lo = hd * HEAD_DIM
        cols = head_cols[hd]
        pre, a_conv = _conv_branch(proj(OFF_CU + lo), proj(OFF_CB + lo), proj(OFF_CC + lo), proj(OFF_CG + lo),
                                   cbuf, cw_ref, cbias_ref, tile, cols)
        cbuf[0:SUBLANES, cols] = pre[tile - SUBLANES:, :]
        nconv_ref[:, cols] = pre[tile - (CONV_WIDTH - 1):, :]
        aconv_ref[:, cols] = _bf(a_conv)

    q = [_bf(_rope(proj(OFF_RQ + hd * HEAD_DIM), cos, sin)) for hd in heads]
    k = [_rope(proj(OFF_RK + hd * HEAD_DIM), cos, sin) * QK_SCALE for hd in heads]
    v = [_bf(proj(OFF_RV + hd * HEAD_DIM)) for hd in heads]
    chunks = [slice(j * RET_CHUNK, (j + 1) * RET_CHUNK) for j in range(tile // RET_CHUNK)]
    scores, cross = [], []
    for rows in chunks:
        for hd in heads:
            state = state_ref[hd]
            kc = k[hd][rows]
            scores.append(_bf(_dot_nt(q[hd][rows], _bf(kc)) * inner_s[hd]))
            cross.append(_dot(q[hd][rows], _bf(state)))
            state_ref[hd] = (state * dec_ref[HEADS + hd]
                             + _dot_tn(_bf(kc * _lanes_twice(kdec_s[hd])), v[hd][rows]))
    conv_block(0)
    ret_o = []
    for hd in heads:
        qdec = _lanes_twice(qdec_s[hd])
        ret_o.append(jnp.concatenate(
            [_dot(scores[j * HEADS + hd], v[hd][rows]) + cross[j * HEADS + hd] * qdec
             for j, rows in enumerate(chunks)], axis=0))
    mq = [_bf(proj(OFF_MQ + hd * HEAD_DIM)) for hd in heads]
    conv_block(1)
    mscores = [_dot_nt(mq[hd], mk_ref[:, head_cols[hd]]) * QK_SCALE for hd in heads]
    for hd in heads:
        cols = head_cols[hd]
        aret_ref[:, cols] = _bf(_rms(ret_o[hd], rng_ref[:, cols]) * _silu(proj(OFF_RG + hd * HEAD_DIM)))
    conv_block(2)
    probs = [_bf(_softmax_rows(mscores[hd])) for hd in heads]
    conv_block(3)
    for hd in heads:
        cols = head_cols[hd]
        om = _dot(probs[hd], mv_ref[:, cols])
        amem_ref[:, cols] = _bf(om * _silu(proj(OFF_MG + hd * HEAD_DIM)))


def _prompt_branches(x, dec, cos, sin, mk_bf, mv_bf, ng, win_pk, rng, cw, cbias):
    nb, seq, _ = x.shape
    tile = PROMPT_TILE
    tok = pl.BlockSpec((None, tile, D_MODEL), lambda b, c: (b, c, 0))
    rot = pl.BlockSpec((tile, ROT_HALF), lambda b, c: (c, 0))
    mem = pl.BlockSpec((None, MEM_LEN, D_MODEL), lambda b, c: (b, 0, 0))
    act = jax.ShapeDtypeStruct(x.shape, BF16)
    return pl.pallas_call(
        _prompt_branch_body,
        grid=(nb, seq // tile),
        in_specs=[pl.BlockSpec(memory_space=pltpu.SMEM), tok, rot, rot, mem, mem,
                  _resident(ng.shape), _resident(win_pk.shape, (D_MODEL // 2, OFF_GATE)), _resident(rng.shape),
                  _resident(cw.shape), _resident(cbias.shape)],
        out_specs=[tok, tok, tok,
                   pl.BlockSpec((None, HEADS, HEAD_DIM, HEAD_DIM), lambda b, c: (b, 0, 0, 0)),
                   pl.BlockSpec((None, CONV_WIDTH - 1, D_MODEL), lambda b, c: (b, 0, 0))],
        out_shape=[act, act, act,
                   jax.ShapeDtypeStruct((nb, HEADS, HEAD_DIM, HEAD_DIM), F32),
                   jax.ShapeDtypeStruct((nb, CONV_WIDTH - 1, D_MODEL), F32)],
        scratch_shapes=[pltpu.VMEM((SUBLANES + tile, D_MODEL), F32),
                        pltpu.VMEM((HEADS, RET_CHUNK, RET_CHUNK), F32),
                        pltpu.VMEM((HEADS, RET_CHUNK, ROT_HALF), F32),
                        pltpu.VMEM((HEADS, RET_CHUNK, ROT_HALF), F32)],
        compiler_params=pltpu.CompilerParams(dimension_semantics=("arbitrary", "arbitrary"),
                                             vmem_limit_bytes=VMEM_LIMIT_BYTES),
        name="prompt_branches",
    )(dec, x, cos, sin, mk_bf, mv_bf, ng, win_pk, rng, cw, cbias)


def _prep_body(x_ref, ng_ref, w_ref, mem_ref, mg_ref, wkv_ref,
               z_ref, wpk_ref, k_ref, v_ref, kb_ref, vb_ref, h_s, wkv_s, *, n_mem):
    n = pl.program_id(0)

    @pl.when(n == 0)
    def _():
        h_s[...] = _bf(_rms(x_ref[...], ng_ref[...]))
        wkv_s[...] = _bf(wkv_ref[...])

    w = _bf(w_ref[...])
    wpk_ref[...] = _pack_rows(w)

    @pl.when(n < OFF_GATE // w_ref.shape[1])
    def _():
        z_ref[...] = _dot(h_s[...], w)

    @pl.when(n < n_mem)
    def _():
        kv = _dot(_bf(_rms(mem_ref[...], mg_ref[...])), wkv_s[...])
        k = kv[:, :D_MODEL]
        v = kv[:, D_MODEL:]
        for hd in range(HEADS):
            lo = hd * HEAD_DIM
            k_ref[:, hd, :] = k[:, lo:lo + HEAD_DIM]
            v_ref[:, hd, :] = v[:, lo:lo + HEAD_DIM]
        kb_ref[...] = _bf(k)
        vb_ref[...] = _bf(v)


def _prep(x2d, ng, w_in, mem, mem_g, w_mem_kv):
    ntok = x2d.shape[0]
    n_mem = mem.shape[0]
    blk = D_MODEL
    steps = IN_TOTAL // blk
    z_blocks = OFF_GATE // blk
    assert n_mem <= steps

    def mem_blk(*tail):
        zeros = (0,) * len(tail)
        return pl.BlockSpec((None,) + tail, lambda n: (jnp.minimum(n, n_mem - 1),) + zeros)

    heads_shape = jax.ShapeDtypeStruct((n_mem, MEM_LEN, HEADS, HEAD_DIM), F32)
    return pl.pallas_call(
        functools.partial(_prep_body, n_mem=n_mem),
        grid=(steps,),
        in_specs=[_resident(x2d.shape), _resident(ng.shape),
                  pl.BlockSpec((D_MODEL, blk), lambda n: (0, n)),
                  mem_blk(MEM_LEN, D_MODEL), _resident(mem_g.shape), _resident(w_mem_kv.shape)],
        out_specs=[pl.BlockSpec((ntok, blk), lambda n: (0, jnp.minimum(n, z_blocks - 1))),
                   pl.BlockSpec((D_MODEL // 2, blk), lambda n: (0, n)),
                   mem_blk(MEM_LEN, HEADS, HEAD_DIM), mem_blk(MEM_LEN, HEADS, HEAD_DIM),
                   mem_blk(MEM_LEN, D_MODEL), mem_blk(MEM_LEN, D_MODEL)],
        out_shape=[jax.ShapeDtypeStruct((ntok, OFF_GATE), F32),
                   jax.ShapeDtypeStruct((D_MODEL // 2, IN_TOTAL), jnp.int32),
                   heads_shape, heads_shape,
                   jax.ShapeDtypeStruct(mem.shape, BF16), jax.ShapeDtypeStruct(mem.shape, BF16)],
        scratch_shapes=[pltpu.VMEM(x2d.shape, BF16), pltpu.VMEM(w_mem_kv.shape, BF16)],
        compiler_params=pltpu.CompilerParams(dimension_semantics=("arbitrary",),
                                             vmem_limit_bytes=VMEM_LIMIT_BYTES),
        name="prep",
    )(x2d, ng, w_in, mem, mem_g, w_mem_kv)


def _tail_stages(x_ref, acts, ng_ref, wg_refs, bg_ref, wro_ref, wco_ref, wmo_ref, wout_ref, fg_ref, y_ref):
    live = {}
    w_refs = (wro_ref, wco_ref, wmo_ref)

    def gate(branch):
        def run(between=()):
            if branch == 0:
                live["h"] = _bf(_rms(x_ref[...], ng_ref[...]))
            nblk = max(len(between), 1)
            width = D_MODEL // nblk
            parts = []
            for i in range(nblk):
                cols = slice(i * width, (i + 1) * width)
                logits = _dot(live["h"], _unpack_rows(wg_refs[branch][:, cols]))
                parts.append(jax.nn.sigmoid(logits + bg_ref[:, branch * D_MODEL + i * width:
                                                            branch * D_MODEL + (i + 1) * width]))
                if between:
                    between[i]()
            live["gate"] = jnp.concatenate(parts, axis=-1)
        return run

    def project(branch):
        def run(between=()):
            nblk = max(len(between), 1)
            width = D_MODEL // nblk
            parts = []
            for i in range(nblk):
                cols = slice(i * width, (i + 1) * width)
                parts.append(live["gate"][:, cols] * _dot(acts[branch](), _unpack_rows(w_refs[branch][:, cols])))
                if between:
                    between[i]()
            y_branch = jnp.concatenate(parts, axis=-1)
            live["merged"] = y_branch if branch == 0 else live["merged"] + y_branch
        return run

    def out_project():
        live["y"] = x_ref[...] + _dot(_bf(live["merged"]), _unpack_rows(wout_ref[...]))

    def finish():
        y_ref[...] = _rms(live["y"], fg_ref[...])

    stages = []
    for branch in range(N_BRANCH):
        stages += [gate(branch), project(branch)]
    return stages + [out_project, finish]


def _sample_stages(z_ref, cos_ref, sin_ref, state_ref, cstate_ref, mk_ref, mv_ref, rng_ref, cw_ref, cbias_ref,
                   store_act, nstate_ref, nconv_ref, cbuf, inner_s, qdec_s, kdec_s, dec_ref):
    group, seq, _ = z_ref.shape
    pairs = [(g, hd) for g in range(group) for hd in range(HEADS)]
    live = {}

    def col(g, off, width=HEAD_DIM):
        return z_ref[g, :, off:off + width]

    def ret_inputs():
        cos = cos_ref[...]
        sin = sin_ref[...]
        live["q"] = {p: _bf(_rope(col(p[0], OFF_RQ + p[1] * HEAD_DIM), cos, sin)) for p in pairs}
        live["k"] = {p: _rope(col(p[0], OFF_RK + p[1] * HEAD_DIM), cos, sin) * QK_SCALE for p in pairs}
        live["v"] = {p: _bf(col(p[0], OFF_RV + p[1] * HEAD_DIM)) for p in pairs}

    def ret_products():
        live["scores"], live["cross"] = {}, {}
        for p in pairs:
            g, hd = p
            state = state_ref[g, hd]
            k = live["k"][p]
            live["scores"][p] = _bf(_dot_nt(live["q"][p], _bf(k)) * inner_s[hd])
            live["cross"][p] = _dot(live["q"][p], _bf(state))
            nstate_ref[g, hd] = (state * dec_ref[HEADS + hd]
                                 + _dot_tn(_bf(k * _lanes_twice(kdec_s[hd])), live["v"][p]))

    def ret_outputs():
        for p in pairs:
            g, hd = p
            lo = hd * HEAD_DIM
            o = _dot(live["scores"][p], live["v"][p]) + live["cross"][p] * _lanes_twice(qdec_s[hd])
            o = _rms(o, rng_ref[:, lo:lo + HEAD_DIM])
            store_act(0, g, slice(lo, lo + HEAD_DIM), o * _silu(col(g, OFF_RG + lo)))

    def conv():
        for g in range(group):
            base = g * (SUBLANES + seq)
            cbuf[base:base + SUBLANES, :] = jnp.zeros((SUBLANES, D_MODEL), F32)
            cbuf[base + SUBLANES - (CONV_WIDTH - 1):base + SUBLANES, :] = cstate_ref[g]
            pre, a_conv = _conv_branch(col(g, OFF_CU, D_MODEL), col(g, OFF_CB, D_MODEL), col(g, OFF_CC, D_MODEL),
                                       col(g, OFF_CG, D_MODEL), cbuf.at[base:base + SUBLANES + seq], cw_ref,
                                       cbias_ref, seq, slice(0, D_MODEL))
            nconv_ref[g] = pre[seq - (CONV_WIDTH - 1):, :]
            store_act(1, g, slice(0, D_MODEL), a_conv)

    def mem_score_piece(g, j):
        def run():
            if j == 0:
                live.setdefault("ms", {})[g] = []
                live.setdefault("mq", {})[g] = _bf(
                    jnp.concatenate([col(g, OFF_MQ + hd * HEAD_DIM) for hd in range(HEADS)], axis=0))
            keys = mk_ref[g, j * MEM_PIECE:(j + 1) * MEM_PIECE].reshape(MEM_PIECE * HEADS, HEAD_DIM)
            live["ms"][g].append(_dot_nt(live["mq"][g], _bf(keys)) * QK_SCALE)
        return run

    def mem_probs():
        live["mp"] = []
        for g in range(group):
            s = jnp.concatenate(live["ms"][g], axis=-1)
            row_head = lax.broadcasted_iota(jnp.int32, s.shape, 0) // seq
            col_head = lax.broadcasted_iota(jnp.int32, s.shape, 1) % HEADS
            live["mp"].append(_bf(_softmax_rows(jnp.where(row_head == col_head, s, -jnp.inf))))

    def mem_outputs():
        for g in range(group):
            mv2d = _bf(mv_ref[g].reshape(MEM_LEN * HEADS, HEAD_DIM))
            om_all = _dot(live["mp"][g], mv2d)
            for hd in range(HEADS):
                lo = hd * HEAD_DIM
                om = om_all[hd * seq:(hd + 1) * seq, :]
                store_act(2, g, slice(lo, lo + HEAD_DIM), om * _silu(col(g, OFF_MG + lo)))

    pieces = [[mem_score_piece(g, j) for j in range(MEM_LEN // MEM_PIECE)] for g in range(group)]
    assert group == 2
    return [(ret_inputs, ()), (ret_products, ()), (None, pieces[0]), (ret_outputs, pieces[1]),
            (mem_probs, ()), (conv, ()), (mem_outputs, ())]


def _tail_with_sample_body(x_ref, aret_ref, aconv_ref, amem_ref, ng_ref, wg0_ref, wg1_ref, wg2_ref, bg_ref,
                           wro_ref, wco_ref, wmo_ref, wout_ref, fg_ref,
                           dec_ref, z_ref, cos_ref, sin_ref, state_ref, cstate_ref, mk_ref, mv_ref, rng_ref,
                           cw_ref, cbias_ref, xs_ref,
                           y_ref, nstate_ref, nconv_ref, ys_ref,
                           cbuf, inner_s, qdec_s, kdec_s, sact_s):
    i = pl.program_id(0)
    group, seq, _ = z_ref.shape
    steps_per_tile = xs_ref.shape[0] // (group * seq)
    slot = lax.rem(i, steps_per_tile)

    @pl.when(i == 0)
    def _():
        _fill_decay(dec_ref, inner_s, qdec_s, kdec_s, seq)

    def store_act(branch, g, cols, value):
        row = pl.multiple_of((slot * group + g) * seq, seq)
        sact_s[branch, pl.ds(row, seq), cols] = value

    weights = ((wg0_ref, wg1_ref, wg2_ref), bg_ref, wro_ref, wco_ref, wmo_ref, wout_ref, fg_ref)
    prompt_acts = [lambda ref=ref: ref[...] for ref in (aret_ref, aconv_ref, amem_ref)]
    tail = _tail_stages(x_ref, prompt_acts, ng_ref, *weights, y_ref)
    sample = _sample_stages(z_ref, cos_ref, sin_ref, state_ref, cstate_ref, mk_ref, mv_ref, rng_ref, cw_ref,
                            cbias_ref, store_act, nstate_ref, nconv_ref, cbuf, inner_s, qdec_s, kdec_s, dec_ref)
    assert len(sample) == len(tail) - 1
    for j, tail_stage in enumerate(tail):
        before, between = sample[j] if j < len(sample) - 1 else sample[-1] if j == len(tail) - 1 else (None, ())
        if before is not None:
            before()
        if between:
            tail_stage(between=between)
        else:
            tail_stage()

    @pl.when(slot == steps_per_tile - 1)
    def _():
        sample_acts = [lambda branch=branch: _bf(sact_s[branch]) for branch in range(N_BRANCH)]
        for stage in _tail_stages(xs_ref, sample_acts, ng_ref, *weights, ys_ref):
            stage()


def _tail_specs(tile, ng, bg, wro, wco, wmo, wout, fg):
    tok = pl.BlockSpec((tile, D_MODEL), lambda i: (i, 0))

    def gate_cols(branch):
        blk = OFF_GATE // D_MODEL + branch
        return pl.BlockSpec((D_MODEL // 2, D_MODEL), lambda i: (0, blk), pipeline_mode=pl.Buffered(1))

    in_specs = [tok, tok, tok, tok,
                _resident(ng.shape), gate_cols(0), gate_cols(1), gate_cols(2), _resident(bg.shape),
                _resident(wro.shape), _resident(wco.shape), _resident(wmo.shape), _resident(wout.shape),
                _resident(fg.shape)]
    return tok, in_specs


def _tail_with_sample(x2d, aret, aconv, amem, ng, win_pk, bg, wro, wco, wmo, wout, fg,
                      dec, z, cos, sin, state, cstate, mk, mv, rng, cw, cbias, xs2d):
    steps = x2d.shape[0] // TAIL_TILE
    nb, seq, _ = z.shape
    grp = nb // steps
    assert grp * steps == nb and TAIL_TILE % (grp * seq) == 0
    steps_per_tile = TAIL_TILE // (grp * seq)
    tok, in_specs = _tail_specs(TAIL_TILE, ng, bg, wro, wco, wmo, wout, fg)
    sample_tok = pl.BlockSpec((TAIL_TILE, D_MODEL), lambda i: (i // steps_per_tile, 0))

    def per_req(*tail):
        zeros = (0,) * len(tail)
        return pl.BlockSpec((grp,) + tail, lambda i: (i,) + zeros)

    def const(shape):
        zeros = (0,) * len(shape)
        return pl.BlockSpec(shape, lambda i: zeros)

    return pl.pallas_call(
        _tail_with_sample_body,
        grid=(steps,),
        in_specs=in_specs + [pl.BlockSpec(memory_space=pltpu.SMEM),
                             per_req(seq, OFF_GATE), const(cos.shape), const(sin.shape),
                             per_req(HEADS, HEAD_DIM, HEAD_DIM), per_req(CONV_WIDTH - 1, D_MODEL),
                             per_req(MEM_LEN, HEADS, HEAD_DIM), per_req(MEM_LEN, HEADS, HEAD_DIM),
                             const(rng.shape), const(cw.shape), const(cbias.shape), sample_tok],
        out_specs=[tok, per_req(HEADS, HEAD_DIM, HEAD_DIM), per_req(CONV_WIDTH - 1, D_MODEL), sample_tok],
        out_shape=[jax.ShapeDtypeStruct(x2d.shape, F32),
                   jax.ShapeDtypeStruct(state.shape, F32),
                   jax.ShapeDtypeStruct(cstate.shape, F32),
                   jax.ShapeDtypeStruct(xs2d.shape, F32)],
        scratch_shapes=[pltpu.VMEM((grp * (SUBLANES + seq), D_MODEL), F32),
                        pltpu.VMEM((HEADS, seq, seq), F32),
                        pltpu.VMEM((HEADS, seq, ROT_HALF), F32),
                        pltpu.VMEM((HEADS, seq, ROT_HALF), F32),
                        pltpu.VMEM((N_BRANCH, TAIL_TILE, D_MODEL), F32)],
        compiler_params=pltpu.CompilerParams(dimension_semantics=("arbitrary",),
                                             vmem_limit_bytes=VMEM_LIMIT_BYTES),
        name="tail_with_sample",
    )(x2d, aret, aconv, amem, ng, win_pk, win_pk, win_pk, bg, wro, wco, wmo, wout, fg,
      dec, z, cos, sin, state, cstate, mk, mv, rng, cw, cbias, xs2d)


def _rope_tables(pos):
    inv = ROPE_BASE ** (-jnp.arange(ROT_HALF, dtype=F32) / ROT_HALF)
    ang = pos.astype(F32)[:, None] * inv[None, :]
    return jnp.cos(ang), jnp.sin(ang)


def _rope_tables_range(n, stride=64):
    cos_hi, sin_hi = _rope_tables(stride * jnp.arange(n // stride, dtype=jnp.int32))
    cos_lo, sin_lo = _rope_tables(jnp.arange(stride, dtype=jnp.int32))
    cos = cos_hi[:, None, :] * cos_lo[None] - sin_hi[:, None, :] * sin_lo[None]
    sin = sin_hi[:, None, :] * cos_lo[None] + cos_hi[:, None, :] * sin_lo[None]
    return cos.reshape(n, ROT_HALF), sin.reshape(n, ROT_HALF)


def _decay_scalars(chunk):
    lg = jnp.log1p(-jnp.exp2(-5.0 - jnp.arange(HEADS, dtype=F32)))
    return jnp.concatenate([lg, jnp.exp(lg * chunk)])


def kernel(x_prompt, x_sample, state_ret, state_conv, cache_mem_k, cache_mem_v, mem_prompt, norm_g, w_in, b_gate, ret_norm_g, conv_w, conv_b, w_ret_o, w_conv_o, w_mem_o, w_out, mem_norm_g, w_mem_kv, final_norm_g):
    assert norm_g.shape[0] == 1, "single-layer trunk"
    nbp, seq_p, _ = x_prompt.shape
    nbs, seq_s, _ = x_sample.shape
    assert seq_p % PROMPT_TILE == 0 and PROMPT_TILE % RET_CHUNK == 0 and (nbp * seq_p) % TAIL_TILE == 0
    assert seq_s == SUBLANES and (nbs * seq_s) % TAIL_TILE == 0

    ng = norm_g[0][None, :]
    bg = b_gate[0][None, :]
    rng = ret_norm_g[0].reshape(1, D_MODEL)
    cw = conv_w[0]
    cbias = conv_b[0][None, :]
    wro, wco, wmo, wout = _pack_weights(w_ret_o[0], w_conv_o[0], w_mem_o[0], w_out[0])
    fg = final_norm_g[None, :]

    cos_p, sin_p = _rope_tables_range(seq_p)
    cos_s, sin_s = _rope_tables(PAST_LEN + jnp.arange(seq_s, dtype=jnp.int32))

    xs2d = x_sample.reshape(nbs * seq_s, D_MODEL)
    z, win_pk, mk, mv, mk_bf, mv_bf = _prep(xs2d, ng, w_in[0], mem_prompt, mem_norm_g[0][None, :],
                                            w_mem_kv[0])
    z = z.reshape(nbs, seq_s, OFF_GATE)
    tail_weights = (ng, win_pk, bg, wro, wco, wmo, wout, fg)

    aret_p, aconv_p, amem_p, ret_p, conv_p = _prompt_branches(
        x_prompt, _decay_scalars(RET_CHUNK), cos_p, sin_p, mk_bf, mv_bf, ng, win_pk, rng, cw, cbias)

    flat_p = lambda a: a.reshape(nbp * seq_p, D_MODEL)
    y_prompt, ret_s, conv_s, y_sample = _tail_with_sample(
        flat_p(x_prompt), flat_p(aret_p), flat_p(aconv_p), flat_p(amem_p), *tail_weights,
        _decay_scalars(seq_s), z, cos_s, sin_s, state_ret[0], state_conv[0], cache_mem_k[0], cache_mem_v[0],
        rng, cw, cbias, xs2d)

    return (y_prompt.reshape(x_prompt.shape), y_sample.reshape(x_sample.shape),
            ret_p[None], ret_s[None], conv_p[None], conv_s[None], mk[None], mv[None])
```

```python
import functools

import jax
import jax.numpy as jnp
from jax import lax
from jax.experimental import pallas as pl
from jax.experimental.pallas import tpu as pltpu

F32 = jnp.float32
BF16 = jnp.bfloat16

D_MODEL = 1024
HEADS = 4
HEAD_DIM = D_MODEL // HEADS
ROT_HALF = HEAD_DIM // 2
ROPE_BASE = 10000.0
CONV_WIDTH = 3
MEM_LEN = 256
MEM_PIECE = 64
N_BRANCH = 3
EPS = 1e-6
PAST_LEN = 16384
QK_SCALE = HEAD_DIM ** -0.5

OFF_RQ, OFF_RK, OFF_RV, OFF_RG = 0, 1024, 2048, 3072
OFF_CU, OFF_CB, OFF_CC, OFF_CG = 4096, 5120, 6144, 7168
OFF_MQ, OFF_MG = 8192, 9216
OFF_GATE = 10240
IN_TOTAL = OFF_GATE + N_BRANCH * D_MODEL

SUBLANES = 8
VMEM_LIMIT_BYTES = 60 * 1024 * 1024

PROMPT_TILE = 512
RET_CHUNK = 256
TAIL_TILE = 256


def _bf(x):
    return x.astype(BF16)


def _dot(a, b):
    return jnp.dot(a, b, preferred_element_type=F32)


def _dot_nt(a, b):
    return lax.dot_general(a, b, (((1,), (1,)), ((), ())), preferred_element_type=F32)


def _dot_tn(a, b):
    return lax.dot_general(a, b, (((0,), (0,)), ((), ())), preferred_element_type=F32)


def _rms(x, g):
    return x * lax.rsqrt(jnp.mean(x * x, axis=-1, keepdims=True) + EPS) * g


def _silu(x):
    return x * jax.nn.sigmoid(x)


def _rope(x, cos, sin):
    x1 = x[:, :ROT_HALF]
    x2 = x[:, ROT_HALF:]
    return jnp.concatenate([x1 * cos - x2 * sin, x1 * sin + x2 * cos], axis=-1)


def _lanes_twice(t):
    return jnp.concatenate([t, t], axis=-1)


def _pack_rows(w_bf):
    return pltpu.bitcast(w_bf, jnp.int32)


def _unpack_rows(w_packed):
    return pltpu.bitcast(w_packed, BF16)


def _fill_decay(dec_ref, inner_s, qdec_s, kdec_s, chunk):
    i = lax.broadcasted_iota(jnp.int32, (chunk, chunk), 0)
    j = lax.broadcasted_iota(jnp.int32, (chunk, chunk), 1)
    diff = (i - j).astype(F32)
    r = lax.broadcasted_iota(jnp.int32, (chunk, ROT_HALF), 0).astype(F32)
    for hd in range(HEADS):
        lg = dec_ref[hd]
        inner_s[hd] = jnp.where(diff >= 0, jnp.exp(lg * jnp.maximum(diff, 0.0)), 0.0)
        qdec_s[hd] = jnp.exp(lg * (r + 1.0))
        kdec_s[hd] = jnp.exp(lg * (chunk - 1.0 - r))


def _conv_branch(cu, cb, cc, cg, cbuf, cw_ref, cbias_ref, rows, cols):
    pre = cc * cu
    cbuf[SUBLANES:SUBLANES + rows, cols] = pre
    conv = (cbias_ref[:, cols]
            + cw_ref[0:1, cols] * cbuf[SUBLANES - 2:SUBLANES - 2 + rows, cols]
            + cw_ref[1:2, cols] * cbuf[SUBLANES - 1:SUBLANES - 1 + rows, cols]
            + cw_ref[2:3, cols] * pre)
    return pre, cb * conv * _silu(cg)


def _softmax_rows(s):
    e = jnp.exp(s - jnp.max(s, axis=-1, keepdims=True))
    return e * (1.0 / jnp.sum(e, axis=-1, keepdims=True))


def _resident(shape, block=None):
    zeros = (0,) * len(shape)
    return pl.BlockSpec(block or shape, lambda *_: zeros, pipeline_mode=pl.Buffered(1))


def _pack_weights_body(*refs):
    n = len(refs) // 2
    for w_ref, out_ref in zip(refs[:n], refs[n:]):
        out_ref[...] = _pack_rows(_bf(w_ref[...]))


def _pack_weights(*ws):
    return pl.pallas_call(
        _pack_weights_body,
        out_shape=[jax.ShapeDtypeStruct((w.shape[0] // 2, w.shape[1]), jnp.int32) for w in ws],
        compiler_params=pltpu.CompilerParams(vmem_limit_bytes=VMEM_LIMIT_BYTES),
        name="pack_weights",
    )(*ws)


def _prompt_branch_body(dec_ref, x_ref, cos_ref, sin_ref, mk_ref, mv_ref, ng_ref, win_ref, rng_ref,
                        cw_ref, cbias_ref,
                        aret_ref, aconv_ref, amem_ref, state_ref, nconv_ref,
                        cbuf, inner_s, qdec_s, kdec_s):
    b = pl.program_id(0)
    c = pl.program_id(1)
    tile = x_ref.shape[0]

    @pl.when(jnp.logical_and(b == 0, c == 0))
    def _():
        _fill_decay(dec_ref, inner_s, qdec_s, kdec_s, RET_CHUNK)

    @pl.when(c == 0)
    def _():
        state_ref[...] = jnp.zeros(state_ref.shape, F32)
        cbuf[0:SUBLANES, :] = jnp.zeros((SUBLANES, D_MODEL), F32)

    h = _bf(_rms(x_ref[...], ng_ref[...]))
    cos = cos_ref[...]
    sin = sin_ref[...]

    def proj(off):
        return _dot(h, _unpack_rows(win_ref[:, off:off + HEAD_DIM]))

    heads = range(HEADS)
    head_cols = [slice(hd * HEAD_DIM, (hd + 1) * HEAD_DIM) for hd in heads]

    def conv_block(hd):
        lo = hd * HEAD_DIM
        cols = head_cols[hd]
        pre, a_conv = _conv_branch(proj(OFF_CU + lo), proj(OFF_CB + lo), proj(OFF_CC + lo), proj(OFF_CG + lo),
                                   cbuf, cw_ref, cbias_ref, tile, cols)
        cbuf[0:SUBLANES, cols] = pre[tile - SUBLANES:, :]
        nconv_ref[:, cols] = pre[tile - (CONV_WIDTH - 1):, :]
        aconv_ref[:, cols] = _bf(a_conv)

    q = [_bf(_rope(proj(OFF_RQ + hd * HEAD_DIM), cos, sin)) for hd in heads]
    k = [_rope(proj(OFF_RK + hd * HEAD_DIM), cos, sin) * QK_SCALE for hd in heads]
    v = [_bf(proj(OFF_RV + hd * HEAD_DIM)) for hd in heads]
    chunks = [slice(j * RET_CHUNK, (j + 1) * RET_CHUNK) for j in range(tile // RET_CHUNK)]
    scores, cross = [], []
    for rows in chunks:
        for hd in heads:
            state = state_ref[hd]
            kc = k[hd][rows]
            scores.append(_bf(_dot_nt(q[hd][rows], _bf(kc)) * inner_s[hd]))
            cross.append(_dot(q[hd][rows], _bf(state)))
            state_ref[hd] = (state * dec_ref[HEADS + hd]
                             + _dot_tn(_bf(kc * _lanes_twice(kdec_s[hd])), v[hd][rows]))
    conv_block(0)
    ret_o = []
    for hd in heads:
        qdec = _lanes_twice(qdec_s[hd])
        ret_o.append(jnp.concatenate(
            [_dot(scores[j * HEADS + hd], v[hd][rows]) + cross[j * HEADS + hd] * qdec
             for j, rows in enumerate(chunks)], axis=0))
    mq = [_bf(proj(OFF_MQ + hd * HEAD_DIM)) for hd in heads]
    conv_block(1)
    mscores = [_dot_nt(mq[hd], mk_ref[:, head_cols[hd]]) * QK_SCALE for hd in heads]
    for hd in heads:
        cols = head_cols[hd]
        aret_ref[:, cols] = _bf(_rms(ret_o[hd], rng_ref[:, cols]) * _silu(proj(OFF_RG + hd * HEAD_DIM)))
    conv_block(2)
    probs = [_bf(_softmax_rows(mscores[hd])) for hd in heads]
    conv_block(3)
    for hd in heads:
        cols = head_cols[hd]
        om = _dot(probs[hd], mv_ref[:, cols])
        amem_ref[:, cols] = _bf(om * _silu(proj(OFF_MG + hd * HEAD_DIM)))


def _prompt_branches(x, dec, cos, sin, mk_bf, mv_bf, ng, win_pk, rng, cw, cbias):
    nb, seq, _ = x.shape
    tile = PROMPT_TILE
    tok = pl.BlockSpec((None, tile, D_MODEL), lambda b, c: (b, c, 0))
    rot = pl.BlockSpec((tile, ROT_HALF), lambda b, c: (c, 0))
    mem = pl.BlockSpec((None, MEM_LEN, D_MODEL), lambda b, c: (b, 0, 0))
    act = jax.ShapeDtypeStruct(x.shape, BF16)
    return pl.pallas_call(
        _prompt_branch_body,
        grid=(nb, seq // tile),
        in_specs=[pl.BlockSpec(memory_space=pltpu.SMEM), tok, rot, rot, mem, mem,
                  _resident(ng.shape), _resident(win_pk.shape, (D_MODEL // 2, OFF_GATE)), _resident(rng.shape),
                  _resident(cw.shape), _resident(cbias.shape)],
        out_specs=[tok, tok, tok,
                   pl.BlockSpec((None, HEADS, HEAD_DIM, HEAD_DIM), lambda b, c: (b, 0, 0, 0)),
                   pl.BlockSpec((None, CONV_WIDTH - 1, D_MODEL), lambda b, c: (b, 0, 0))],
        out_shape=[act, act, act,
                   jax.ShapeDtypeStruct((nb, HEADS, HEAD_DIM, HEAD_DIM), F32),
                   jax.ShapeDtypeStruct((nb, CONV_WIDTH - 1, D_MODEL), F32)],
        scratch_shapes=[pltpu.VMEM((SUBLANES + tile, D_MODEL), F32),
                        pltpu.VMEM((HEADS, RET_CHUNK, RET_CHUNK), F32),
                        pltpu.VMEM((HEADS, RET_CHUNK, ROT_HALF), F32),
                        pltpu.VMEM((HEADS, RET_CHUNK, ROT_HALF), F32)],
        compiler_params=pltpu.CompilerParams(dimension_semantics=("arbitrary", "arbitrary"),
                                             vmem_limit_bytes=VMEM_LIMIT_BYTES),
        name="prompt_branches",
    )(dec, x, cos, sin, mk_bf, mv_bf, ng, win_pk, rng, cw, cbias)


def _prep_body(x_ref, ng_ref, w_ref, mem_ref, mg_ref, wkv_ref,
               z_ref, wpk_ref, k_ref, v_ref, kb_ref, vb_ref, h_s, wkv_s, *, n_mem):
    n = pl.program_id(0)

    @pl.when(n == 0)
    def _():
        h_s[...] = _bf(_rms(x_ref[...], ng_ref[...]))
        wkv_s[...] = _bf(wkv_ref[...])

    w = _bf(w_ref[...])
    wpk_ref[...] = _pack_rows(w)

    def sample_projection():
        z_ref[...] = _dot(h_s[...], w)

    def memory_kv():
        kv = _dot(_bf(_rms(mem_ref[...], mg_ref[...])), wkv_s[...])
        k = kv[:, :D_MODEL]
        v = kv[:, D_MODEL:]
        for hd in range(HEADS):
            lo = hd * HEAD_DIM
            k_ref[:, hd, :] = k[:, lo:lo + HEAD_DIM]
            v_ref[:, hd, :] = v[:, lo:lo + HEAD_DIM]
        kb_ref[...] = _bf(k)
        vb_ref[...] = _bf(v)

    z_blocks = OFF_GATE // w_ref.shape[1]
    assert n_mem <= z_blocks

    @pl.when(n < n_mem)
    def _():
        sample_projection()
        memory_kv()

    @pl.when(jnp.logical_and(n >= n_mem, n < z_blocks))
    def _():
        sample_projection()


def _prep(x2d, ng, w_in, mem, mem_g, w_mem_kv):
    ntok = x2d.shape[0]
    n_mem = mem.shape[0]
    blk = D_MODEL
    steps = IN_TOTAL // blk
    z_blocks = OFF_GATE // blk
    assert n_mem <= steps

    def mem_blk(*tail):
        zeros = (0,) * len(tail)
        return pl.BlockSpec((None,) + tail, lambda n: (jnp.minimum(n, n_mem - 1),) + zeros)

    heads_shape = jax.ShapeDtypeStruct((n_mem, MEM_LEN, HEADS, HEAD_DIM), F32)
    return pl.pallas_call(
        functools.partial(_prep_body, n_mem=n_mem),
        grid=(steps,),
        in_specs=[_resident(x2d.shape), _resident(ng.shape),
                  pl.BlockSpec((D_MODEL, blk), lambda n: (0, n)),
                  mem_blk(MEM_LEN, D_MODEL), _resident(mem_g.shape), _resident(w_mem_kv.shape)],
        out_specs=[pl.BlockSpec((ntok, blk), lambda n: (0, jnp.minimum(n, z_blocks - 1))),
                   pl.BlockSpec((D_MODEL // 2, blk), lambda n: (0, n)),
                   mem_blk(MEM_LEN, HEADS, HEAD_DIM), mem_blk(MEM_LEN, HEADS, HEAD_DIM),
                   mem_blk(MEM_LEN, D_MODEL), mem_blk(MEM_LEN, D_MODEL)],
        out_shape=[jax.ShapeDtypeStruct((ntok, OFF_GATE), F32),
                   jax.ShapeDtypeStruct((D_MODEL // 2, IN_TOTAL), jnp.int32),
                   heads_shape, heads_shape,
                   jax.ShapeDtypeStruct(mem.shape, BF16), jax.ShapeDtypeStruct(mem.shape, BF16)],
        scratch_shapes=[pltpu.VMEM(x2d.shape, BF16), pltpu.VMEM(w_mem_kv.shape, BF16)],
        compiler_params=pltpu.CompilerParams(dimension_semantics=("arbitrary",),
                                             vmem_limit_bytes=VMEM_LIMIT_BYTES),
        name="prep",
    )(x2d, ng, w_in, mem, mem_g, w_mem_kv)


def _tail_stages(x_ref, acts, ng_ref, wg_refs, bg_ref, wro_ref, wco_ref, wmo_ref, wout_ref, fg_ref, y_ref):
    live = {}
    w_refs = (wro_ref, wco_ref, wmo_ref)

    def gate(branch):
        def run(between=()):
            if branch == 0:
                live["h"] = _bf(_rms(x_ref[...], ng_ref[...]))
            nblk = max(len(between), 1)
            width = D_MODEL // nblk
            parts = []
            for i in range(nblk):
                cols = slice(i * width, (i + 1) * width)
                logits = _dot(live["h"], _unpack_rows(wg_refs[branch][:, cols]))
                parts.append(jax.nn.sigmoid(logits + bg_ref[:, branch * D_MODEL + i * width:
                                                            branch * D_MODEL + (i + 1) * width]))
                if between:
                    between[i]()
            live["gate"] = jnp.concatenate(parts, axis=-1)
        return run

    def project(branch):
        def run(between=()):
            nblk = max(len(between), 1)
            width = D_MODEL // nblk
            parts = []
            for i in range(nblk):
                cols = slice(i * width, (i + 1) * width)
                parts.append(live["gate"][:, cols] * _dot(acts[branch](), _unpack_rows(w_refs[branch][:, cols])))
                if between:
                    between[i]()
            y_branch = jnp.concatenate(parts, axis=-1)
            live["merged"] = y_branch if branch == 0 else live["merged"] + y_branch
        return run

    def out_project():
        live["y"] = x_ref[...] + _dot(_bf(live["merged"]), _unpack_rows(wout_ref[...]))

    def finish():
        y_ref[...] = _rms(live["y"], fg_ref[...])

    stages = []
    for branch in range(N_BRANCH):
        stages += [gate(branch), project(branch)]
    return stages + [out_project, finish]


def _sample_stages(z_ref, cos_ref, sin_ref, state_ref, cstate_ref, mk_ref, mv_ref, rng_ref, cw_ref, cbias_ref,
                   store_act, nstate_ref, nconv_ref, cbuf, inner_s, qdec_s, kdec_s, dec_ref):
    group, seq, _ = z_ref.shape
    pairs = [(g, hd) for g in range(group) for hd in range(HEADS)]
    live = {}

    def col(g, off, width=HEAD_DIM):
        return z_ref[g, :, off:off + width]

    def ret_inputs():
        cos = cos_ref[...]
        sin = sin_ref[...]
        live["q"] = {p: _bf(_rope(col(p[0], OFF_RQ + p[1] * HEAD_DIM), cos, sin)) for p in pairs}
        live["k"] = {p: _rope(col(p[0], OFF_RK + p[1] * HEAD_DIM), cos, sin) * QK_SCALE for p in pairs}
        live["v"] = {p: _bf(col(p[0], OFF_RV + p[1] * HEAD_DIM)) for p in pairs}

    def ret_products():
        live["scores"], live["cross"] = {}, {}
        for p in pairs:
            g, hd = p
            state = state_ref[g, hd]
            k = live["k"][p]
            live["scores"][p] = _bf(_dot_nt(live["q"][p], _bf(k)) * inner_s[hd])
            live["cross"][p] = _dot(live["q"][p], _bf(state))
            nstate_ref[g, hd] = (state * dec_ref[HEADS + hd]
                                 + _dot_tn(_bf(k * _lanes_twice(kdec_s[hd])), live["v"][p]))

    def ret_outputs():
        for p in pairs:
            g, hd = p
            lo = hd * HEAD_DIM
            o = _dot(live["scores"][p], live["v"][p]) + live["cross"][p] * _lanes_twice(qdec_s[hd])
            o = _rms(o, rng_ref[:, lo:lo + HEAD_DIM])
            store_act(0, g, slice(lo, lo + HEAD_DIM), o * _silu(col(g, OFF_RG + lo)))

    def conv():
        for g in range(group):
            base = g * (SUBLANES + seq)
            cbuf[base:base + SUBLANES, :] = jnp.zeros((SUBLANES, D_MODEL), F32)
            cbuf[base + SUBLANES - (CONV_WIDTH - 1):base + SUBLANES, :] = cstate_ref[g]
            pre, a_conv = _conv_branch(col(g, OFF_CU, D_MODEL), col(g, OFF_CB, D_MODEL), col(g, OFF_CC, D_MODEL),
                                       col(g, OFF_CG, D_MODEL), cbuf.at[base:base + SUBLANES + seq], cw_ref,
                                       cbias_ref, seq, slice(0, D_MODEL))
            nconv_ref[g] = pre[seq - (CONV_WIDTH - 1):, :]
            store_act(1, g, slice(0, D_MODEL), a_conv)

    def mem_score_piece(g, j):
        def run():
            if j == 0:
                live.setdefault("ms", {})[g] = []
                live.setdefault("mq", {})[g] = _bf(
                    jnp.concatenate([col(g, OFF_MQ + hd * HEAD_DIM) for hd in range(HEADS)], axis=0))
            keys = mk_ref[g, j * MEM_PIECE:(j + 1) * MEM_PIECE].reshape(MEM_PIECE * HEADS, HEAD_DIM)
            live["ms"][g].append(_dot_nt(live["mq"][g], _bf(keys)) * QK_SCALE)
        return run

    def mem_probs():
        live["mp"] = []
        for g in range(group):
            s = jnp.concatenate(live["ms"][g], axis=-1)
            row_head = lax.broadcasted_iota(jnp.int32, s.shape, 0) // seq
            col_head = lax.broadcasted_iota(jnp.int32, s.shape, 1) % HEADS
            live["mp"].append(_bf(_softmax_rows(jnp.where(row_head == col_head, s, -jnp.inf))))

    def mem_outputs():
        for g in range(group):
            mv2d = _bf(mv_ref[g].reshape(MEM_LEN * HEADS, HEAD_DIM))
            om_all = _dot(live["mp"][g], mv2d)
            for hd in range(HEADS):
                lo = hd * HEAD_DIM
                om = om_all[hd * seq:(hd + 1) * seq, :]
                store_act(2, g, slice(lo, lo + HEAD_DIM), om * _silu(col(g, OFF_MG + lo)))

    pieces = [[mem_score_piece(g, j) for j in range(MEM_LEN // MEM_PIECE)] for g in range(group)]
    assert group == 2
    return [(ret_inputs, ()), (ret_products, ()), (None, pieces[0]), (ret_outputs, pieces[1]),
            (mem_probs, ()), (conv, ()), (mem_outputs, ())]


def _tail_with_sample_body(x_ref, aret_ref, aconv_ref, amem_ref, ng_ref, wg0_ref, wg1_ref, wg2_ref, bg_ref,
                           wro_ref, wco_ref, wmo_ref, wout_ref, fg_ref,
                           dec_ref, z_ref, cos_ref, sin_ref, state_ref, cstate_ref, mk_ref, mv_ref, rng_ref,
                           cw_ref, cbias_ref, xs_ref,
                           y_ref, nstate_ref, nconv_ref, ys_ref,
                           cbuf, inner_s, qdec_s, kdec_s, sact_s):
    i = pl.program_id(0)
    group, seq, _ = z_ref.shape
    steps_per_tile = xs_ref.shape[0] // (group * seq)
    slot = lax.rem(i, steps_per_tile)

    @pl.when(i == 0)
    def _():
        _fill_decay(dec_ref, inner_s, qdec_s, kdec_s, seq)

    def store_act(branch, g, cols, value):
        row = pl.multiple_of((slot * group + g) * seq, seq)
        sact_s[branch, pl.ds(row, seq), cols] = value

    weights = ((wg0_ref, wg1_ref, wg2_ref), bg_ref, wro_ref, wco_ref, wmo_ref, wout_ref, fg_ref)
    prompt_acts = [lambda ref=ref: ref[...] for ref in (aret_ref, aconv_ref, amem_ref)]
    tail = _tail_stages(x_ref, prompt_acts, ng_ref, *weights, y_ref)
    sample = _sample_stages(z_ref, cos_ref, sin_ref, state_ref, cstate_ref, mk_ref, mv_ref, rng_ref, cw_ref,
                            cbias_ref, store_act, nstate_ref, nconv_ref, cbuf, inner_s, qdec_s, kdec_s, dec_ref)
    assert len(sample) == len(tail) - 1
    for j, tail_stage in enumerate(tail):
        before, between = sample[j] if j < len(sample) - 1 else sample[-1] if j == len(tail) - 1 else (None, ())
        if before is not None:
            before()
        if between:
            tail_stage(between=between)
        else:
            tail_stage()

    @pl.when(slot == steps_per_tile - 1)
    def _():
        sample_acts = [lambda branch=branch: _bf(sact_s[branch]) for branch in range(N_BRANCH)]
        for stage in _tail_stages(xs_ref, sample_acts, ng_ref, *weights, ys_ref):
            stage()


def _tail_specs(tile, ng, bg, wro, wco, wmo, wout, fg):
    tok = pl.BlockSpec((tile, D_MODEL), lambda i: (i, 0))

    def gate_cols(branch):
        blk = OFF_GATE // D_MODEL + branch
        return pl.BlockSpec((D_MODEL // 2, D_MODEL), lambda i: (0, blk), pipeline_mode=pl.Buffered(1))

    in_specs = [tok, tok, tok, tok,
                _resident(ng.shape), gate_cols(0), gate_cols(1), gate_cols(2), _resident(bg.shape),
                _resident(wro.shape), _resident(wco.shape), _resident(wmo.shape), _resident(wout.shape),
                _resident(fg.shape)]
    return tok, in_specs


def _tail_with_sample(x2d, aret, aconv, amem, ng, win_pk, bg, wro, wco, wmo, wout, fg,
                      dec, z, cos, sin, state, cstate, mk, mv, rng, cw, cbias, xs2d):
    steps = x2d.shape[0] // TAIL_TILE
    nb, seq, _ = z.shape
    grp = nb // steps
    assert grp * steps == nb and TAIL_TILE % (grp * seq) == 0
    steps_per_tile = TAIL_TILE // (grp * seq)
    tok, in_specs = _tail_specs(TAIL_TILE, ng, bg, wro, wco, wmo, wout, fg)
    sample_tok = pl.BlockSpec((TAIL_TILE, D_MODEL), lambda i: (i // steps_per_tile, 0))

    def per_req(*tail):
        zeros = (0,) * len(tail)
        return pl.BlockSpec((grp,) + tail, lambda i: (i,) + zeros)

    def const(shape):
        zeros = (0,) * len(shape)
        return pl.BlockSpec(shape, lambda i: zeros)

    return pl.pallas_call(
        _tail_with_sample_body,
        grid=(steps,),
        in_specs=in_specs + [pl.BlockSpec(memory_space=pltpu.SMEM),
                             per_req(seq, OFF_GATE), const(cos.shape), const(sin.shape),
                             per_req(HEADS, HEAD_DIM, HEAD_DIM), per_req(CONV_WIDTH - 1, D_MODEL),
                             per_req(MEM_LEN, HEADS, HEAD_DIM), per_req(MEM_LEN, HEADS, HEAD_DIM),
                             const(rng.shape), const(cw.shape), const(cbias.shape), sample_tok],
        out_specs=[tok, per_req(HEADS, HEAD_DIM, HEAD_DIM), per_req(CONV_WIDTH - 1, D_MODEL), sample_tok],
        out_shape=[jax.ShapeDtypeStruct(x2d.shape, F32),
                   jax.ShapeDtypeStruct(state.shape, F32),
                   jax.ShapeDtypeStruct(cstate.shape, F32),
                   jax.ShapeDtypeStruct(xs2d.shape, F32)],
        scratch_shapes=[pltpu.VMEM((grp * (SUBLANES + seq), D_MODEL), F32),
                        pltpu.VMEM((HEADS, seq, seq), F32),
                        pltpu.VMEM((HEADS, seq, ROT_HALF), F32),
                        pltpu.VMEM((HEADS, seq, ROT_HALF), F32),
                        pltpu.VMEM((N_BRANCH, TAIL_TILE, D_MODEL), F32)],
        compiler_params=pltpu.CompilerParams(dimension_semantics=("arbitrary",),
                                             vmem_limit_bytes=VMEM_LIMIT_BYTES),
        name="tail_with_sample",
    )(x2d, aret, aconv, amem, ng, win_pk, win_pk, win_pk, bg, wro, wco, wmo, wout, fg,
      dec, z, cos, sin, state, cstate, mk, mv, rng, cw, cbias, xs2d)


def _rope_tables(pos):
    inv = ROPE_BASE ** (-jnp.arange(ROT_HALF, dtype=F32) / ROT_HALF)
    ang = pos.astype(F32)[:, None] * inv[None, :]
    return jnp.cos(ang), jnp.sin(ang)


def _rope_tables_range(n, stride=64):
    cos_hi, sin_hi = _rope_tables(stride * jnp.arange(n // stride, dtype=jnp.int32))
    cos_lo, sin_lo = _rope_tables(jnp.arange(stride, dtype=jnp.int32))
    cos = cos_hi[:, None, :] * cos_lo[None] - sin_hi[:, None, :] * sin_lo[None]
    sin = sin_hi[:, None, :] * cos_lo[None] + cos_hi[:, None, :] * sin_lo[None]
    return cos.reshape(n, ROT_HALF), sin.reshape(n, ROT_HALF)


def _decay_scalars(chunk):
    lg = jnp.log1p(-jnp.exp2(-5.0 - jnp.arange(HEADS, dtype=F32)))
    return jnp.concatenate([lg, jnp.exp(lg * chunk)])


def kernel(x_prompt, x_sample, state_ret, state_conv, cache_mem_k, cache_mem_v, mem_prompt, norm_g, w_in, b_gate, ret_norm_g, conv_w, conv_b, w_ret_o, w_conv_o, w_mem_o, w_out, mem_norm_g, w_mem_kv, final_norm_g):
    assert norm_g.shape[0] == 1, "single-layer trunk"
    nbp, seq_p, _ = x_prompt.shape
    nbs, seq_s, _ = x_sample.shape
    assert seq_p % PROMPT_TILE == 0 and PROMPT_TILE % RET_CHUNK == 0 and (nbp * seq_p) % TAIL_TILE == 0
    assert seq_s == SUBLANES and (nbs * seq_s) % TAIL_TILE == 0

    ng = norm_g[0][None, :]
    bg = b_gate[0][None, :]
    rng = ret_norm_g[0].reshape(1, D_MODEL)
    cw = conv_w[0]
    cbias = conv_b[0][None, :]
    wro, wco, wmo, wout = _pack_weights(w_ret_o[0], w_conv_o[0], w_mem_o[0], w_out[0])
    fg = final_norm_g[None, :]

    cos_p, sin_p = _rope_tables_range(seq_p)
    cos_s, sin_s = _rope_tables(PAST_LEN + jnp.arange(seq_s, dtype=jnp.int32))

    xs2d = x_sample.reshape(nbs * seq_s, D_MODEL)
    z, win_pk, mk, mv, mk_bf, mv_bf = _prep(xs2d, ng, w_in[0], mem_prompt, mem_norm_g[0][None, :],
                                            w_mem_kv[0])
    z = z.reshape(nbs, seq_s, OFF_GATE)
    tail_weights = (ng, win_pk, bg, wro, wco, wmo, wout, fg)

    aret_p, aconv_p, amem_p, ret_p, conv_p = _prompt_branches(
        x_prompt, _decay_scalars(RET_CHUNK), cos_p, sin_p, mk_bf, mv_bf, ng, win_pk, rng, cw, cbias)

    flat_p = lambda a: a.reshape(nbp * seq_p, D_MODEL)
    y_prompt, ret_s, conv_s, y_sample = _tail_with_sample(
        flat_p(x_prompt), flat_p(aret_p), flat_p(aconv_p), flat_p(amem_p), *tail_weights,
        _decay_scalars(seq_s), z, cos_s, sin_s, state_ret[0], state_conv[0], cache_mem_k[0], cache_mem_v[0],
        rng, cw, cbias, xs2d)

    return (y_prompt.reshape(x_prompt.shape), y_sample.reshape(x_sample.shape),
            ret_p[None], ret_s[None], conv_p[None], conv_s[None], mk[None], mv[None])
```

```python
import functools

import jax
import jax.numpy as jnp
from jax import lax
from jax.experimental import pallas as pl
from jax.experimental.pallas import tpu as pltpu

F32 = jnp.float32
BF16 = jnp.bfloat16

D_MODEL = 1024
HEADS = 4
HEAD_DIM = D_MODEL // HEADS
ROT_HALF = HEAD_DIM // 2
ROPE_BASE = 10000.0
CONV_WIDTH = 3
MEM_LEN = 256
MEM_PIECE = 64
N_BRANCH = 3
EPS = 1e-6
PAST_LEN = 16384
QK_SCALE = HEAD_DIM ** -0.5

OFF_RQ, OFF_RK, OFF_RV, OFF_RG = 0, 1024, 2048, 3072
OFF_CU, OFF_CB, OFF_CC, OFF_CG = 4096, 5120, 6144, 7168
OFF_MQ, OFF_MG = 8192, 9216
OFF_GATE = 10240
IN_TOTAL = OFF_GATE + N_BRANCH * D_MODEL

SUBLANES = 8
VMEM_LIMIT_BYTES = 60 * 1024 * 1024

PROMPT_TILE = 512
RET_CHUNK = 256
TAIL_TILE = 256


def _bf(x):
    return x.astype(BF16)


def _dot(a, b):
    return jnp.dot(a, b, preferred_element_type=F32)


def _dot_nt(a, b):
    return lax.dot_general(a, b, (((1,), (1,)), ((), ())), preferred_element_type=F32)


def _dot_tn(a, b):
    return lax.dot_general(a, b, (((0,), (0,)), ((), ())), preferred_element_type=F32)


def _rms(x, g):
    return x * lax.rsqrt(jnp.mean(x * x, axis=-1, keepdims=True) + EPS) * g


def _silu(x):
    return x * jax.nn.sigmoid(x)


def _rope(x, cos, sin):
    x1 = x[:, :ROT_HALF]
    x2 = x[:, ROT_HALF:]
    return jnp.concatenate([x1 * cos - x2 * sin, x1 * sin + x2 * cos], axis=-1)


def _lanes_twice(t):
    return jnp.concatenate([t, t], axis=-1)


def _pack_rows(w_bf):
    return pltpu.bitcast(w_bf, jnp.int32)


def _unpack_rows(w_packed):
    return pltpu.bitcast(w_packed, BF16)


def _fill_decay(dec_ref, inner_s, qdec_s, kdec_s, chunk):
    i = lax.broadcasted_iota(jnp.int32, (chunk, chunk), 0)
    j = lax.broadcasted_iota(jnp.int32, (chunk, chunk), 1)
    diff = (i - j).astype(F32)
    r = lax.broadcasted_iota(jnp.int32, (chunk, ROT_HALF), 0).astype(F32)
    for hd in range(HEADS):
        lg = dec_ref[hd]
        inner_s[hd] = jnp.where(diff >= 0, jnp.exp(lg * jnp.maximum(diff, 0.0)), 0.0)
        qdec_s[hd] = jnp.exp(lg * (r + 1.0))
        kdec_s[hd] = jnp.exp(lg * (chunk - 1.0 - r))


def _conv_branch(cu, cb, cc, cg, carry, cw_ref, cbias_ref, cols):
    pre = cc * cu
    ext = jnp.concatenate([carry[:, cols], pre], axis=0)
    back1 = pltpu.roll(ext, 1, axis=0)[SUBLANES:]
    back2 = pltpu.roll(ext, 2, axis=0)[SUBLANES:]
    conv = (cbias_ref[:, cols]
            + cw_ref[0:1, cols] * back2
            + cw_ref[1:2, cols] * back1
            + cw_ref[2:3, cols] * pre)
    return pre, cb * conv * _silu(cg)


def _softmax_rows(s):
    e = jnp.exp(s - jnp.max(s, axis=-1, keepdims=True))
    return e * (1.0 / jnp.sum(e, axis=-1, keepdims=True))


def _resident(shape, block=None):
    zeros = (0,) * len(shape)
    return pl.BlockSpec(block or shape, lambda *_: zeros, pipeline_mode=pl.Buffered(1))


def _pack_weights_body(*refs):
    n = len(refs) // 2
    for w_ref, out_ref in zip(refs[:n], refs[n:]):
        out_ref[...] = _pack_rows(_bf(w_ref[...]))


def _pack_weights(*ws):
    return pl.pallas_call(
        _pack_weights_body,
        out_shape=[jax.ShapeDtypeStruct((w.shape[0] // 2, w.shape[1]), jnp.int32) for w in ws],
        compiler_params=pltpu.CompilerParams(vmem_limit_bytes=VMEM_LIMIT_BYTES),
        name="pack_weights",
    )(*ws)


def _prompt_branch_body(dec_ref, x_ref, cos_ref, sin_ref, mk_ref, mv_ref, ng_ref, win_ref, rng_ref,
                        cw_ref, cbias_ref,
                        aret_ref, aconv_ref, amem_ref, state_ref, nconv_ref,
                        cbuf, inner_s, qdec_s, kdec_s):
    b = pl.program_id(0)
    c = pl.program_id(1)
    tile = x_ref.shape[0]

    @pl.when(jnp.logical_and(b == 0, c == 0))
    def _():
        _fill_decay(dec_ref, inner_s, qdec_s, kdec_s, RET_CHUNK)

    @pl.when(c == 0)
    def _():
        state_ref[...] = jnp.zeros(state_ref.shape, F32)
        cbuf[0:SUBLANES, :] = jnp.zeros((SUBLANES, D_MODEL), F32)

    h = _bf(_rms(x_ref[...], ng_ref[...]))
    cos = cos_ref[...]
    sin = sin_ref[...]

    def proj(off):
        return _dot(h, _unpack_rows(win_ref[:, off:off + HEAD_DIM]))

    heads = range(HEADS)
    head_cols = [slice(hd * HEAD_DIM, (hd + 1) * HEAD_DIM) for hd in heads]

    def conv_block(hd):
        lo = hd * HEAD_DIM
        cols = head_cols[hd]
        pre, a_conv = _conv_branch(proj(OFF_CU + lo), proj(OFF_CB + lo), proj(OFF_CC + lo), proj(OFF_CG + lo),
                                   cbuf, cw_ref, cbias_ref, cols)
        cbuf[0:SUBLANES, cols] = pre[tile - SUBLANES:, :]
        nconv_ref[:, cols] = pre[tile - (CONV_WIDTH - 1):, :]
        aconv_ref[:, cols] = _bf(a_conv)

    q = [_bf(_rope(proj(OFF_RQ + hd * HEAD_DIM), cos, sin)) for hd in heads]
    k = [_rope(proj(OFF_RK + hd * HEAD_DIM), cos, sin) * QK_SCALE for hd in heads]
    v = [_bf(proj(OFF_RV + hd * HEAD_DIM)) for hd in heads]
    chunks = [slice(j * RET_CHUNK, (j + 1) * RET_CHUNK) for j in range(tile // RET_CHUNK)]
    scores, cross = [], []
    for rows in chunks:
        for hd in heads:
            state = state_ref[hd]
            kc = k[hd][rows]
            scores.append(_bf(_dot_nt(q[hd][rows], _bf(kc)) * inner_s[hd]))
            cross.append(_dot(q[hd][rows], _bf(state)))
            state_ref[hd] = (state * dec_ref[HEADS + hd]
                             + _dot_tn(_bf(kc * _lanes_twice(kdec_s[hd])), v[hd][rows]))
    conv_block(0)
    ret_o = []
    for hd in heads:
        qdec = _lanes_twice(qdec_s[hd])
        ret_o.append(jnp.concatenate(
            [_dot(scores[j * HEADS + hd], v[hd][rows]) + cross[j * HEADS + hd] * qdec
             for j, rows in enumerate(chunks)], axis=0))
    mq = [_bf(proj(OFF_MQ + hd * HEAD_DIM)) for hd in heads]
    conv_block(1)
    mscores = [_dot_nt(mq[hd], mk_ref[:, head_cols[hd]]) * QK_SCALE for hd in heads]
    for hd in heads:
        cols = head_cols[hd]
        aret_ref[:, cols] = _bf(_rms(ret_o[hd], rng_ref[:, cols]) * _silu(proj(OFF_RG + hd * HEAD_DIM)))
    conv_block(2)
    probs = [_bf(_softmax_rows(mscores[hd])) for hd in heads]
    conv_block(3)
    for hd in heads:
        cols = head_cols[hd]
        om = _dot(probs[hd], mv_ref[:, cols])
        amem_ref[:, cols] = _bf(om * _silu(proj(OFF_MG + hd * HEAD_DIM)))


def _prompt_branches(x, dec, cos, sin, mk_bf, mv_bf, ng, win_pk, rng, cw, cbias):
    nb, seq, _ = x.shape
    tile = PROMPT_TILE
    tok = pl.BlockSpec((None, tile, D_MODEL), lambda b, c: (b, c, 0))
    rot = pl.BlockSpec((tile, ROT_HALF), lambda b, c: (c, 0))
    mem = pl.BlockSpec((None, MEM_LEN, D_MODEL), lambda b, c: (b, 0, 0))
    act = jax.ShapeDtypeStruct(x.shape, BF16)
    return pl.pallas_call(
        _prompt_branch_body,
        grid=(nb, seq // tile),
        in_specs=[pl.BlockSpec(memory_space=pltpu.SMEM), tok, rot, rot, mem, mem,
                  _resident(ng.shape), _resident(win_pk.shape, (D_MODEL // 2, OFF_GATE)), _resident(rng.shape),
                  _resident(cw.shape), _resident(cbias.shape)],
        out_specs=[tok, tok, tok,
                   pl.BlockSpec((None, HEADS, HEAD_DIM, HEAD_DIM), lambda b, c: (b, 0, 0, 0)),
                   pl.BlockSpec((None, CONV_WIDTH - 1, D_MODEL), lambda b, c: (b, 0, 0))],
        out_shape=[act, act, act,
                   jax.ShapeDtypeStruct((nb, HEADS, HEAD_DIM, HEAD_DIM), F32),
                   jax.ShapeDtypeStruct((nb, CONV_WIDTH - 1, D_MODEL), F32)],
        scratch_shapes=[pltpu.VMEM((SUBLANES, D_MODEL), F32),
                        pltpu.VMEM((HEADS, RET_CHUNK, RET_CHUNK), F32),
                        pltpu.VMEM((HEADS, RET_CHUNK, ROT_HALF), F32),
                        pltpu.VMEM((HEADS, RET_CHUNK, ROT_HALF), F32)],
        compiler_params=pltpu.CompilerParams(dimension_semantics=("arbitrary", "arbitrary"),
                                             vmem_limit_bytes=VMEM_LIMIT_BYTES),
        name="prompt_branches",
    )(dec, x, cos, sin, mk_bf, mv_bf, ng, win_pk, rng, cw, cbias)


def _prep_body(x_ref, ng_ref, w_ref, mem_ref, mg_ref, wkv_ref,
               z_ref, wpk_ref, k_ref, v_ref, kb_ref, vb_ref, h_s, wkv_s, *, n_mem):
    n = pl.program_id(0)

    @pl.when(n == 0)
    def _():
        h_s[...] = _bf(_rms(x_ref[...], ng_ref[...]))
        wkv_s[...] = _bf(wkv_ref[...])

    w = _bf(w_ref[...])
    wpk_ref[...] = _pack_rows(w)

    @pl.when(n < OFF_GATE // w_ref.shape[1])
    def _():
        z_ref[...] = _dot(h_s[...], w)

    @pl.when(n < n_mem)
    def _():
        kv = _dot(_bf(_rms(mem_ref[...], mg_ref[...])), wkv_s[...])
        k = kv[:, :D_MODEL]
        v = kv[:, D_MODEL:]
        for hd in range(HEADS):
            lo = hd * HEAD_DIM
            k_ref[:, hd, :] = k[:, lo:lo + HEAD_DIM]
            v_ref[:, hd, :] = v[:, lo:lo + HEAD_DIM]
        kb_ref[...] = _bf(k)
        vb_ref[...] = _bf(v)


def _prep(x2d, ng, w_in, mem, mem_g, w_mem_kv):
    ntok = x2d.shape[0]
    n_mem = mem.shape[0]
    blk = D_MODEL
    steps = IN_TOTAL // blk
    z_blocks = OFF_GATE // blk
    assert n_mem <= steps

    def mem_blk(*tail):
        zeros = (0,) * len(tail)
        return pl.BlockSpec((None,) + tail, lambda n: (jnp.minimum(n, n_mem - 1),) + zeros)

    heads_shape = jax.ShapeDtypeStruct((n_mem, MEM_LEN, HEADS, HEAD_DIM), F32)
    return pl.pallas_call(
        functools.partial(_prep_body, n_mem=n_mem),
        grid=(steps,),
        in_specs=[_resident(x2d.shape), _resident(ng.shape),
                  pl.BlockSpec((D_MODEL, blk), lambda n: (0, n)),
                  mem_blk(MEM_LEN, D_MODEL), _resident(mem_g.shape), _resident(w_mem_kv.shape)],
        out_specs=[pl.BlockSpec((ntok, blk), lambda n: (0, jnp.minimum(n, z_blocks - 1))),
                   pl.BlockSpec((D_MODEL // 2, blk), lambda n: (0, n)),
                   mem_blk(MEM_LEN, HEADS, HEAD_DIM), mem_blk(MEM_LEN, HEADS, HEAD_DIM),
                   mem_blk(MEM_LEN, D_MODEL), mem_blk(MEM_LEN, D_MODEL)],
        out_shape=[jax.ShapeDtypeStruct((ntok, OFF_GATE), F32),
                   jax.ShapeDtypeStruct((D_MODEL // 2, IN_TOTAL), jnp.int32),
                   heads_shape, heads_shape,
                   jax.ShapeDtypeStruct(mem.shape, BF16), jax.ShapeDtypeStruct(mem.shape, BF16)],
        scratch_shapes=[pltpu.VMEM(x2d.shape, BF16), pltpu.VMEM(w_mem_kv.shape, BF16)],
        compiler_params=pltpu.CompilerParams(dimension_semantics=("arbitrary",),
                                             vmem_limit_bytes=VMEM_LIMIT_BYTES),
        name="prep",
    )(x2d, ng, w_in, mem, mem_g, w_mem_kv)


def _tail_stages(x_ref, acts, ng_ref, wg_refs, bg_ref, wro_ref, wco_ref, wmo_ref, wout_ref, fg_ref, y_ref):
    live = {}
    w_refs = (wro_ref, wco_ref, wmo_ref)

    def gate(branch):
        def run(between=()):
            if branch == 0:
                live["h"] = _bf(_rms(x_ref[...], ng_ref[...]))
            nblk = max(len(between), 1)
            width = D_MODEL // nblk
            parts = []
            for i in range(nblk):
                cols = slice(i * width, (i + 1) * width)
                logits = _dot(live["h"], _unpack_rows(wg_refs[branch][:, cols]))
                parts.append(jax.nn.sigmoid(logits + bg_ref[:, branch * D_MODEL + i * width:
                                                            branch * D_MODEL + (i + 1) * width]))
                if between:
                    between[i]()
            live["gate"] = jnp.concatenate(parts, axis=-1)
        return run

    def project(branch):
        def run(between=()):
            nblk = max(len(between), 1)
            width = D_MODEL // nblk
            parts = []
            for i in range(nblk):
                cols = slice(i * width, (i + 1) * width)
                parts.append(live["gate"][:, cols] * _dot(acts[branch](), _unpack_rows(w_refs[branch][:, cols])))
                if between:
                    between[i]()
            y_branch = jnp.concatenate(parts, axis=-1)
            live["merged"] = y_branch if branch == 0 else live["merged"] + y_branch
        return run

    def out_project():
        live["y"] = x_ref[...] + _dot(_bf(live["merged"]), _unpack_rows(wout_ref[...]))

    def finish():
        y_ref[...] = _rms(live["y"], fg_ref[...])

    stages = []
    for branch in range(N_BRANCH):
        stages += [gate(branch), project(branch)]
    return stages + [out_project, finish]


def _sample_stages(z_ref, cos_ref, sin_ref, state_ref, cstate_ref, mk_ref, mv_ref, rng_ref, cw_ref, cbias_ref,
                   store_act, nstate_ref, nconv_ref, cbuf, inner_s, qdec_s, kdec_s, dec_ref):
    group, seq, _ = z_ref.shape
    pairs = [(g, hd) for g in range(group) for hd in range(HEADS)]
    live = {}

    def col(g, off, width=HEAD_DIM):
        return z_ref[g, :, off:off + width]

    def ret_inputs():
        cos = cos_ref[...]
        sin = sin_ref[...]
        live["q"] = {p: _bf(_rope(col(p[0], OFF_RQ + p[1] * HEAD_DIM), cos, sin)) for p in pairs}
        live["k"] = {p: _rope(col(p[0], OFF_RK + p[1] * HEAD_DIM), cos, sin) * QK_SCALE for p in pairs}
        live["v"] = {p: _bf(col(p[0], OFF_RV + p[1] * HEAD_DIM)) for p in pairs}

    def ret_products():
        live["scores"], live["cross"] = {}, {}
        for p in pairs:
            g, hd = p
            state = state_ref[g, hd]
            k = live["k"][p]
            live["scores"][p] = _bf(_dot_nt(live["q"][p], _bf(k)) * inner_s[hd])
            live["cross"][p] = _dot(live["q"][p], _bf(state))
            nstate_ref[g, hd] = (state * dec_ref[HEADS + hd]
                                 + _dot_tn(_bf(k * _lanes_twice(kdec_s[hd])), live["v"][p]))

    def ret_outputs():
        for p in pairs:
            g, hd = p
            lo = hd * HEAD_DIM
            o = _dot(live["scores"][p], live["v"][p]) + live["cross"][p] * _lanes_twice(qdec_s[hd])
            o = _rms(o, rng_ref[:, lo:lo + HEAD_DIM])
            store_act(0, g, slice(lo, lo + HEAD_DIM), o * _silu(col(g, OFF_RG + lo)))

    def conv():
        for g in range(group):
            base = g * SUBLANES
            cbuf[base:base + SUBLANES, :] = jnp.zeros((SUBLANES, D_MODEL), F32)
            cbuf[base + SUBLANES - (CONV_WIDTH - 1):base + SUBLANES, :] = cstate_ref[g]
            pre, a_conv = _conv_branch(col(g, OFF_CU, D_MODEL), col(g, OFF_CB, D_MODEL), col(g, OFF_CC, D_MODEL),
                                       col(g, OFF_CG, D_MODEL), cbuf.at[base:base + SUBLANES], cw_ref,
                                       cbias_ref, slice(0, D_MODEL))
            nconv_ref[g] = pre[seq - (CONV_WIDTH - 1):, :]
            store_act(1, g, slice(0, D_MODEL), a_conv)

    def mem_score_piece(g, j):
        def run():
            if j == 0:
                live.setdefault("ms", {})[g] = []
                live.setdefault("mq", {})[g] = _bf(
                    jnp.concatenate([col(g, OFF_MQ + hd * HEAD_DIM) for hd in range(HEADS)], axis=0))
            keys = mk_ref[g, j * MEM_PIECE:(j + 1) * MEM_PIECE].reshape(MEM_PIECE * HEADS, HEAD_DIM)
            live["ms"][g].append(_dot_nt(live["mq"][g], _bf(keys)) * QK_SCALE)
        return run

    def mem_probs():
        live["mp"] = []
        for g in range(group):
            s = jnp.concatenate(live["ms"][g], axis=-1)
            row_head = lax.broadcasted_iota(jnp.int32, s.shape, 0) // seq
            col_head = lax.broadcasted_iota(jnp.int32, s.shape, 1) % HEADS
            live["mp"].append(_bf(_softmax_rows(jnp.where(row_head == col_head, s, -jnp.inf))))

    def mem_outputs():
        for g in range(group):
            mv2d = _bf(mv_ref[g].reshape(MEM_LEN * HEADS, HEAD_DIM))
            om_all = _dot(live["mp"][g], mv2d)
            for hd in range(HEADS):
                lo = hd * HEAD_DIM
                om = om_all[hd * seq:(hd + 1) * seq, :]
                store_act(2, g, slice(lo, lo + HEAD_DIM), om * _silu(col(g, OFF_MG + lo)))

    pieces = [[mem_score_piece(g, j) for j in range(MEM_LEN // MEM_PIECE)] for g in range(group)]
    assert group == 2
    return [(ret_inputs, ()), (ret_products, ()), (None, pieces[0]), (ret_outputs, pieces[1]),
            (mem_probs, ()), (conv, ()), (mem_outputs, ())]


def _tail_with_sample_body(x_ref, aret_ref, aconv_ref, amem_ref, ng_ref, wg0_ref, wg1_ref, wg2_ref, bg_ref,
                           wro_ref, wco_ref, wmo_ref, wout_ref, fg_ref,
                           dec_ref, z_ref, cos_ref, sin_ref, state_ref, cstate_ref, mk_ref, mv_ref, rng_ref,
                           cw_ref, cbias_ref, xs_ref,
                           y_ref, nstate_ref, nconv_ref, ys_ref,
                           cbuf, inner_s, qdec_s, kdec_s, sact_s):
    i = pl.program_id(0)
    group, seq, _ = z_ref.shape
    steps_per_tile = xs_ref.shape[0] // (group * seq)
    slot = lax.rem(i, steps_per_tile)

    @pl.when(i == 0)
    def _():
        _fill_decay(dec_ref, inner_s, qdec_s, kdec_s, seq)

    def store_act(branch, g, cols, value):
        row = pl.multiple_of((slot * group + g) * seq, seq)
        sact_s[branch, pl.ds(row, seq), cols] = value

    weights = ((wg0_ref, wg1_ref, wg2_ref), bg_ref, wro_ref, wco_ref, wmo_ref, wout_ref, fg_ref)
    prompt_acts = [lambda ref=ref: ref[...] for ref in (aret_ref, aconv_ref, amem_ref)]
    tail = _tail_stages(x_ref, prompt_acts, ng_ref, *weights, y_ref)
    sample = _sample_stages(z_ref, cos_ref, sin_ref, state_ref, cstate_ref, mk_ref, mv_ref, rng_ref, cw_ref,
                            cbias_ref, store_act, nstate_ref, nconv_ref, cbuf, inner_s, qdec_s, kdec_s, dec_ref)
    assert len(sample) == len(tail) - 1
    for j, tail_stage in enumerate(tail):
        before, between = sample[j] if j < len(sample) - 1 else sample[-1] if j == len(tail) - 1 else (None, ())
        if before is not None:
            before()
        if between:
            tail_stage(between=between)
        else:
            tail_stage()

    @pl.when(slot == steps_per_tile - 1)
    def _():
        sample_acts = [lambda branch=branch: _bf(sact_s[branch]) for branch in range(N_BRANCH)]
        for stage in _tail_stages(xs_ref, sample_acts, ng_ref, *weights, ys_ref):
            stage()


def _tail_specs(tile, ng, bg, wro, wco, wmo, wout, fg):
    tok = pl.BlockSpec((tile, D_MODEL), lambda i: (i, 0))

    def gate_cols(branch):
        blk = OFF_GATE // D_MODEL + branch
        return pl.BlockSpec((D_MODEL // 2, D_MODEL), lambda i: (0, blk), pipeline_mode=pl.Buffered(1))

    in_specs = [tok, tok, tok, tok,
                _resident(ng.shape), gate_cols(0), gate_cols(1), gate_cols(2), _resident(bg.shape),
                _resident(wro.shape), _resident(wco.shape), _resident(wmo.shape), _resident(wout.shape),
                _resident(fg.shape)]
    return tok, in_specs


def _tail_with_sample(x2d, aret, aconv, amem, ng, win_pk, bg, wro, wco, wmo, wout, fg,
                      dec, z, cos, sin, state, cstate, mk, mv, rng, cw, cbias, xs2d):
    steps = x2d.shape[0] // TAIL_TILE
    nb, seq, _ = z.shape
    grp = nb // steps
    assert grp * steps == nb and TAIL_TILE % (grp * seq) == 0
    steps_per_tile = TAIL_TILE // (grp * seq)
    tok, in_specs = _tail_specs(TAIL_TILE, ng, bg, wro, wco, wmo, wout, fg)
    sample_tok = pl.BlockSpec((TAIL_TILE, D_MODEL), lambda i: (i // steps_per_tile, 0))

    def per_req(*tail):
        zeros = (0,) * len(tail)
        return pl.BlockSpec((grp,) + tail, lambda i: (i,) + zeros)

    def const(shape):
        zeros = (0,) * len(shape)
        return pl.BlockSpec(shape, lambda i: zeros)

    return pl.pallas_call(
        _tail_with_sample_body,
        grid=(steps,),
        in_specs=in_specs + [pl.BlockSpec(memory_space=pltpu.SMEM),
                             per_req(seq, OFF_GATE), const(cos.shape), const(sin.shape),
                             per_req(HEADS, HEAD_DIM, HEAD_DIM), per_req(CONV_WIDTH - 1, D_MODEL),
                             per_req(MEM_LEN, HEADS, HEAD_DIM), per_req(MEM_LEN, HEADS, HEAD_DIM),
                             const(rng.shape), const(cw.shape), const(cbias.shape), sample_tok],
        out_specs=[tok, per_req(HEADS, HEAD_DIM, HEAD_DIM), per_req(CONV_WIDTH - 1, D_MODEL), sample_tok],
        out_shape=[jax.ShapeDtypeStruct(x2d.shape, F32),
                   jax.ShapeDtypeStruct(state.shape, F32),
                   jax.ShapeDtypeStruct(cstate.shape, F32),
                   jax.ShapeDtypeStruct(xs2d.shape, F32)],
        scratch_shapes=[pltpu.VMEM((grp * SUBLANES, D_MODEL), F32),
                        pltpu.VMEM((HEADS, seq, seq), F32),
                        pltpu.VMEM((HEADS, seq, ROT_HALF), F32),
                        pltpu.VMEM((HEADS, seq, ROT_HALF), F32),
                        pltpu.VMEM((N_BRANCH, TAIL_TILE, D_MODEL), F32)],
        compiler_params=pltpu.CompilerParams(dimension_semantics=("arbitrary",),
                                             vmem_limit_bytes=VMEM_LIMIT_BYTES),
        name="tail_with_sample",
    )(x2d, aret, aconv, amem, ng, win_pk, win_pk, win_pk, bg, wro, wco, wmo, wout, fg,
      dec, z, cos, sin, state, cstate, mk, mv, rng, cw, cbias, xs2d)


def _rope_tables(pos):
    inv = ROPE_BASE ** (-jnp.arange(ROT_HALF, dtype=F32) / ROT_HALF)
    ang = pos.astype(F32)[:, None] * inv[None, :]
    return jnp.cos(ang), jnp.sin(ang)


def _rope_tables_range(n, stride=64):
    cos_hi, sin_hi = _rope_tables(stride * jnp.arange(n // stride, dtype=jnp.int32))
    cos_lo, sin_lo = _rope_tables(jnp.arange(stride, dtype=jnp.int32))
    cos = cos_hi[:, None, :] * cos_lo[None] - sin_hi[:, None, :] * sin_lo[None]
    sin = sin_hi[:, None, :] * cos_lo[None] + cos_hi[:, None, :] * sin_lo[None]
    return cos.reshape(n, ROT_HALF), sin.reshape(n, ROT_HALF)


def _decay_scalars(chunk):
    lg = jnp.log1p(-jnp.exp2(-5.0 - jnp.arange(HEADS, dtype=F32)))
    return jnp.concatenate([lg, jnp.exp(lg * chunk)])


def kernel(x_prompt, x_sample, state_ret, state_conv, cache_mem_k, cache_mem_v, mem_prompt, norm_g, w_in, b_gate, ret_norm_g, conv_w, conv_b, w_ret_o, w_conv_o, w_mem_o, w_out, mem_norm_g, w_mem_kv, final_norm_g):
    assert norm_g.shape[0] == 1, "single-layer trunk"
    nbp, seq_p, _ = x_prompt.shape
    nbs, seq_s, _ = x_sample.shape
    assert seq_p % PROMPT_TILE == 0 and PROMPT_TILE % RET_CHUNK == 0 and (nbp * seq_p) % TAIL_TILE == 0
    assert seq_s == SUBLANES and (nbs * seq_s) % TAIL_TILE == 0

    ng = norm_g[0][None, :]
    bg = b_gate[0][None, :]
    rng = ret_norm_g[0].reshape(1, D_MODEL)
    cw = conv_w[0]
    cbias = conv_b[0][None, :]
    wro, wco, wmo, wout = _pack_weights(w_ret_o[0], w_conv_o[0], w_mem_o[0], w_out[0])
    fg = final_norm_g[None, :]

    cos_p, sin_p = _rope_tables_range(seq_p)
    cos_s, sin_s = _rope_tables(PAST_LEN + jnp.arange(seq_s, dtype=jnp.int32))

    xs2d = x_sample.reshape(nbs * seq_s, D_MODEL)
    z, win_pk, mk, mv, mk_bf, mv_bf = _prep(xs2d, ng, w_in[0], mem_prompt, mem_norm_g[0][None, :],
                                            w_mem_kv[0])
    z = z.reshape(nbs, seq_s, OFF_GATE)
    tail_weights = (ng, win_pk, bg, wro, wco, wmo, wout, fg)

    aret_p, aconv_p, amem_p, ret_p, conv_p = _prompt_branches(
        x_prompt, _decay_scalars(RET_CHUNK), cos_p, sin_p, mk_bf, mv_bf, ng, win_pk, rng, cw, cbias)

    flat_p = lambda a: a.reshape(nbp * seq_p, D_MODEL)
    y_prompt, ret_s, conv_s, y_sample = _tail_with_sample(
        flat_p(x_prompt), flat_p(aret_p), flat_p(aconv_p), flat_p(amem_p), *tail_weights,
        _decay_scalars(seq_s), z, cos_s, sin_s, state_ret[0], state_conv[0], cache_mem_k[0], cache_mem_v[0],
        rng, cw, cbias, xs2d)

    return (y_prompt.reshape(x_prompt.shape), y_sample.reshape(x_sample.shape),
            ret_p[None], ret_s[None], conv_p[None], conv_s[None], mk[None], mv[None])
```

```python
import functools

import jax
import jax.numpy as jnp
from jax import lax
from jax.experimental import pallas as pl
from jax.experimental.pallas import tpu as pltpu

F32 = jnp.float32
BF16 = jnp.bfloat16

D_MODEL = 1024
HEADS = 4
HEAD_DIM = D_MODEL // HEADS
ROT_HALF = HEAD_DIM // 2
ROPE_BASE = 10000.0
CONV_WIDTH = 3
MEM_LEN = 256
MEM_PIECE = 64
N_BRANCH = 3
EPS = 1e-6
PAST_LEN = 16384
QK_SCALE = HEAD_DIM ** -0.5

OFF_RQ, OFF_RK, OFF_RV, OFF_RG = 0, 1024, 2048, 3072
OFF_CU, OFF_CB, OFF_CC, OFF_CG = 4096, 5120, 6144, 7168
OFF_MQ, OFF_MG = 8192, 9216
OFF_GATE = 10240
IN_TOTAL = OFF_GATE + N_BRANCH * D_MODEL

SUBLANES = 8
VMEM_LIMIT_BYTES = 60 * 1024 * 1024

PROMPT_TILE = 512
RET_CHUNK = 256
TAIL_TILE = 256


def _bf(x):
    return x.astype(BF16)


def _dot(a, b):
    return jnp.dot(a, b, preferred_element_type=F32)


def _dot_nt(a, b):
    return lax.dot_general(a, b, (((1,), (1,)), ((), ())), preferred_element_type=F32)


def _dot_tn(a, b):
    return lax.dot_general(a, b, (((0,), (0,)), ((), ())), preferred_element_type=F32)


def _rms(x, g):
    return x * lax.rsqrt(jnp.mean(x * x, axis=-1, keepdims=True) + EPS) * g


def _silu(x):
    return x * jax.nn.sigmoid(x)


def _rope(x, cos, sin):
    x1 = x[:, :ROT_HALF]
    x2 = x[:, ROT_HALF:]
    return jnp.concatenate([x1 * cos - x2 * sin, x1 * sin + x2 * cos], axis=-1)


def _lanes_twice(t):
    return jnp.concatenate([t, t], axis=-1)


def _pack_rows(w_bf):
    return pltpu.bitcast(w_bf, jnp.int32)


def _unpack_rows(w_packed):
    return pltpu.bitcast(w_packed, BF16)


def _fill_decay(dec_ref, inner_s, qdec_s, kdec_s, chunk):
    i = lax.broadcasted_iota(jnp.int32, (chunk, chunk), 0)
    j = lax.broadcasted_iota(jnp.int32, (chunk, chunk), 1)
    diff = (i - j).astype(F32)
    r = lax.broadcasted_iota(jnp.int32, (chunk, ROT_HALF), 0).astype(F32)
    for hd in range(HEADS):
        lg = dec_ref[hd]
        inner_s[hd] = jnp.where(diff >= 0, jnp.exp(lg * jnp.maximum(diff, 0.0)), 0.0)
        qdec_s[hd] = jnp.exp(lg * (r + 1.0))
        kdec_s[hd] = jnp.exp(lg * (chunk - 1.0 - r))


def _conv_branch(cu, cb, cc, cg, cbuf, cw_ref, cbias_ref, rows, cols):
    pre = cc * cu
    cbuf[SUBLANES:SUBLANES + rows, cols] = pre
    conv = (cbias_ref[:, cols]
            + cw_ref[0:1, cols] * cbuf[SUBLANES - 2:SUBLANES - 2 + rows, cols]
            + cw_ref[1:2, cols] * cbuf[SUBLANES - 1:SUBLANES - 1 + rows, cols]
            + cw_ref[2:3, cols] * pre)
    return pre, cb * conv * _silu(cg)


def _softmax_rows(s):
    e = jnp.exp(s - jnp.max(s, axis=-1, keepdims=True))
    return e * (1.0 / jnp.sum(e, axis=-1, keepdims=True))


def _resident(shape, block=None):
    zeros = (0,) * len(shape)
    return pl.BlockSpec(block or shape, lambda *_: zeros, pipeline_mode=pl.Buffered(1))


def _pack_weights_body(*refs):
    n = len(refs) // 2
    for w_ref, out_ref in zip(refs[:n], refs[n:]):
        out_ref[...] = _pack_rows(_bf(w_ref[...]))


def _pack_weights(*ws):
    return pl.pallas_call(
        _pack_weights_body,
        out_shape=[jax.ShapeDtypeStruct((w.shape[0] // 2, w.shape[1]), jnp.int32) for w in ws],
        compiler_params=pltpu.CompilerParams(vmem_limit_bytes=VMEM_LIMIT_BYTES),
        name="pack_weights",
    )(*ws)


def _prompt_branch_body(dec_ref, x_ref, cos_ref, sin_ref, mk_ref, mv_ref, ng_ref, win_ref, rng_ref,
                        cw_ref, cbias_ref,
                        aret_ref, aconv_ref, amem_ref, state_ref, nconv_ref,
                        cbuf, inner_s, qdec_s, kdec_s):
    b = pl.program_id(0)
    c = pl.program_id(1)
    tile = x_ref.shape[0]

    @pl.when(jnp.logical_and(b == 0, c == 0))
    def _():
        _fill_decay(dec_ref, inner_s, qdec_s, kdec_s, RET_CHUNK)

    @pl.when(c == 0)
    def _():
        state_ref[...] = jnp.zeros(state_ref.shape, F32)
        cbuf[0:SUBLANES, :] = jnp.zeros((SUBLANES, D_MODEL), F32)

    h = _bf(_rms(x_ref[...], ng_ref[...]))
    cos = cos_ref[...]
    sin = sin_ref[...]

    def proj(off):
        return _dot(h, _unpack_rows(win_ref[:, off:off + HEAD_DIM]))

    heads = range(HEADS)
    head_cols = [slice(hd * HEAD_DIM, (hd + 1) * HEAD_DIM) for hd in heads]

    def conv_block(hd):
        lo = hd * HEAD_DIM
        cols = head_cols[hd]
        pre, a_conv = _conv_branch(proj(OFF_CU + lo), proj(OFF_CB + lo), proj(OFF_CC + lo), proj(OFF_CG + lo),
                                   cbuf, cw_ref, cbias_ref, tile, cols)
        cbuf[0:SUBLANES, cols] = pre[tile - SUBLANES:, :]
        nconv_ref[:, cols] = pre[tile - (CONV_WIDTH - 1):, :]
        aconv_ref[:, cols] = _bf(a_conv)

    q = [_bf(_rope(proj(OFF_RQ + hd * HEAD_DIM), cos, sin)) for hd in heads]
    k = [_rope(proj(OFF_RK + hd * HEAD_DIM), cos, sin) * QK_SCALE for hd in heads]
    v = [_bf(proj(OFF_RV + hd * HEAD_DIM)) for hd in heads]
    chunks = [slice(j * RET_CHUNK, (j + 1) * RET_CHUNK) for j in range(tile // RET_CHUNK)]
    scores, cross = [], []
    for rows in chunks:
        for hd in heads:
            state = state_ref[hd]
            kc = k[hd][rows]
            scores.append(_bf(_dot_nt(q[hd][rows], _bf(kc)) * inner_s[hd]))
            cross.append(_dot(q[hd][rows], _bf(state)))
            state_ref[hd] = (state * dec_ref[HEADS + hd]
                             + _dot_tn(_bf(kc * _lanes_twice(kdec_s[hd])), v[hd][rows]))
    conv_block(0)
    ret_o = []
    for hd in heads:
        qdec = _lanes_twice(qdec_s[hd])
        ret_o.append(jnp.concatenate(
            [_dot(scores[j * HEADS + hd], v[hd][rows]) + cross[j * HEADS + hd] * qdec
             for j, rows in enumerate(chunks)], axis=0))
    mq = [_bf(proj(OFF_MQ + hd * HEAD_DIM)) for hd in heads]
    conv_block(1)
    mscores = [_dot_nt(mq[hd], mk_ref[:, head_cols[hd]]) * QK_SCALE for hd in heads]
    for hd in heads:
        cols = head_cols[hd]
        aret_ref[:, cols] = _bf(_rms(ret_o[hd], rng_ref[:, cols]) * _silu(proj(OFF_RG + hd * HEAD_DIM)))
    conv_block(2)
    probs = [_bf(_softmax_rows(mscores[hd])) for hd in heads]
    conv_block(3)
    for hd in heads:
        cols = head_cols[hd]
        om = _dot(probs[hd], mv_ref[:, cols])
        amem_ref[:, cols] = _bf(om * _silu(proj(OFF_MG + hd * HEAD_DIM)))


def _prompt_branches(x, dec, cos, sin, mk_bf, mv_bf, ng, win_pk, rng, cw, cbias):
    nb, seq, _ = x.shape
    tile = PROMPT_TILE
    tok = pl.BlockSpec((None, tile, D_MODEL), lambda b, c: (b, c, 0))
    rot = pl.BlockSpec((tile, ROT_HALF), lambda b, c: (c, 0))
    mem = pl.BlockSpec((None, MEM_LEN, D_MODEL), lambda b, c: (b, 0, 0))
    act = jax.ShapeDtypeStruct(x.shape, BF16)
    return pl.pallas_call(
        _prompt_branch_body,
        grid=(nb, seq // tile),
        in_specs=[pl.BlockSpec(memory_space=pltpu.SMEM), tok, rot, rot, mem, mem,
                  _resident(ng.shape), _resident(win_pk.shape, (D_MODEL // 2, OFF_GATE)), _resident(rng.shape),
                  _resident(cw.shape), _resident(cbias.shape)],
        out_specs=[tok, tok, tok,
                   pl.BlockSpec((None, HEADS, HEAD_DIM, HEAD_DIM), lambda b, c: (b, 0, 0, 0)),
                   pl.BlockSpec((None, CONV_WIDTH - 1, D_MODEL), lambda b, c: (b, 0, 0))],
        out_shape=[act, act, act,
                   jax.ShapeDtypeStruct((nb, HEADS, HEAD_DIM, HEAD_DIM), F32),
                   jax.ShapeDtypeStruct((nb, CONV_WIDTH - 1, D_MODEL), F32)],
        scratch_shapes=[pltpu.VMEM((SUBLANES + tile, D_MODEL), F32),
                        pltpu.VMEM((HEADS, RET_CHUNK, RET_CHUNK), F32),
                        pltpu.VMEM((HEADS, RET_CHUNK, ROT_HALF), F32),
                        pltpu.VMEM((HEADS, RET_CHUNK, ROT_HALF), F32)],
        compiler_params=pltpu.CompilerParams(dimension_semantics=("arbitrary", "arbitrary"),
                                             vmem_limit_bytes=VMEM_LIMIT_BYTES),
        name="prompt_branches",
    )(dec, x, cos, sin, mk_bf, mv_bf, ng, win_pk, rng, cw, cbias)


def _prep_body(x_ref, ng_ref, w_ref, mem_ref, mg_ref, wkv_ref,
               z_ref, wpk_ref, k_ref, v_ref, kb_ref, vb_ref, h_s, wkv_s, *, n_mem):
    n = pl.program_id(0)

    @pl.when(n == 0)
    def _():
        h_s[...] = _bf(_rms(x_ref[...], ng_ref[...]))
        wkv_s[...] = _bf(wkv_ref[...])

    w = _bf(w_ref[...])
    wpk_ref[...] = _pack_rows(w)

    @pl.when(n < OFF_GATE // w_ref.shape[1])
    def _():
        z_ref[...] = _dot(h_s[...], w)

    @pl.when(n < n_mem)
    def _():
        kv = _dot(_bf(_rms(mem_ref[...], mg_ref[...])), wkv_s[...])
        k = kv[:, :D_MODEL]
        v = kv[:, D_MODEL:]
        for hd in range(HEADS):
            lo = hd * HEAD_DIM
            k_ref[:, hd, :] = k[:, lo:lo + HEAD_DIM]
            v_ref[:, hd, :] = v[:, lo:lo + HEAD_DIM]
        kb_ref[...] = _bf(k)
        vb_ref[...] = _bf(v)


def _prep(x2d, ng, w_in, mem, mem_g, w_mem_kv):
    ntok = x2d.shape[0]
    n_mem = mem.shape[0]
    blk = D_MODEL
    steps = IN_TOTAL // blk
    z_blocks = OFF_GATE // blk
    assert n_mem <= steps

    def mem_blk(*tail):
        zeros = (0,) * len(tail)
        return pl.BlockSpec((None,) + tail, lambda n: (jnp.minimum(n, n_mem - 1),) + zeros)

    heads_shape = jax.ShapeDtypeStruct((n_mem, MEM_LEN, HEADS, HEAD_DIM), F32)
    return pl.pallas_call(
        functools.partial(_prep_body, n_mem=n_mem),
        grid=(steps,),
        in_specs=[_resident(x2d.shape), _resident(ng.shape),
                  pl.BlockSpec((D_MODEL, blk), lambda n: (0, n)),
                  mem_blk(MEM_LEN, D_MODEL), _resident(mem_g.shape), _resident(w_mem_kv.shape)],
        out_specs=[pl.BlockSpec((ntok, blk), lambda n: (0, jnp.minimum(n, z_blocks - 1))),
                   pl.BlockSpec((D_MODEL // 2, blk), lambda n: (0, n)),
                   mem_blk(MEM_LEN, HEADS, HEAD_DIM), mem_blk(MEM_LEN, HEADS, HEAD_DIM),
                   mem_blk(MEM_LEN, D_MODEL), mem_blk(MEM_LEN, D_MODEL)],
        out_shape=[jax.ShapeDtypeStruct((ntok, OFF_GATE), F32),
                   jax.ShapeDtypeStruct((D_MODEL // 2, IN_TOTAL), jnp.int32),
                   heads_shape, heads_shape,
                   jax.ShapeDtypeStruct(mem.shape, BF16), jax.ShapeDtypeStruct(mem.shape, BF16)],
        scratch_shapes=[pltpu.VMEM(x2d.shape, BF16), pltpu.VMEM(w_mem_kv.shape, BF16)],
        compiler_params=pltpu.CompilerParams(dimension_semantics=("arbitrary",),
                                             vmem_limit_bytes=VMEM_LIMIT_BYTES),
        name="prep",
    )(x2d, ng, w_in, mem, mem_g, w_mem_kv)


def _tail_stages(x_ref, acts, ng_ref, wg_refs, bg_ref, wro_ref, wco_ref, wmo_ref, wout_ref, fg_ref, y_ref):
    live = {}
    w_refs = (wro_ref, wco_ref, wmo_ref)

    def gate(branch):
        def run(between=()):
            if branch == 0:
                live["h"] = _bf(_rms(x_ref[...], ng_ref[...]))
            nblk = max(len(between), 1)
            width = D_MODEL // nblk
            parts = []
            for i in range(nblk):
                cols = slice(i * width, (i + 1) * width)
                logits = _dot(live["h"], _unpack_rows(wg_refs[branch][:, cols]))
                parts.append(jax.nn.sigmoid(logits + bg_ref[:, branch * D_MODEL + i * width:
                                                            branch * D_MODEL + (i + 1) * width]))
                if between:
                    between[i]()
            live["gate"] = jnp.concatenate(parts, axis=-1)
        return run

    def project(branch):
        def run(between=()):
            nblk = max(len(between), 1)
            width = D_MODEL // nblk
            parts = []
            for i in range(nblk):
                cols = slice(i * width, (i + 1) * width)
                parts.append(live["gate"][:, cols] * _dot(acts[branch](), _unpack_rows(w_refs[branch][:, cols])))
                if between:
                    between[i]()
            y_branch = jnp.concatenate(parts, axis=-1)
            live["merged"] = y_branch if branch == 0 else live["merged"] + y_branch
        return run

    def out_project():
        live["y"] = x_ref[...] + _dot(_bf(live["merged"]), _unpack_rows(wout_ref[...]))

    def finish():
        y_ref[...] = _rms(live["y"], fg_ref[...])

    stages = []
    for branch in range(N_BRANCH):
        stages += [gate(branch), project(branch)]
    return stages + [out_project, finish]


def _sample_stages(z_ref, cos_ref, sin_ref, state_ref, cstate_ref, mk_ref, mv_ref, rng_ref, cw_ref, cbias_ref,
                   store_act, nstate_ref, nconv_ref, cbuf, inner_s, qdec_s, kdec_s, dec_ref):
    group, seq, _ = z_ref.shape
    pairs = [(g, hd) for g in range(group) for hd in range(HEADS)]
    live = {}

    def col(g, off, width=HEAD_DIM):
        return z_ref[g, :, off:off + width]

    def ret_inputs():
        cos = cos_ref[...]
        sin = sin_ref[...]
        live["q"] = {p: _bf(_rope(col(p[0], OFF_RQ + p[1] * HEAD_DIM), cos, sin)) for p in pairs}
        live["k"] = {p: _rope(col(p[0], OFF_RK + p[1] * HEAD_DIM), cos, sin) * QK_SCALE for p in pairs}
        live["v"] = {p: _bf(col(p[0], OFF_RV + p[1] * HEAD_DIM)) for p in pairs}

    def ret_products():
        live["scores"], live["cross"] = {}, {}
        for p in pairs:
            g, hd = p
            state = state_ref[g, hd]
            k = live["k"][p]
            live["scores"][p] = _bf(_dot_nt(live["q"][p], _bf(k)) * inner_s[hd])
            live["cross"][p] = _dot(live["q"][p], _bf(state))
            nstate_ref[g, hd] = (state * dec_ref[HEADS + hd]
                                 + _dot_tn(_bf(k * _lanes_twice(kdec_s[hd])), live["v"][p]))

    def ret_outputs():
        for p in pairs:
            g, hd = p
            lo = hd * HEAD_DIM
            o = _dot(live["scores"][p], live["v"][p]) + live["cross"][p] * _lanes_twice(qdec_s[hd])
            o = _rms(o, rng_ref[:, lo:lo + HEAD_DIM])
            store_act(0, g, slice(lo, lo + HEAD_DIM), o * _silu(col(g, OFF_RG + lo)))

    def conv():
        for g in range(group):
            base = g * (SUBLANES + seq)
            cbuf[base:base + SUBLANES, :] = jnp.zeros((SUBLANES, D_MODEL), F32)
            cbuf[base + SUBLANES - (CONV_WIDTH - 1):base + SUBLANES, :] = cstate_ref[g]
            pre, a_conv = _conv_branch(col(g, OFF_CU, D_MODEL), col(g, OFF_CB, D_MODEL), col(g, OFF_CC, D_MODEL),
                                       col(g, OFF_CG, D_MODEL), cbuf.at[base:base + SUBLANES + seq], cw_ref,
                                       cbias_ref, seq, slice(0, D_MODEL))
            nconv_ref[g] = pre[seq - (CONV_WIDTH - 1):, :]
            store_act(1, g, slice(0, D_MODEL), a_conv)

    def mem_score_piece(g, j):
        def run():
            if j == 0:
                live.setdefault("ms", {})[g] = []
                live.setdefault("mq", {})[g] = _bf(
                    jnp.concatenate([col(g, OFF_MQ + hd * HEAD_DIM) for hd in range(HEADS)], axis=0))
            keys = mk_ref[g, j * MEM_PIECE:(j + 1) * MEM_PIECE].reshape(MEM_PIECE * HEADS, HEAD_DIM)
            live["ms"][g].append(_dot_nt(live["mq"][g], _bf(keys)) * QK_SCALE)
        return run

    def mem_probs():
        live["mp"] = []
        for g in range(group):
            s = jnp.concatenate(live["ms"][g], axis=-1)
            row_head = lax.broadcasted_iota(jnp.int32, s.shape, 0) // seq
            col_head = lax.broadcasted_iota(jnp.int32, s.shape, 1) % HEADS
            live["mp"].append(_bf(_softmax_rows(jnp.where(row_head == col_head, s, -jnp.inf))))

    def mem_outputs():
        for g in range(group):
            mv2d = _bf(mv_ref[g].reshape(MEM_LEN * HEADS, HEAD_DIM))
            om_all = _dot(live["mp"][g], mv2d)
            for hd in range(HEADS):
                lo = hd * HEAD_DIM
                om = om_all[hd * seq:(hd + 1) * seq, :]
                store_act(2, g, slice(lo, lo + HEAD_DIM), om * _silu(col(g, OFF_MG + lo)))

    pieces = [[mem_score_piece(g, j) for j in range(MEM_LEN // MEM_PIECE)] for g in range(group)]
    assert group == 2
    return [(ret_inputs, ()), (ret_products, ()), (None, pieces[0]), (ret_outputs, pieces[1]),
            (mem_probs, ()), (conv, ()), (mem_outputs, ())]


def _tail_with_sample_body(x_ref, aret_ref, aconv_ref, amem_ref, ng_ref, wg0_ref, wg1_ref, wg2_ref, bg_ref,
                           wro_ref, wco_ref, wmo_ref, wout_ref, fg_ref,
                           dec_ref, z_ref, cos_ref, sin_ref, state_ref, cstate_ref, mk_ref, mv_ref, rng_ref,
                           cw_ref, cbias_ref, xs_ref,
                           y_ref, nstate_ref, nconv_ref, ys_ref,
                           cbuf, inner_s, qdec_s, kdec_s, sact_s):
    i = pl.program_id(0)
    group, seq, _ = z_ref.shape
    steps_per_tile = xs_ref.shape[0] // (group * seq)
    slot = lax.rem(i, steps_per_tile)

    @pl.when(i == 0)
    def _():
        _fill_decay(dec_ref, inner_s, qdec_s, kdec_s, seq)

    def store_act(branch, g, cols, value):
        row = pl.multiple_of((slot * group + g) * seq, seq)
        sact_s[branch, pl.ds(row, seq), cols] = value

    weights = ((wg0_ref, wg1_ref, wg2_ref), bg_ref, wro_ref, wco_ref, wmo_ref, wout_ref, fg_ref)
    prompt_acts = [lambda ref=ref: ref[...] for ref in (aret_ref, aconv_ref, amem_ref)]
    tail = _tail_stages(x_ref, prompt_acts, ng_ref, *weights, y_ref)
    sample = _sample_stages(z_ref, cos_ref, sin_ref, state_ref, cstate_ref, mk_ref, mv_ref, rng_ref, cw_ref,
                            cbias_ref, store_act, nstate_ref, nconv_ref, cbuf, inner_s, qdec_s, kdec_s, dec_ref)
    assert len(sample) == len(tail) - 1
    for j, tail_stage in enumerate(tail):
        before, between = sample[j] if j < len(sample) - 1 else sample[-1] if j == len(tail) - 1 else (None, ())
        if before is not None:
            before()
        if between:
            tail_stage(between=between)
        else:
            tail_stage()

    @pl.when(slot == steps_per_tile - 1)
    def _():
        sample_acts = [lambda branch=branch: _bf(sact_s[branch]) for branch in range(N_BRANCH)]
        for stage in _tail_stages(xs_ref, sample_acts, ng_ref, *weights, ys_ref):
            stage()


def _tail_specs(tile, ng, bg, wro, wco, wmo, wout, fg):
    tok = pl.BlockSpec((tile, D_MODEL), lambda i: (i, 0))

    def gate_cols(branch):
        blk = OFF_GATE // D_MODEL + branch
        return pl.BlockSpec((D_MODEL // 2, D_MODEL), lambda i: (0, blk), pipeline_mode=pl.Buffered(1))

    in_specs = [tok, tok, tok, tok,
                _resident(ng.shape), gate_cols(0), gate_cols(1), gate_cols(2), _resident(bg.shape),
                _resident(wro.shape), _resident(wco.shape), _resident(wmo.shape), _resident(wout.shape),
                _resident(fg.shape)]
    return tok, in_specs


def _tail_with_sample(x2d, aret, aconv, amem, ng, win_pk, bg, wro, wco, wmo, wout, fg,
                      dec, z, cos, sin, state, cstate, mk, mv, rng, cw, cbias, xs2d):
    steps = x2d.shape[0] // TAIL_TILE
    nb, seq, _ = z.shape
    grp = nb // steps
    assert grp * steps == nb and TAIL_TILE % (grp * seq) == 0
    steps_per_tile = TAIL_TILE // (grp * seq)
    tok, in_specs = _tail_specs(TAIL_TILE, ng, bg, wro, wco, wmo, wout, fg)
    sample_tok = pl.BlockSpec((TAIL_TILE, D_MODEL), lambda i: (i // steps_per_tile, 0))

    def per_req(*tail):
        zeros = (0,) * len(tail)
        return pl.BlockSpec((grp,) + tail, lambda i: (i,) + zeros)

    def const(shape):
        zeros = (0,) * len(shape)
        return pl.BlockSpec(shape, lambda i: zeros)

    return pl.pallas_call(
        _tail_with_sample_body,
        grid=(steps,),
        in_specs=in_specs + [pl.BlockSpec(memory_space=pltpu.SMEM),
                             per_req(seq, OFF_GATE), const(cos.shape), const(sin.shape),
                             per_req(HEADS, HEAD_DIM, HEAD_DIM), per_req(CONV_WIDTH - 1, D_MODEL),
                             per_req(MEM_LEN, HEADS, HEAD_DIM), per_req(MEM_LEN, HEADS, HEAD_DIM),
                             const(rng.shape), const(cw.shape), const(cbias.shape), sample_tok],
        out_specs=[tok, per_req(HEADS, HEAD_DIM, HEAD_DIM), per_req(CONV_WIDTH - 1, D_MODEL), sample_tok],
        out_shape=[jax.ShapeDtypeStruct(x2d.shape, F32),
                   jax.ShapeDtypeStruct(state.shape, F32),
                   jax.ShapeDtypeStruct(cstate.shape, F32),
                   jax.ShapeDtypeStruct(xs2d.shape, F32)],
        scratch_shapes=[pltpu.VMEM((grp * (SUBLANES + seq), D_MODEL), F32),
                        pltpu.VMEM((HEADS, seq, seq), F32),
                        pltpu.VMEM((HEADS, seq, ROT_HALF), F32),
                        pltpu.VMEM((HEADS, seq, ROT_HALF), F32),
                        pltpu.VMEM((N_BRANCH, TAIL_TILE, D_MODEL), F32)],
        compiler_params=pltpu.CompilerParams(dimension_semantics=("arbitrary",),
                                             vmem_limit_bytes=VMEM_LIMIT_BYTES),
        name="tail_with_sample",
    )(x2d, aret, aconv, amem, ng, win_pk, win_pk, win_pk, bg, wro, wco, wmo, wout, fg,
      dec, z, cos, sin, state, cstate, mk, mv, rng, cw, cbias, xs2d)


def _rope_tables(pos):
    inv = ROPE_BASE ** (-jnp.arange(ROT_HALF, dtype=F32) / ROT_HALF)
    ang = pos.astype(F32)[:, None] * inv[None, :]
    return jnp.cos(ang), jnp.sin(ang)


def _rope_tables_range(n, stride=64):
    assert n % stride == 0
    cos_hi, sin_hi = _rope_tables(stride * jnp.arange(n // stride, dtype=jnp.int32))
    cos_lo, sin_lo = _rope_tables(jnp.arange(stride, dtype=jnp.int32))
    cos = cos_hi[:, None, :] * cos_lo[None] - sin_hi[:, None, :] * sin_lo[None]
    sin = sin_hi[:, None, :] * cos_lo[None] + cos_hi[:, None, :] * sin_lo[None]
    return cos.reshape(n, ROT_HALF), sin.reshape(n, ROT_HALF)


def _decay_scalars(chunk):
    lg = jnp.log1p(-jnp.exp2(-5.0 - jnp.arange(HEADS, dtype=F32)))
    return jnp.concatenate([lg, jnp.exp(lg * chunk)])


def kernel(x_prompt, x_sample, state_ret, state_conv, cache_mem_k, cache_mem_v, mem_prompt, norm_g, w_in, b_gate, ret_norm_g, conv_w, conv_b, w_ret_o, w_conv_o, w_mem_o, w_out, mem_norm_g, w_mem_kv, final_norm_g):
    assert norm_g.shape[0] == 1, "single-layer trunk"
    nbp, seq_p, _ = x_prompt.shape
    nbs, seq_s, _ = x_sample.shape
    assert seq_p % PROMPT_TILE == 0 and PROMPT_TILE % RET_CHUNK == 0 and (nbp * seq_p) % TAIL_TILE == 0
    assert seq_s == SUBLANES and (nbs * seq_s) % TAIL_TILE == 0

    ng = norm_g[0][None, :]
    bg = b_gate[0][None, :]
    rng = ret_norm_g[0].reshape(1, D_MODEL)
    cw = conv_w[0]
    cbias = conv_b[0][None, :]
    wro, wco, wmo, wout = _pack_weights(w_ret_o[0], w_conv_o[0], w_mem_o[0], w_out[0])
    fg = final_norm_g[None, :]

    cos_p, sin_p = _rope_tables_range(seq_p)
    cos_s, sin_s = _rope_tables(PAST_LEN + jnp.arange(seq_s, dtype=jnp.int32))

    xs2d = x_sample.reshape(nbs * seq_s, D_MODEL)
    z, win_pk, mk, mv, mk_bf, mv_bf = _prep(xs2d, ng, w_in[0], mem_prompt, mem_norm_g[0][None, :],
                                            w_mem_kv[0])
    z = z.reshape(nbs, seq_s, OFF_GATE)
    tail_weights = (ng, win_pk, bg, wro, wco, wmo, wout, fg)

    aret_p, aconv_p, amem_p, ret_p, conv_p = _prompt_branches(
        x_prompt, _decay_scalars(RET_CHUNK), cos_p, sin_p, mk_bf, mv_bf, ng, win_pk, rng, cw, cbias)

    flat_p = lambda a: a.reshape(nbp * seq_p, D_MODEL)
    y_prompt, ret_s, conv_s, y_sample = _tail_with_sample(
        flat_p(x_prompt), flat_p(aret_p), flat_p(aconv_p), flat_p(amem_p), *tail_weights,
        _decay_scalars(seq_s), z, cos_s, sin_s, state_ret[0], state_conv[0], cache_mem_k[0], cache_mem_v[0],
        rng, cw, cbias, xs2d)

    return (y_prompt.reshape(x_prompt.shape), y_sample.reshape(x_sample.shape),
            ret_p[None], ret_s[None], conv_p[None], conv_s[None], mk[None], mv[None])
```

```python
import functools

import jax
import jax.numpy as jnp
from jax import lax
from jax.experimental import pallas as pl
from jax.experimental.pallas import tpu as pltpu

F32 = jnp.float32
BF16 = jnp.bfloat16

D_MODEL = 1024
HEADS = 4
HEAD_DIM = D_MODEL // HEADS
ROT_HALF = HEAD_DIM // 2
ROPE_BASE = 10000.0
CONV_WIDTH = 3
MEM_LEN = 256
MEM_PIECE = 64
N_BRANCH = 3
EPS = 1e-6
PAST_LEN = 16384
QK_SCALE = HEAD_DIM ** -0.5

OFF_RQ, OFF_RK, OFF_RV, OFF_RG = 0, 1024, 2048, 3072
OFF_CU, OFF_CB, OFF_CC, OFF_CG = 4096, 5120, 6144, 7168
OFF_MQ, OFF_MG = 8192, 9216
OFF_GATE = 10240
IN_TOTAL = OFF_GATE + N_BRANCH * D_MODEL

SUBLANES = 8
VMEM_LIMIT_BYTES = 60 * 1024 * 1024

PROMPT_TILE = 512
RET_CHUNK = 256
TAIL_TILE = 512
SAMPLE_TILE = 256


def _bf(x):
    return x.astype(BF16)


def _dot(a, b):
    return jnp.dot(a, b, preferred_element_type=F32)


def _dot_nt(a, b):
    return lax.dot_general(a, b, (((1,), (1,)), ((), ())), preferred_element_type=F32)


def _dot_tn(a, b):
    return lax.dot_general(a, b, (((0,), (0,)), ((), ())), preferred_element_type=F32)


def _rms(x, g):
    return x * lax.rsqrt(jnp.mean(x * x, axis=-1, keepdims=True) + EPS) * g


def _silu(x):
    return x * jax.nn.sigmoid(x)


def _rope(x, cos, sin):
    x1 = x[:, :ROT_HALF]
    x2 = x[:, ROT_HALF:]
    return jnp.concatenate([x1 * cos - x2 * sin, x1 * sin + x2 * cos], axis=-1)


def _lanes_twice(t):
    return jnp.concatenate([t, t], axis=-1)


def _pack_rows(w_bf):
    return pltpu.bitcast(w_bf, jnp.int32)


def _unpack_rows(w_packed):
    return pltpu.bitcast(w_packed, BF16)


def _fill_decay(dec_ref, inner_s, qdec_s, kdec_s, chunk):
    i = lax.broadcasted_iota(jnp.int32, (chunk, chunk), 0)
    j = lax.broadcasted_iota(jnp.int32, (chunk, chunk), 1)
    diff = (i - j).astype(F32)
    r = lax.broadcasted_iota(jnp.int32, (chunk, ROT_HALF), 0).astype(F32)
    for hd in range(HEADS):
        lg = dec_ref[hd]
        inner_s[hd] = jnp.where(diff >= 0, jnp.exp(lg * jnp.maximum(diff, 0.0)), 0.0)
        qdec_s[hd] = jnp.exp(lg * (r + 1.0))
        kdec_s[hd] = jnp.exp(lg * (chunk - 1.0 - r))


def _conv_branch(cu, cb, cc, cg, cbuf, cw_ref, cbias_ref, rows, cols):
    pre = cc * cu
    cbuf[SUBLANES:SUBLANES + rows, cols] = pre
    conv = (cbias_ref[:, cols]
            + cw_ref[0:1, cols] * cbuf[SUBLANES - 2:SUBLANES - 2 + rows, cols]
            + cw_ref[1:2, cols] * cbuf[SUBLANES - 1:SUBLANES - 1 + rows, cols]
            + cw_ref[2:3, cols] * pre)
    return pre, cb * conv * _silu(cg)


def _softmax_rows(s):
    e = jnp.exp(s - jnp.max(s, axis=-1, keepdims=True))
    return e * (1.0 / jnp.sum(e, axis=-1, keepdims=True))


def _resident(shape, block=None):
    zeros = (0,) * len(shape)
    return pl.BlockSpec(block or shape, lambda *_: zeros, pipeline_mode=pl.Buffered(1))


def _pack_weights_body(*refs):
    n = len(refs) // 2
    for w_ref, out_ref in zip(refs[:n], refs[n:]):
        out_ref[...] = _pack_rows(_bf(w_ref[...]))


def _pack_weights(*ws):
    return pl.pallas_call(
        _pack_weights_body,
        out_shape=[jax.ShapeDtypeStruct((w.shape[0] // 2, w.shape[1]), jnp.int32) for w in ws],
        compiler_params=pltpu.CompilerParams(vmem_limit_bytes=VMEM_LIMIT_BYTES),
        name="pack_weights",
    )(*ws)


def _prompt_branch_body(dec_ref, x_ref, cos_ref, sin_ref, mk_ref, mv_ref, ng_ref, win_ref, rng_ref,
                        cw_ref, cbias_ref,
                        aret_ref, aconv_ref, amem_ref, state_ref, nconv_ref,
                        cbuf, inner_s, qdec_s, kdec_s):
    b = pl.program_id(0)
    c = pl.program_id(1)
    tile = x_ref.shape[0]

    @pl.when(jnp.logical_and(b == 0, c == 0))
    def _():
        _fill_decay(dec_ref, inner_s, qdec_s, kdec_s, RET_CHUNK)

    @pl.when(c == 0)
    def _():
        state_ref[...] = jnp.zeros(state_ref.shape, F32)
        cbuf[0:SUBLANES, :] = jnp.zeros((SUBLANES, D_MODEL), F32)

    h = _bf(_rms(x_ref[...], ng_ref[...]))
    cos = cos_ref[...]
    sin = sin_ref[...]

    def proj(off):
        return _dot(h, _unpack_rows(win_ref[:, off:off + HEAD_DIM]))

    heads = range(HEADS)
    head_cols = [slice(hd * HEAD_DIM, (hd + 1) * HEAD_DIM) for hd in heads]

    def conv_block(hd):
        lo = hd * HEAD_DIM
        cols = head_cols[hd]
        pre, a_conv = _conv_branch(proj(OFF_CU + lo), proj(OFF_CB + lo), proj(OFF_CC + lo), proj(OFF_CG + lo),
                                   cbuf, cw_ref, cbias_ref, tile, cols)
        cbuf[0:SUBLANES, cols] = pre[tile - SUBLANES:, :]
        nconv_ref[:, cols] = pre[tile - (CONV_WIDTH - 1):, :]
        aconv_ref[:, cols] = _bf(a_conv)

    q = [_bf(_rope(proj(OFF_RQ + hd * HEAD_DIM), cos, sin)) for hd in heads]
    k = [_rope(proj(OFF_RK + hd * HEAD_DIM), cos, sin) * QK_SCALE for hd in heads]
    v = [_bf(proj(OFF_RV + hd * HEAD_DIM)) for hd in heads]
    chunks = [slice(j * RET_CHUNK, (j + 1) * RET_CHUNK) for j in range(tile // RET_CHUNK)]
    scores, cross = [], []
    for rows in chunks:
        for hd in heads:
            state = state_ref[hd]
            kc = k[hd][rows]
            scores.append(_bf(_dot_nt(q[hd][rows], _bf(kc)) * inner_s[hd]))
            cross.append(_dot(q[hd][rows], _bf(state)))
            state_ref[hd] = (state * dec_ref[HEADS + hd]
                             + _dot_tn(_bf(kc * _lanes_twice(kdec_s[hd])), v[hd][rows]))
    conv_block(0)
    ret_o = []
    for hd in heads:
        qdec = _lanes_twice(qdec_s[hd])
        ret_o.append(jnp.concatenate(
            [_dot(scores[j * HEADS + hd], v[hd][rows]) + cross[j * HEADS + hd] * qdec
             for j, rows in enumerate(chunks)], axis=0))
    mq = [_bf(proj(OFF_MQ + hd * HEAD_DIM)) for hd in heads]
    conv_block(1)
    mscores = [_dot_nt(mq[hd], mk_ref[:, head_cols[hd]]) * QK_SCALE for hd in heads]
    for hd in heads:
        cols = head_cols[hd]
        aret_ref[:, cols] = _bf(_rms(ret_o[hd], rng_ref[:, cols]) * _silu(proj(OFF_RG + hd * HEAD_DIM)))
    conv_block(2)
    probs = [_bf(_softmax_rows(mscores[hd])) for hd in heads]
    conv_block(3)
    for hd in heads:
        cols = head_cols[hd]
        om = _dot(probs[hd], mv_ref[:, cols])
        amem_ref[:, cols] = _bf(om * _silu(proj(OFF_MG + hd * HEAD_DIM)))


def _prompt_branches(x, dec, cos, sin, mk_bf, mv_bf, ng, win_pk, rng, cw, cbias):
    nb, seq, _ = x.shape
    tile = PROMPT_TILE
    tok = pl.BlockSpec((None, tile, D_MODEL), lambda b, c: (b, c, 0))
    rot = pl.BlockSpec((tile, ROT_HALF), lambda b, c: (c, 0))
    mem = pl.BlockSpec((None, MEM_LEN, D_MODEL), lambda b, c: (b, 0, 0))
    act = jax.ShapeDtypeStruct(x.shape, BF16)
    return pl.pallas_call(
        _prompt_branch_body,
        grid=(nb, seq // tile),
        in_specs=[pl.BlockSpec(memory_space=pltpu.SMEM), tok, rot, rot, mem, mem,
                  _resident(ng.shape), _resident(win_pk.shape, (D_MODEL // 2, OFF_GATE)), _resident(rng.shape),
                  _resident(cw.shape), _resident(cbias.shape)],
        out_specs=[tok, tok, tok,
                   pl.BlockSpec((None, HEADS, HEAD_DIM, HEAD_DIM), lambda b, c: (b, 0, 0, 0)),
                   pl.BlockSpec((None, CONV_WIDTH - 1, D_MODEL), lambda b, c: (b, 0, 0))],
        out_shape=[act, act, act,
                   jax.ShapeDtypeStruct((nb, HEADS, HEAD_DIM, HEAD_DIM), F32),
                   jax.ShapeDtypeStruct((nb, CONV_WIDTH - 1, D_MODEL), F32)],
        scratch_shapes=[pltpu.VMEM((SUBLANES + tile, D_MODEL), F32),
                        pltpu.VMEM((HEADS, RET_CHUNK, RET_CHUNK), F32),
                        pltpu.VMEM((HEADS, RET_CHUNK, ROT_HALF), F32),
                        pltpu.VMEM((HEADS, RET_CHUNK, ROT_HALF), F32)],
        compiler_params=pltpu.CompilerParams(dimension_semantics=("arbitrary", "arbitrary"),
                                             vmem_limit_bytes=VMEM_LIMIT_BYTES),
        name="prompt_branches",
    )(dec, x, cos, sin, mk_bf, mv_bf, ng, win_pk, rng, cw, cbias)


def _prep_body(x_ref, ng_ref, w_ref, mem_ref, mg_ref, wkv_ref,
               z_ref, wpk_ref, k_ref, v_ref, kb_ref, vb_ref, h_s, wkv_s, *, n_mem):
    n = pl.program_id(0)

    @pl.when(n == 0)
    def _():
        h_s[...] = _bf(_rms(x_ref[...], ng_ref[...]))
        wkv_s[...] = _bf(wkv_ref[...])

    w = _bf(w_ref[...])
    wpk_ref[...] = _pack_rows(w)

    @pl.when(n < OFF_GATE // w_ref.shape[1])
    def _():
        z_ref[...] = _dot(h_s[...], w)

    @pl.when(n < n_mem)
    def _():
        kv = _dot(_bf(_rms(mem_ref[...], mg_ref[...])), wkv_s[...])
        k = kv[:, :D_MODEL]
        v = kv[:, D_MODEL:]
        for hd in range(HEADS):
            lo = hd * HEAD_DIM
            k_ref[:, hd, :] = k[:, lo:lo + HEAD_DIM]
            v_ref[:, hd, :] = v[:, lo:lo + HEAD_DIM]
        kb_ref[...] = _bf(k)
        vb_ref[...] = _bf(v)


def _prep(x2d, ng, w_in, mem, mem_g, w_mem_kv):
    ntok = x2d.shape[0]
    n_mem = mem.shape[0]
    blk = D_MODEL
    steps = IN_TOTAL // blk
    z_blocks = OFF_GATE // blk
    assert n_mem <= steps

    def mem_blk(*tail):
        zeros = (0,) * len(tail)
        return pl.BlockSpec((None,) + tail, lambda n: (jnp.minimum(n, n_mem - 1),) + zeros)

    heads_shape = jax.ShapeDtypeStruct((n_mem, MEM_LEN, HEADS, HEAD_DIM), F32)
    return pl.pallas_call(
        functools.partial(_prep_body, n_mem=n_mem),
        grid=(steps,),
        in_specs=[_resident(x2d.shape), _resident(ng.shape),
                  pl.BlockSpec((D_MODEL, blk), lambda n: (0, n)),
                  mem_blk(MEM_LEN, D_MODEL), _resident(mem_g.shape), _resident(w_mem_kv.shape)],
        out_specs=[pl.BlockSpec((ntok, blk), lambda n: (0, jnp.minimum(n, z_blocks - 1))),
                   pl.BlockSpec((D_MODEL // 2, blk), lambda n: (0, n)),
                   mem_blk(MEM_LEN, HEADS, HEAD_DIM), mem_blk(MEM_LEN, HEADS, HEAD_DIM),
                   mem_blk(MEM_LEN, D_MODEL), mem_blk(MEM_LEN, D_MODEL)],
        out_shape=[jax.ShapeDtypeStruct((ntok, OFF_GATE), F32),
                   jax.ShapeDtypeStruct((D_MODEL // 2, IN_TOTAL), jnp.int32),
                   heads_shape, heads_shape,
                   jax.ShapeDtypeStruct(mem.shape, BF16), jax.ShapeDtypeStruct(mem.shape, BF16)],
        scratch_shapes=[pltpu.VMEM(x2d.shape, BF16), pltpu.VMEM(w_mem_kv.shape, BF16)],
        compiler_params=pltpu.CompilerParams(dimension_semantics=("arbitrary",),
                                             vmem_limit_bytes=VMEM_LIMIT_BYTES),
        name="prep",
    )(x2d, ng, w_in, mem, mem_g, w_mem_kv)


def _tail_stages(x_ref, acts, ng_ref, wg_refs, bg_ref, wro_ref, wco_ref, wmo_ref, wout_ref, fg_ref, y_ref):
    live = {}
    w_refs = (wro_ref, wco_ref, wmo_ref)

    def gate(branch):
        def run(between=()):
            if branch == 0:
                live["h"] = _bf(_rms(x_ref[...], ng_ref[...]))
            nblk = max(len(between), 1)
            width = D_MODEL // nblk
            parts = []
            for i in range(nblk):
                cols = slice(i * width, (i + 1) * width)
                logits = _dot(live["h"], _unpack_rows(wg_refs[branch][:, cols]))
                parts.append(jax.nn.sigmoid(logits + bg_ref[:, branch * D_MODEL + i * width:
                                                            branch * D_MODEL + (i + 1) * width]))
                if between:
                    between[i]()
            live["gate"] = jnp.concatenate(parts, axis=-1)
        return run

    def project(branch):
        def run(between=()):
            nblk = max(len(between), 1)
            width = D_MODEL // nblk
            parts = []
            for i in range(nblk):
                cols = slice(i * width, (i + 1) * width)
                parts.append(live["gate"][:, cols] * _dot(acts[branch](), _unpack_rows(w_refs[branch][:, cols])))
                if between:
                    between[i]()
            y_branch = jnp.concatenate(parts, axis=-1)
            live["merged"] = y_branch if branch == 0 else live["merged"] + y_branch
        return run

    def out_project():
        live["y"] = x_ref[...] + _dot(_bf(live["merged"]), _unpack_rows(wout_ref[...]))

    def finish():
        y_ref[...] = _rms(live["y"], fg_ref[...])

    stages = []
    for branch in range(N_BRANCH):
        stages += [gate(branch), project(branch)]
    return stages + [out_project, finish]


def _sample_stages(z_ref, cos_ref, sin_ref, state_ref, cstate_ref, mk_ref, mv_ref, rng_ref, cw_ref, cbias_ref,
                   store_act, nstate_ref, nconv_ref, cbuf, inner_s, qdec_s, kdec_s, dec_ref, hooks):
    group, seq, _ = z_ref.shape
    pairs = [(g, hd) for g in range(group) for hd in range(HEADS)]
    live = {}

    def col(g, off, width=HEAD_DIM):
        return z_ref[g, :, off:off + width]

    def ret_inputs():
        cos = cos_ref[...]
        sin = sin_ref[...]
        live["q"] = {p: _bf(_rope(col(p[0], OFF_RQ + p[1] * HEAD_DIM), cos, sin)) for p in pairs}
        live["k"] = {p: _rope(col(p[0], OFF_RK + p[1] * HEAD_DIM), cos, sin) * QK_SCALE for p in pairs}
        live["v"] = {p: _bf(col(p[0], OFF_RV + p[1] * HEAD_DIM)) for p in pairs}

    def ret_products():
        live["scores"], live["cross"] = {}, {}
        for p in pairs:
            g, hd = p
            state = state_ref[g, hd]
            k = live["k"][p]
            live["scores"][p] = _bf(_dot_nt(live["q"][p], _bf(k)) * inner_s[hd])
            live["cross"][p] = _dot(live["q"][p], _bf(state))
            nstate_ref[g, hd] = (state * dec_ref[HEADS + hd]
                                 + _dot_tn(_bf(k * _lanes_twice(kdec_s[hd])), live["v"][p]))

    def ret_outputs():
        for p in pairs:
            g, hd = p
            lo = hd * HEAD_DIM
            o = _dot(live["scores"][p], live["v"][p]) + live["cross"][p] * _lanes_twice(qdec_s[hd])
            o = _rms(o, rng_ref[:, lo:lo + HEAD_DIM])
            store_act(0, g, slice(lo, lo + HEAD_DIM), o * _silu(col(g, OFF_RG + lo)))

    def conv():
        for g in range(group):
            base = g * (SUBLANES + seq)
            cbuf[base:base + SUBLANES, :] = jnp.zeros((SUBLANES, D_MODEL), F32)
            cbuf[base + SUBLANES - (CONV_WIDTH - 1):base + SUBLANES, :] = cstate_ref[g]
            pre, a_conv = _conv_branch(col(g, OFF_CU, D_MODEL), col(g, OFF_CB, D_MODEL), col(g, OFF_CC, D_MODEL),
                                       col(g, OFF_CG, D_MODEL), cbuf.at[base:base + SUBLANES + seq], cw_ref,
                                       cbias_ref, seq, slice(0, D_MODEL))
            nconv_ref[g] = pre[seq - (CONV_WIDTH - 1):, :]
            store_act(1, g, slice(0, D_MODEL), a_conv)

    def mem_score_piece(g, j):
        def run():
            if j == 0:
                live.setdefault("ms", {})[g] = []
                live.setdefault("mq", {})[g] = _bf(
                    jnp.concatenate([col(g, OFF_MQ + hd * HEAD_DIM) for hd in range(HEADS)], axis=0))
            keys = mk_ref[g, j * MEM_PIECE:(j + 1) * MEM_PIECE].reshape(MEM_PIECE * HEADS, HEAD_DIM)
            live["ms"][g].append(_dot_nt(live["mq"][g], _bf(keys)) * QK_SCALE)
        return run

    def mem_probs():
        live["mp"] = []
        for g in range(group):
            s = jnp.concatenate(live["ms"][g], axis=-1)
            row_head = lax.broadcasted_iota(jnp.int32, s.shape, 0) // seq
            col_head = lax.broadcasted_iota(jnp.int32, s.shape, 1) % HEADS
            live["mp"].append(_bf(_softmax_rows(jnp.where(row_head == col_head, s, -jnp.inf))))

    def mem_outputs():
        for g in range(group):
            mv2d = _bf(mv_ref[g].reshape(MEM_LEN * HEADS, HEAD_DIM))
            om_all = _dot(live["mp"][g], mv2d)
            for hd in range(HEADS):
                lo = hd * HEAD_DIM
                om = om_all[hd * seq:(hd + 1) * seq, :]
                store_act(2, g, slice(lo, lo + HEAD_DIM), om * _silu(col(g, OFF_MG + lo)))

    def run_all(*stages):
        def run():
            for stage in stages:
                stage()
        return run

    assert group % 2 == 0
    pieces = range(MEM_LEN // MEM_PIECE)
    first = [run_all(*[mem_score_piece(g, j) for g in range(group // 2)]) for j in pieces]
    second = [run_all(*[mem_score_piece(g, j) for g in range(group // 2, group)]) for j in pieces]
    second[-1] = run_all(second[-1], hooks["keys_done"])
    return [(ret_inputs, ()),
            (run_all(ret_products, hooks["state_done"]), ()),
            (None, first),
            (ret_outputs, second),
            (mem_probs, ()),
            (conv, ()),
            (run_all(hooks["values_ready"], hooks["state_swap"], mem_outputs, hooks["values_done"]), ())]


def _tail_with_sample_body(x_ref, aret_ref, aconv_ref, amem_ref, ng_ref, wg0_ref, wg1_ref, wg2_ref, bg_ref,
                           wro_ref, wco_ref, wmo_ref, wout_ref, fg_ref,
                           dec_ref, z_ref, cos_ref, sin_ref, state_hbm, cstate_ref, mk_hbm, mv_hbm, rng_ref,
                           cw_ref, cbias_ref, xs_ref,
                           y_ref, nstate_hbm, nconv_ref, ys_ref,
                           cbuf, inner_s, qdec_s, kdec_s, sact_s, state_buf, key_buf, value_buf, sems):
    i = pl.program_id(0)
    last = pl.num_programs(0) - 1
    nxt = jnp.minimum(i + 1, last)
    group, seq, _ = z_ref.shape

    def requests(step):
        return pl.ds(step * group, group)

    def state_in(step):
        return pltpu.make_async_copy(state_hbm.at[requests(step)], state_buf, sems.at[0])

    def state_out(step):
        return pltpu.make_async_copy(state_buf, nstate_hbm.at[requests(step)], sems.at[1])

    def keys_in(step):
        return pltpu.make_async_copy(mk_hbm.at[requests(step)], key_buf, sems.at[2])

    def values_in(step):
        return pltpu.make_async_copy(mv_hbm.at[requests(step)], value_buf, sems.at[3])

    @pl.when(i == 0)
    def _():
        state_in(0).start()
        keys_in(0).start()
        values_in(0).start()

    state_in(i).wait()
    keys_in(i).wait()

    def state_swap():
        state_out(i).wait()
        state_in(nxt).start()

    hooks = {"state_done": lambda: state_out(i).start(),
             "state_swap": state_swap,
             "keys_done": lambda: keys_in(nxt).start(),
             "values_ready": lambda: values_in(i).wait(),
             "values_done": lambda: values_in(nxt).start()}
    steps_per_tile = xs_ref.shape[0] // (group * seq)
    slot = lax.rem(i, steps_per_tile)

    @pl.when(i == 0)
    def _():
        _fill_decay(dec_ref, inner_s, qdec_s, kdec_s, seq)

    def store_act(branch, g, cols, value):
        row = pl.multiple_of((slot * group + g) * seq, seq)
        sact_s[branch, pl.ds(row, seq), cols] = value

    weights = ((wg0_ref, wg1_ref, wg2_ref), bg_ref, wro_ref, wco_ref, wmo_ref, wout_ref, fg_ref)
    prompt_acts = [lambda ref=ref: ref[...] for ref in (aret_ref, aconv_ref, amem_ref)]
    tail = _tail_stages(x_ref, prompt_acts, ng_ref, *weights, y_ref)
    sample = _sample_stages(z_ref, cos_ref, sin_ref, state_buf, cstate_ref, key_buf, value_buf, rng_ref, cw_ref,
                            cbias_ref, store_act, state_buf, nconv_ref, cbuf, inner_s, qdec_s, kdec_s, dec_ref,
                            hooks)
    assert len(sample) == len(tail) - 1
    for j, tail_stage in enumerate(tail):
        before, between = sample[j] if j < len(sample) - 1 else sample[-1] if j == len(tail) - 1 else (None, ())
        if before is not None:
            before()
        if between:
            tail_stage(between=between)
        else:
            tail_stage()

    @pl.when(slot == steps_per_tile - 1)
    def _():
        sample_acts = [lambda branch=branch: _bf(sact_s[branch]) for branch in range(N_BRANCH)]
        for stage in _tail_stages(xs_ref, sample_acts, ng_ref, *weights, ys_ref):
            stage()

    @pl.when(i == last)
    def _():
        state_in(last).wait()
        keys_in(last).wait()
        values_in(last).wait()


def _tail_specs(tile, ng, bg, wro, wco, wmo, wout, fg):
    tok = pl.BlockSpec((tile, D_MODEL), lambda i: (i, 0))

    def gate_cols(branch):
        blk = OFF_GATE // D_MODEL + branch
        return pl.BlockSpec((D_MODEL // 2, D_MODEL), lambda i: (0, blk), pipeline_mode=pl.Buffered(1))

    in_specs = [tok, tok, tok, tok,
                _resident(ng.shape), gate_cols(0), gate_cols(1), gate_cols(2), _resident(bg.shape),
                _resident(wro.shape), _resident(wco.shape), _resident(wmo.shape), _resident(wout.shape),
                _resident(fg.shape)]
    return tok, in_specs


def _tail_with_sample(x2d, aret, aconv, amem, ng, win_pk, bg, wro, wco, wmo, wout, fg,
                      dec, z, cos, sin, state, cstate, mk, mv, rng, cw, cbias, xs2d):
    steps = x2d.shape[0] // TAIL_TILE
    nb, seq, _ = z.shape
    grp = nb // steps
    assert grp * steps == nb and SAMPLE_TILE % (grp * seq) == 0
    steps_per_tile = SAMPLE_TILE // (grp * seq)
    tok, in_specs = _tail_specs(TAIL_TILE, ng, bg, wro, wco, wmo, wout, fg)
    sample_tok = pl.BlockSpec((SAMPLE_TILE, D_MODEL), lambda i: (i // steps_per_tile, 0))
    in_hbm = pl.BlockSpec(memory_space=pl.ANY)

    def per_req(*tail):
        zeros = (0,) * len(tail)
        return pl.BlockSpec((grp,) + tail, lambda i: (i,) + zeros)

    def const(shape):
        zeros = (0,) * len(shape)
        return pl.BlockSpec(shape, lambda i: zeros)

    return pl.pallas_call(
        _tail_with_sample_body,
        grid=(steps,),
        in_specs=in_specs + [pl.BlockSpec(memory_space=pltpu.SMEM),
                             per_req(seq, OFF_GATE), const(cos.shape), const(sin.shape),
                             in_hbm, per_req(CONV_WIDTH - 1, D_MODEL), in_hbm, in_hbm,
                             const(rng.shape), const(cw.shape), const(cbias.shape), sample_tok],
        out_specs=[tok, in_hbm, per_req(CONV_WIDTH - 1, D_MODEL), sample_tok],
        out_shape=[jax.ShapeDtypeStruct(x2d.shape, F32),
                   jax.ShapeDtypeStruct(state.shape, F32),
                   jax.ShapeDtypeStruct(cstate.shape, F32),
                   jax.ShapeDtypeStruct(xs2d.shape, F32)],
        scratch_shapes=[pltpu.VMEM((grp * (SUBLANES + seq), D_MODEL), F32),
                        pltpu.VMEM((HEADS, seq, seq), F32),
                        pltpu.VMEM((HEADS, seq, ROT_HALF), F32),
                        pltpu.VMEM((HEADS, seq, ROT_HALF), F32),
                        pltpu.VMEM((N_BRANCH, SAMPLE_TILE, D_MODEL), F32),
                        pltpu.VMEM((grp, HEADS, HEAD_DIM, HEAD_DIM), F32),
                        pltpu.VMEM((grp, MEM_LEN, HEADS, HEAD_DIM), F32),
                        pltpu.VMEM((grp, MEM_LEN, HEADS, HEAD_DIM), F32),
                        pltpu.SemaphoreType.DMA((4,))],
        compiler_params=pltpu.CompilerParams(dimension_semantics=("arbitrary",),
                                             vmem_limit_bytes=VMEM_LIMIT_BYTES),
        name="tail_with_sample",
    )(x2d, aret, aconv, amem, ng, win_pk, win_pk, win_pk, bg, wro, wco, wmo, wout, fg,
      dec, z, cos, sin, state, cstate, mk, mv, rng, cw, cbias, xs2d)


def _rope_tables(pos):
    inv = ROPE_BASE ** (-jnp.arange(ROT_HALF, dtype=F32) / ROT_HALF)
    ang = pos.astype(F32)[:, None] * inv[None, :]
    return jnp.cos(ang), jnp.sin(ang)


def _rope_tables_range(n, stride=64):
    cos_hi, sin_hi = _rope_tables(stride * jnp.arange(n // stride, dtype=jnp.int32))
    cos_lo, sin_lo = _rope_tables(jnp.arange(stride, dtype=jnp.int32))
    cos = cos_hi[:, None, :] * cos_lo[None] - sin_hi[:, None, :] * sin_lo[None]
    sin = sin_hi[:, None, :] * cos_lo[None] + cos_hi[:, None, :] * sin_lo[None]
    return cos.reshape(n, ROT_HALF), sin.reshape(n, ROT_HALF)


def _decay_scalars(chunk):
    lg = jnp.log1p(-jnp.exp2(-5.0 - jnp.arange(HEADS, dtype=F32)))
    return jnp.concatenate([lg, jnp.exp(lg * chunk)])


def kernel(x_prompt, x_sample, state_ret, state_conv, cache_mem_k, cache_mem_v, mem_prompt, norm_g, w_in, b_gate, ret_norm_g, conv_w, conv_b, w_ret_o, w_conv_o, w_mem_o, w_out, mem_norm_g, w_mem_kv, final_norm_g):
    assert norm_g.shape[0] == 1, "single-layer trunk"
    nbp, seq_p, _ = x_prompt.shape
    nbs, seq_s, _ = x_sample.shape
    assert seq_p % PROMPT_TILE == 0 and PROMPT_TILE % RET_CHUNK == 0 and (nbp * seq_p) % TAIL_TILE == 0
    assert seq_s == SUBLANES and (nbs * seq_s) % SAMPLE_TILE == 0

    ng = norm_g[0][None, :]
    bg = b_gate[0][None, :]
    rng = ret_norm_g[0].reshape(1, D_MODEL)
    cw = conv_w[0]
    cbias = conv_b[0][None, :]
    wro, wco, wmo, wout = _pack_weights(w_ret_o[0], w_conv_o[0], w_mem_o[0], w_out[0])
    fg = final_norm_g[None, :]

    cos_p, sin_p = _rope_tables_range(seq_p)
    cos_s, sin_s = _rope_tables(PAST_LEN + jnp.arange(seq_s, dtype=jnp.int32))

    xs2d = x_sample.reshape(nbs * seq_s, D_MODEL)
    z, win_pk, mk, mv, mk_bf, mv_bf = _prep(xs2d, ng, w_in[0], mem_prompt, mem_norm_g[0][None, :],
                                            w_mem_kv[0])
    z = z.reshape(nbs, seq_s, OFF_GATE)
    tail_weights = (ng, win_pk, bg, wro, wco, wmo, wout, fg)

    aret_p, aconv_p, amem_p, ret_p, conv_p = _prompt_branches(
        x_prompt, _decay_scalars(RET_CHUNK), cos_p, sin_p, mk_bf, mv_bf, ng, win_pk, rng, cw, cbias)

    flat_p = lambda a: a.reshape(nbp * seq_p, D_MODEL)
    y_prompt, ret_s, conv_s, y_sample = _tail_with_sample(
        flat_p(x_prompt), flat_p(aret_p), flat_p(aconv_p), flat_p(amem_p), *tail_weights,
        _decay_scalars(seq_s), z, cos_s, sin_s, state_ret[0], state_conv[0], cache_mem_k[0], cache_mem_v[0],
        rng, cw, cbias, xs2d)

    return (y_prompt.reshape(x_prompt.shape), y_sample.reshape(x_sample.shape),
            ret_p[None], ret_s[None], conv_p[None], conv_s[None], mk[None], mv[None])
```

```python
import functools

import jax
import jax.numpy as jnp
from jax import lax
from jax.experimental import pallas as pl
from jax.experimental.pallas import tpu as pltpu

F32 = jnp.float32
BF16 = jnp.bfloat16

D_MODEL = 1024
HEADS = 4
HEAD_DIM = D_MODEL // HEADS
ROT_HALF = HEAD_DIM // 2
ROPE_BASE = 10000.0
CONV_WIDTH = 3
MEM_LEN = 256
MEM_PIECE = 64
N_BRANCH = 3
EPS = 1e-6
PAST_LEN = 16384
QK_SCALE = HEAD_DIM ** -0.5

OFF_RQ, OFF_RK, OFF_RV, OFF_RG = 0, 1024, 2048, 3072
OFF_CU, OFF_CB, OFF_CC, OFF_CG = 4096, 5120, 6144, 7168
OFF_MQ, OFF_MG = 8192, 9216
OFF_GATE = 10240
IN_TOTAL = OFF_GATE + N_BRANCH * D_MODEL

SUBLANES = 8
VMEM_LIMIT_BYTES = 60 * 1024 * 1024

PROMPT_TILE = 512
RET_CHUNK = 256
TAIL_TILE = 256
SAMPLE_RING = 3


def _bf(x):
    return x.astype(BF16)


def _dot(a, b):
    return jnp.dot(a, b, preferred_element_type=F32)


def _dot_nt(a, b):
    return lax.dot_general(a, b, (((1,), (1,)), ((), ())), preferred_element_type=F32)


def _dot_tn(a, b):
    return lax.dot_general(a, b, (((0,), (0,)), ((), ())), preferred_element_type=F32)


def _rms(x, g):
    return x * lax.rsqrt(jnp.mean(x * x, axis=-1, keepdims=True) + EPS) * g


def _silu(x):
    return x * jax.nn.sigmoid(x)


def _rope(x, cos, sin):
    x1 = x[:, :ROT_HALF]
    x2 = x[:, ROT_HALF:]
    return jnp.concatenate([x1 * cos - x2 * sin, x1 * sin + x2 * cos], axis=-1)


def _lanes_twice(t):
    return jnp.concatenate([t, t], axis=-1)


def _pack_rows(w_bf):
    return pltpu.bitcast(w_bf, jnp.int32)


def _unpack_rows(w_packed):
    return pltpu.bitcast(w_packed, BF16)


def _fill_decay(dec_ref, inner_s, qdec_s, kdec_s, chunk):
    i = lax.broadcasted_iota(jnp.int32, (chunk, chunk), 0)
    j = lax.broadcasted_iota(jnp.int32, (chunk, chunk), 1)
    diff = (i - j).astype(F32)
    r = lax.broadcasted_iota(jnp.int32, (chunk, ROT_HALF), 0).astype(F32)
    for hd in range(HEADS):
        lg = dec_ref[hd]
        inner_s[hd] = jnp.where(diff >= 0, jnp.exp(lg * jnp.maximum(diff, 0.0)), 0.0)
        qdec_s[hd] = jnp.exp(lg * (r + 1.0))
        kdec_s[hd] = jnp.exp(lg * (chunk - 1.0 - r))


def _conv_branch(cu, cb, cc, cg, cbuf, cw_ref, cbias_ref, rows, cols):
    pre = cc * cu
    cbuf[SUBLANES:SUBLANES + rows, cols] = pre
    conv = (cbias_ref[:, cols]
            + cw_ref[0:1, cols] * cbuf[SUBLANES - 2:SUBLANES - 2 + rows, cols]
            + cw_ref[1:2, cols] * cbuf[SUBLANES - 1:SUBLANES - 1 + rows, cols]
            + cw_ref[2:3, cols] * pre)
    return pre, cb * conv * _silu(cg)


def _softmax_rows(s):
    e = jnp.exp(s - jnp.max(s, axis=-1, keepdims=True))
    return e * (1.0 / jnp.sum(e, axis=-1, keepdims=True))


def _resident(shape, block=None):
    zeros = (0,) * len(shape)
    return pl.BlockSpec(block or shape, lambda *_: zeros, pipeline_mode=pl.Buffered(1))


def _pack_weights_body(*refs):
    n = len(refs) // 2
    for w_ref, out_ref in zip(refs[:n], refs[n:]):
        out_ref[...] = _pack_rows(_bf(w_ref[...]))


def _pack_weights(*ws):
    return pl.pallas_call(
        _pack_weights_body,
        out_shape=[jax.ShapeDtypeStruct((w.shape[0] // 2, w.shape[1]), jnp.int32) for w in ws],
        compiler_params=pltpu.CompilerParams(vmem_limit_bytes=VMEM_LIMIT_BYTES),
        name="pack_weights",
    )(*ws)


def _prompt_branch_body(dec_ref, x_ref, cos_ref, sin_ref, mk_ref, mv_ref, ng_ref, win_ref, rng_ref,
                        cw_ref, cbias_ref,
                        aret_ref, aconv_ref, amem_ref, state_ref, nconv_ref,
                        cbuf, inner_s, qdec_s, kdec_s):
    b = pl.program_id(0)
    c = pl.program_id(1)
    tile = x_ref.shape[0]

    @pl.when(jnp.logical_and(b == 0, c == 0))
    def _():
        _fill_decay(dec_ref, inner_s, qdec_s, kdec_s, RET_CHUNK)

    @pl.when(c == 0)
    def _():
        state_ref[...] = jnp.zeros(state_ref.shape, F32)
        cbuf[0:SUBLANES, :] = jnp.zeros((SUBLANES, D_MODEL), F32)

    h = _bf(_rms(x_ref[...], ng_ref[...]))
    cos = cos_ref[...]
    sin = sin_ref[...]

    def proj(off):
        return _dot(h, _unpack_rows(win_ref[:, off:off + HEAD_DIM]))

    heads = range(HEADS)
    head_cols = [slice(hd * HEAD_DIM, (hd + 1) * HEAD_DIM) for hd in heads]

    def conv_block(hd):
        lo = hd * HEAD_DIM
        cols = head_cols[hd]
        pre, a_conv = _conv_branch(proj(OFF_CU + lo), proj(OFF_CB + lo), proj(OFF_CC + lo), proj(OFF_CG + lo),
                                   cbuf, cw_ref, cbias_ref, tile, cols)
        cbuf[0:SUBLANES, cols] = pre[tile - SUBLANES:, :]
        nconv_ref[:, cols] = pre[tile - (CONV_WIDTH - 1):, :]
        aconv_ref[:, cols] = _bf(a_conv)

    q = [_bf(_rope(proj(OFF_RQ + hd * HEAD_DIM), cos, sin)) for hd in heads]
    k = [_rope(proj(OFF_RK + hd * HEAD_DIM), cos, sin) * QK_SCALE for hd in heads]
    v = [_bf(proj(OFF_RV + hd * HEAD_DIM)) for hd in heads]
    chunks = [slice(j * RET_CHUNK, (j + 1) * RET_CHUNK) for j in range(tile // RET_CHUNK)]
    scores, cross = [], []
    for rows in chunks:
        for hd in heads:
            state = state_ref[hd]
            kc = k[hd][rows]
            scores.append(_bf(_dot_nt(q[hd][rows], _bf(kc)) * inner_s[hd]))
            cross.append(_dot(q[hd][rows], _bf(state)))
            state_ref[hd] = (state * dec_ref[HEADS + hd]
                             + _dot_tn(_bf(kc * _lanes_twice(kdec_s[hd])), v[hd][rows]))
    conv_block(0)
    ret_o = []
    for hd in heads:
        qdec = _lanes_twice(qdec_s[hd])
        ret_o.append(jnp.concatenate(
            [_dot(scores[j * HEADS + hd], v[hd][rows]) + cross[j * HEADS + hd] * qdec
             for j, rows in enumerate(chunks)], axis=0))
    mq = [_bf(proj(OFF_MQ + hd * HEAD_DIM)) for hd in heads]
    conv_block(1)
    mscores = [_dot_nt(mq[hd], mk_ref[:, head_cols[hd]]) * QK_SCALE for hd in heads]
    for hd in heads:
        cols = head_cols[hd]
        aret_ref[:, cols] = _bf(_rms(ret_o[hd], rng_ref[:, cols]) * _silu(proj(OFF_RG + hd * HEAD_DIM)))
    conv_block(2)
    probs = [_bf(_softmax_rows(mscores[hd])) for hd in heads]
    conv_block(3)
    for hd in heads:
        cols = head_cols[hd]
        om = _dot(probs[hd], mv_ref[:, cols])
        amem_ref[:, cols] = _bf(om * _silu(proj(OFF_MG + hd * HEAD_DIM)))


def _prompt_branches(x, dec, cos, sin, mk_bf, mv_bf, ng, win_pk, rng, cw, cbias):
    nb, seq, _ = x.shape
    tile = PROMPT_TILE
    tok = pl.BlockSpec((None, tile, D_MODEL), lambda b, c: (b, c, 0))
    rot = pl.BlockSpec((tile, ROT_HALF), lambda b, c: (c, 0))
    mem = pl.BlockSpec((None, MEM_LEN, D_MODEL), lambda b, c: (b, 0, 0))
    act = jax.ShapeDtypeStruct(x.shape, BF16)
    return pl.pallas_call(
        _prompt_branch_body,
        grid=(nb, seq // tile),
        in_specs=[pl.BlockSpec(memory_space=pltpu.SMEM), tok, rot, rot, mem, mem,
                  _resident(ng.shape), _resident(win_pk.shape, (D_MODEL // 2, OFF_GATE)), _resident(rng.shape),
                  _resident(cw.shape), _resident(cbias.shape)],
        out_specs=[tok, tok, tok,
                   pl.BlockSpec((None, HEADS, HEAD_DIM, HEAD_DIM), lambda b, c: (b, 0, 0, 0)),
                   pl.BlockSpec((None, CONV_WIDTH - 1, D_MODEL), lambda b, c: (b, 0, 0))],
        out_shape=[act, act, act,
                   jax.ShapeDtypeStruct((nb, HEADS, HEAD_DIM, HEAD_DIM), F32),
                   jax.ShapeDtypeStruct((nb, CONV_WIDTH - 1, D_MODEL), F32)],
        scratch_shapes=[pltpu.VMEM((SUBLANES + tile, D_MODEL), F32),
                        pltpu.VMEM((HEADS, RET_CHUNK, RET_CHUNK), F32),
                        pltpu.VMEM((HEADS, RET_CHUNK, ROT_HALF), F32),
                        pltpu.VMEM((HEADS, RET_CHUNK, ROT_HALF), F32)],
        compiler_params=pltpu.CompilerParams(dimension_semantics=("arbitrary", "arbitrary"),
                                             vmem_limit_bytes=VMEM_LIMIT_BYTES),
        name="prompt_branches",
    )(dec, x, cos, sin, mk_bf, mv_bf, ng, win_pk, rng, cw, cbias)


def _prep_body(x_ref, ng_ref, w_ref, mem_ref, mg_ref, wkv_ref,
               z_ref, wpk_ref, k_ref, v_ref, kb_ref, vb_ref, h_s, wkv_s, *, n_mem):
    n = pl.program_id(0)

    @pl.when(n == 0)
    def _():
        h_s[...] = _bf(_rms(x_ref[...], ng_ref[...]))
        wkv_s[...] = _bf(wkv_ref[...])

    w = _bf(w_ref[...])
    wpk_ref[...] = _pack_rows(w)

    @pl.when(n < OFF_GATE // w_ref.shape[1])
    def _():
        z_ref[...] = _dot(h_s[...], w)

    @pl.when(n < n_mem)
    def _():
        kv = _dot(_bf(_rms(mem_ref[...], mg_ref[...])), wkv_s[...])
        k = kv[:, :D_MODEL]
        v = kv[:, D_MODEL:]
        for hd in range(HEADS):
            lo = hd * HEAD_DIM
            k_ref[:, hd, :] = k[:, lo:lo + HEAD_DIM]
            v_ref[:, hd, :] = v[:, lo:lo + HEAD_DIM]
        kb_ref[...] = _bf(k)
        vb_ref[...] = _bf(v)


def _prep(x2d, ng, w_in, mem, mem_g, w_mem_kv):
    ntok = x2d.shape[0]
    n_mem = mem.shape[0]
    blk = D_MODEL
    steps = IN_TOTAL // blk
    z_blocks = OFF_GATE // blk
    assert n_mem <= steps

    def mem_blk(*tail):
        zeros = (0,) * len(tail)
        return pl.BlockSpec((None,) + tail, lambda n: (jnp.minimum(n, n_mem - 1),) + zeros)

    heads_shape = jax.ShapeDtypeStruct((n_mem, MEM_LEN, HEADS, HEAD_DIM), F32)
    return pl.pallas_call(
        functools.partial(_prep_body, n_mem=n_mem),
        grid=(steps,),
        in_specs=[_resident(x2d.shape), _resident(ng.shape),
                  pl.BlockSpec((D_MODEL, blk), lambda n: (0, n)),
                  mem_blk(MEM_LEN, D_MODEL), _resident(mem_g.shape), _resident(w_mem_kv.shape)],
        out_specs=[pl.BlockSpec((ntok, blk), lambda n: (0, jnp.minimum(n, z_blocks - 1))),
                   pl.BlockSpec((D_MODEL // 2, blk), lambda n: (0, n)),
                   mem_blk(MEM_LEN, HEADS, HEAD_DIM), mem_blk(MEM_LEN, HEADS, HEAD_DIM),
                   mem_blk(MEM_LEN, D_MODEL), mem_blk(MEM_LEN, D_MODEL)],
        out_shape=[jax.ShapeDtypeStruct((ntok, OFF_GATE), F32),
                   jax.ShapeDtypeStruct((D_MODEL // 2, IN_TOTAL), jnp.int32),
                   heads_shape, heads_shape,
                   jax.ShapeDtypeStruct(mem.shape, BF16), jax.ShapeDtypeStruct(mem.shape, BF16)],
        scratch_shapes=[pltpu.VMEM(x2d.shape, BF16), pltpu.VMEM(w_mem_kv.shape, BF16)],
        compiler_params=pltpu.CompilerParams(dimension_semantics=("arbitrary",),
                                             vmem_limit_bytes=VMEM_LIMIT_BYTES),
        name="prep",
    )(x2d, ng, w_in, mem, mem_g, w_mem_kv)


def _tail_stages(x_ref, acts, ng_ref, wg_refs, bg_ref, wro_ref, wco_ref, wmo_ref, wout_ref, fg_ref, y_ref):
    live = {}
    w_refs = (wro_ref, wco_ref, wmo_ref)

    def gate(branch):
        def run(between=()):
            if branch == 0:
                live["h"] = _bf(_rms(x_ref[...], ng_ref[...]))
            nblk = max(len(between), 1)
            width = D_MODEL // nblk
            parts = []
            for i in range(nblk):
                cols = slice(i * width, (i + 1) * width)
                logits = _dot(live["h"], _unpack_rows(wg_refs[branch][:, cols]))
                parts.append(jax.nn.sigmoid(logits + bg_ref[:, branch * D_MODEL + i * width:
                                                            branch * D_MODEL + (i + 1) * width]))
                if between:
                    between[i]()
            live["gate"] = jnp.concatenate(parts, axis=-1)
        return run

    def project(branch):
        def run(between=()):
            nblk = max(len(between), 1)
            width = D_MODEL // nblk
            parts = []
            for i in range(nblk):
                cols = slice(i * width, (i + 1) * width)
                parts.append(live["gate"][:, cols] * _dot(acts[branch](), _unpack_rows(w_refs[branch][:, cols])))
                if between:
                    between[i]()
            y_branch = jnp.concatenate(parts, axis=-1)
            live["merged"] = y_branch if branch == 0 else live["merged"] + y_branch
        return run

    def out_project():
        live["y"] = x_ref[...] + _dot(_bf(live["merged"]), _unpack_rows(wout_ref[...]))

    def finish():
        y_ref[...] = _rms(live["y"], fg_ref[...])

    stages = []
    for branch in range(N_BRANCH):
        stages += [gate(branch), project(branch)]
    return stages + [out_project, finish]


def _sample_stages(z_ref, cos_ref, sin_ref, state_ref, cstate_ref, mk_ref, mv_ref, rng_ref, cw_ref, cbias_ref,
                   store_act, nstate_ref, nconv_ref, cbuf, inner_s, qdec_s, kdec_s, dec_ref):
    group, seq, _ = z_ref.shape
    pairs = [(g, hd) for g in range(group) for hd in range(HEADS)]
    live = {}

    def col(g, off, width=HEAD_DIM):
        return z_ref[g, :, off:off + width]

    def ret_inputs():
        cos = cos_ref[...]
        sin = sin_ref[...]
        live["q"] = {p: _bf(_rope(col(p[0], OFF_RQ + p[1] * HEAD_DIM), cos, sin)) for p in pairs}
        live["k"] = {p: _rope(col(p[0], OFF_RK + p[1] * HEAD_DIM), cos, sin) * QK_SCALE for p in pairs}
        live["v"] = {p: _bf(col(p[0], OFF_RV + p[1] * HEAD_DIM)) for p in pairs}

    def ret_products():
        live["scores"], live["cross"] = {}, {}
        for p in pairs:
            g, hd = p
            state = state_ref[g, hd]
            k = live["k"][p]
            live["scores"][p] = _bf(_dot_nt(live["q"][p], _bf(k)) * inner_s[hd])
            live["cross"][p] = _dot(live["q"][p], _bf(state))
            nstate_ref[g, hd] = (state * dec_ref[HEADS + hd]
                                 + _dot_tn(_bf(k * _lanes_twice(kdec_s[hd])), live["v"][p]))

    def ret_outputs():
        for p in pairs:
            g, hd = p
            lo = hd * HEAD_DIM
            o = _dot(live["scores"][p], live["v"][p]) + live["cross"][p] * _lanes_twice(qdec_s[hd])
            o = _rms(o, rng_ref[:, lo:lo + HEAD_DIM])
            store_act(0, g, slice(lo, lo + HEAD_DIM), o * _silu(col(g, OFF_RG + lo)))

    def conv():
        for g in range(group):
            base = g * (SUBLANES + seq)
            cbuf[base:base + SUBLANES, :] = jnp.zeros((SUBLANES, D_MODEL), F32)
            cbuf[base + SUBLANES - (CONV_WIDTH - 1):base + SUBLANES, :] = cstate_ref[g]
            pre, a_conv = _conv_branch(col(g, OFF_CU, D_MODEL), col(g, OFF_CB, D_MODEL), col(g, OFF_CC, D_MODEL),
                                       col(g, OFF_CG, D_MODEL), cbuf.at[base:base + SUBLANES + seq], cw_ref,
                                       cbias_ref, seq, slice(0, D_MODEL))
            nconv_ref[g] = pre[seq - (CONV_WIDTH - 1):, :]
            store_act(1, g, slice(0, D_MODEL), a_conv)

    def mem_score_piece(g, j):
        def run():
            if j == 0:
                live.setdefault("ms", {})[g] = []
                live.setdefault("mq", {})[g] = _bf(
                    jnp.concatenate([col(g, OFF_MQ + hd * HEAD_DIM) for hd in range(HEADS)], axis=0))
            keys = mk_ref[g, j * MEM_PIECE:(j + 1) * MEM_PIECE].reshape(MEM_PIECE * HEADS, HEAD_DIM)
            live["ms"][g].append(_dot_nt(live["mq"][g], _bf(keys)) * QK_SCALE)
        return run

    def mem_probs():
        live["mp"] = []
        for g in range(group):
            s = jnp.concatenate(live["ms"][g], axis=-1)
            row_head = lax.broadcasted_iota(jnp.int32, s.shape, 0) // seq
            col_head = lax.broadcasted_iota(jnp.int32, s.shape, 1) % HEADS
            live["mp"].append(_bf(_softmax_rows(jnp.where(row_head == col_head, s, -jnp.inf))))

    def mem_outputs():
        for g in range(group):
            mv2d = _bf(mv_ref[g].reshape(MEM_LEN * HEADS, HEAD_DIM))
            om_all = _dot(live["mp"][g], mv2d)
            for hd in range(HEADS):
                lo = hd * HEAD_DIM
                om = om_all[hd * seq:(hd + 1) * seq, :]
                store_act(2, g, slice(lo, lo + HEAD_DIM), om * _silu(col(g, OFF_MG + lo)))

    pieces = [[mem_score_piece(g, j) for j in range(MEM_LEN // MEM_PIECE)] for g in range(group)]
    assert group == 2
    return [(ret_inputs, ()), (ret_products, ()), (None, pieces[0]), (ret_outputs, pieces[1]),
            (mem_probs, ()), (conv, ()), (mem_outputs, ())]


def _tail_with_sample_body(x_ref, aret_ref, aconv_ref, amem_ref, ng_ref, wg0_ref, wg1_ref, wg2_ref, bg_ref,
                           wro_ref, wco_ref, wmo_ref, wout_ref, fg_ref,
                           dec_ref, z_ref, cos_ref, sin_ref, state_hbm, cstate_ref, mk_hbm, mv_hbm, rng_ref,
                           cw_ref, cbias_ref, xs_ref,
                           y_ref, nstate_ref, nconv_ref, ys_ref,
                           cbuf, inner_s, qdec_s, kdec_s, sact_s, state_ring, key_ring, value_ring, sems):
    i = pl.program_id(0)
    steps = pl.num_programs(0)
    group, seq, _ = z_ref.shape
    ring = state_ring.shape[0]

    def fetches(step):
        slot = lax.rem(step, ring)
        requests = pl.ds(step * group, group)
        return [pltpu.make_async_copy(hbm.at[requests], buf.at[slot], sems.at[k, slot])
                for k, (hbm, buf) in enumerate(((state_hbm, state_ring), (mk_hbm, key_ring),
                                                (mv_hbm, value_ring)))]

    @pl.when(i == 0)
    def _():
        for step in range(ring - 1):
            for fetch in fetches(step):
                fetch.start()

    @pl.when(i + (ring - 1) < steps)
    def _():
        for fetch in fetches(i + (ring - 1)):
            fetch.start()

    for fetch in fetches(i):
        fetch.wait()
    cur = lax.rem(i, ring)
    state_ref, mk_ref, mv_ref = state_ring.at[cur], key_ring.at[cur], value_ring.at[cur]
    steps_per_tile = xs_ref.shape[0] // (group * seq)
    slot = lax.rem(i, steps_per_tile)

    @pl.when(i == 0)
    def _():
        _fill_decay(dec_ref, inner_s, qdec_s, kdec_s, seq)

    def store_act(branch, g, cols, value):
        row = pl.multiple_of((slot * group + g) * seq, seq)
        sact_s[branch, pl.ds(row, seq), cols] = value

    weights = ((wg0_ref, wg1_ref, wg2_ref), bg_ref, wro_ref, wco_ref, wmo_ref, wout_ref, fg_ref)
    prompt_acts = [lambda ref=ref: ref[...] for ref in (aret_ref, aconv_ref, amem_ref)]
    tail = _tail_stages(x_ref, prompt_acts, ng_ref, *weights, y_ref)
    sample = _sample_stages(z_ref, cos_ref, sin_ref, state_ref, cstate_ref, mk_ref, mv_ref, rng_ref, cw_ref,
                            cbias_ref, store_act, nstate_ref, nconv_ref, cbuf, inner_s, qdec_s, kdec_s, dec_ref)
    assert len(sample) == len(tail) - 1
    for j, tail_stage in enumerate(tail):
        before, between = sample[j] if j < len(sample) - 1 else sample[-1] if j == len(tail) - 1 else (None, ())
        if before is not None:
            before()
        if between:
            tail_stage(between=between)
        else:
            tail_stage()

    @pl.when(slot == steps_per_tile - 1)
    def _():
        sample_acts = [lambda branch=branch: _bf(sact_s[branch]) for branch in range(N_BRANCH)]
        for stage in _tail_stages(xs_ref, sample_acts, ng_ref, *weights, ys_ref):
            stage()


def _tail_specs(tile, ng, bg, wro, wco, wmo, wout, fg):
    tok = pl.BlockSpec((tile, D_MODEL), lambda i: (i, 0))

    def gate_cols(branch):
        blk = OFF_GATE // D_MODEL + branch
        return pl.BlockSpec((D_MODEL // 2, D_MODEL), lambda i: (0, blk), pipeline_mode=pl.Buffered(1))

    in_specs = [tok, tok, tok, tok,
                _resident(ng.shape), gate_cols(0), gate_cols(1), gate_cols(2), _resident(bg.shape),
                _resident(wro.shape), _resident(wco.shape), _resident(wmo.shape), _resident(wout.shape),
                _resident(fg.shape)]
    return tok, in_specs


def _tail_with_sample(x2d, aret, aconv, amem, ng, win_pk, bg, wro, wco, wmo, wout, fg,
                      dec, z, cos, sin, state, cstate, mk, mv, rng, cw, cbias, xs2d):
    steps = x2d.shape[0] // TAIL_TILE
    nb, seq, _ = z.shape
    grp = nb // steps
    assert grp * steps == nb and TAIL_TILE % (grp * seq) == 0
    steps_per_tile = TAIL_TILE // (grp * seq)
    tok, in_specs = _tail_specs(TAIL_TILE, ng, bg, wro, wco, wmo, wout, fg)
    sample_tok = pl.BlockSpec((TAIL_TILE, D_MODEL), lambda i: (i // steps_per_tile, 0))
    in_hbm = pl.BlockSpec(memory_space=pl.ANY)

    def per_req(*tail):
        zeros = (0,) * len(tail)
        return pl.BlockSpec((grp,) + tail, lambda i: (i,) + zeros)

    def const(shape):
        zeros = (0,) * len(shape)
        return pl.BlockSpec(shape, lambda i: zeros)

    return pl.pallas_call(
        _tail_with_sample_body,
        grid=(steps,),
        in_specs=in_specs + [pl.BlockSpec(memory_space=pltpu.SMEM),
                             per_req(seq, OFF_GATE), const(cos.shape), const(sin.shape),
                             in_hbm, per_req(CONV_WIDTH - 1, D_MODEL), in_hbm, in_hbm,
                             const(rng.shape), const(cw.shape), const(cbias.shape), sample_tok],
        out_specs=[tok, per_req(HEADS, HEAD_DIM, HEAD_DIM), per_req(CONV_WIDTH - 1, D_MODEL), sample_tok],
        out_shape=[jax.ShapeDtypeStruct(x2d.shape, F32),
                   jax.ShapeDtypeStruct(state.shape, F32),
                   jax.ShapeDtypeStruct(cstate.shape, F32),
                   jax.ShapeDtypeStruct(xs2d.shape, F32)],
        scratch_shapes=[pltpu.VMEM((grp * (SUBLANES + seq), D_MODEL), F32),
                        pltpu.VMEM((HEADS, seq, seq), F32),
                        pltpu.VMEM((HEADS, seq, ROT_HALF), F32),
                        pltpu.VMEM((HEADS, seq, ROT_HALF), F32),
                        pltpu.VMEM((N_BRANCH, TAIL_TILE, D_MODEL), F32),
                        pltpu.VMEM((SAMPLE_RING, grp, HEADS, HEAD_DIM, HEAD_DIM), F32),
                        pltpu.VMEM((SAMPLE_RING, grp, MEM_LEN, HEADS, HEAD_DIM), F32),
                        pltpu.VMEM((SAMPLE_RING, grp, MEM_LEN, HEADS, HEAD_DIM), F32),
                        pltpu.SemaphoreType.DMA((3, SAMPLE_RING))],
        compiler_params=pltpu.CompilerParams(dimension_semantics=("arbitrary",),
                                             vmem_limit_bytes=VMEM_LIMIT_BYTES),
        name="tail_with_sample",
    )(x2d, aret, aconv, amem, ng, win_pk, win_pk, win_pk, bg, wro, wco, wmo, wout, fg,
      dec, z, cos, sin, state, cstate, mk, mv, rng, cw, cbias, xs2d)


def _rope_tables(pos):
    inv = ROPE_BASE ** (-jnp.arange(ROT_HALF, dtype=F32) / ROT_HALF)
    ang = pos.astype(F32)[:, None] * inv[None, :]
    return jnp.cos(ang), jnp.sin(ang)


def _rope_tables_range(n, stride=64):
    cos_hi, sin_hi = _rope_tables(stride * jnp.arange(n // stride, dtype=jnp.int32))
    cos_lo, sin_lo = _rope_tables(jnp.arange(stride, dtype=jnp.int32))
    cos = cos_hi[:, None, :] * cos_lo[None] - sin_hi[:, None, :] * sin_lo[None]
    sin = sin_hi[:, None, :] * cos_lo[None] + cos_hi[:, None, :] * sin_lo[None]
    return cos.reshape(n, ROT_HALF), sin.reshape(n, ROT_HALF)


def _decay_scalars(chunk):
    lg = jnp.log1p(-jnp.exp2(-5.0 - jnp.arange(HEADS, dtype=F32)))
    return jnp.concatenate([lg, jnp.exp(lg * chunk)])


def kernel(x_prompt, x_sample, state_ret, state_conv, cache_mem_k, cache_mem_v, mem_prompt, norm_g, w_in, b_gate, ret_norm_g, conv_w, conv_b, w_ret_o, w_conv_o, w_mem_o, w_out, mem_norm_g, w_mem_kv, final_norm_g):
    assert norm_g.shape[0] == 1, "single-layer trunk"
    nbp, seq_p, _ = x_prompt.shape
    nbs, seq_s, _ = x_sample.shape
    assert seq_p % PROMPT_TILE == 0 and PROMPT_TILE % RET_CHUNK == 0 and (nbp * seq_p) % TAIL_TILE == 0
    assert seq_s == SUBLANES and (nbs * seq_s) % TAIL_TILE == 0

    ng = norm_g[0][None, :]
    bg = b_gate[0][None, :]
    rng = ret_norm_g[0].reshape(1, D_MODEL)
    cw = conv_w[0]
    cbias = conv_b[0][None, :]
    wro, wco, wmo, wout = _pack_weights(w_ret_o[0], w_conv_o[0], w_mem_o[0], w_out[0])
    fg = final_norm_g[None, :]

    cos_p, sin_p = _rope_tables_range(seq_p)
    cos_s, sin_s = _rope_tables(PAST_LEN + jnp.arange(seq_s, dtype=jnp.int32))

    xs2d = x_sample.reshape(nbs * seq_s, D_MODEL)
    z, win_pk, mk, mv, mk_bf, mv_bf = _prep(xs2d, ng, w_in[0], mem_prompt, mem_norm_g[0][None, :],
                                            w_mem_kv[0])
    z = z.reshape(nbs, seq_s, OFF_GATE)
    tail_weights = (ng, win_pk, bg, wro, wco, wmo, wout, fg)

    aret_p, aconv_p, amem_p, ret_p, conv_p = _prompt_branches(
        x_prompt, _decay_scalars(RET_CHUNK), cos_p, sin_p, mk_bf, mv_bf, ng, win_pk, rng, cw, cbias)

    flat_p = lambda a: a.reshape(nbp * seq_p, D_MODEL)
    y_prompt, ret_s, conv_s, y_sample = _tail_with_sample(
        flat_p(x_prompt), flat_p(aret_p), flat_p(aconv_p), flat_p(amem_p), *tail_weights,
        _decay_scalars(seq_s), z, cos_s, sin_s, state_ret[0], state_conv[0], cache_mem_k[0], cache_mem_v[0],
        rng, cw, cbias, xs2d)

    return (y_prompt.reshape(x_prompt.shape), y_sample.reshape(x_sample.shape),
            ret_p[None], ret_s[None], conv_p[None], conv_s[None], mk[None], mv[None])
```

```python
import functools

import jax
import jax.numpy as jnp
from jax import lax
from jax.experimental import pallas as pl
from jax.experimental.pallas import tpu as pltpu

F32 = jnp.float32
BF16 = jnp.bfloat16

D_MODEL = 1024
HEADS = 4
HEAD_DIM = D_MODEL // HEADS
ROT_HALF = HEAD_DIM // 2
ROPE_BASE = 10000.0
CONV_WIDTH = 3
MEM_LEN = 256
MEM_PIECE = 64
N_BRANCH = 3
EPS = 1e-6
PAST_LEN = 16384
QK_SCALE = HEAD_DIM ** -0.5

OFF_RQ, OFF_RK, OFF_RV, OFF_RG = 0, 1024, 2048, 3072
OFF_CU, OFF_CB, OFF_CC, OFF_CG = 4096, 5120, 6144, 7168
OFF_MQ, OFF_MG = 8192, 9216
OFF_GATE = 10240
IN_TOTAL = OFF_GATE + N_BRANCH * D_MODEL

SUBLANES = 8
VMEM_LIMIT_BYTES = 60 * 1024 * 1024

PROMPT_TILE = 512
RET_CHUNK = 256
TAIL_TILE = 256


def _bf(x):
    return x.astype(BF16)


def _dot(a, b):
    return jnp.dot(a, b, preferred_element_type=F32)


def _dot_nt(a, b):
    return lax.dot_general(a, b, (((1,), (1,)), ((), ())), preferred_element_type=F32)


def _dot_tn(a, b):
    return lax.dot_general(a, b, (((0,), (0,)), ((), ())), preferred_element_type=F32)


def _rms(x, g):
    return x * lax.rsqrt(jnp.mean(x * x, axis=-1, keepdims=True) + EPS) * g


def _silu(x):
    return x * jax.nn.sigmoid(x)


def _rope(x, cos, sin):
    x1 = x[:, :ROT_HALF]
    x2 = x[:, ROT_HALF:]
    return jnp.concatenate([x1 * cos - x2 * sin, x1 * sin + x2 * cos], axis=-1)


def _lanes_twice(t):
    return jnp.concatenate([t, t], axis=-1)


def _pack_rows(w_bf):
    return pltpu.bitcast(w_bf, jnp.int32)


def _unpack_rows(w_packed):
    return pltpu.bitcast(w_packed, BF16)


def _fill_decay(dec_ref, inner_s, qdec_s, kdec_s, chunk):
    i = lax.broadcasted_iota(jnp.int32, (chunk, chunk), 0)
    j = lax.broadcasted_iota(jnp.int32, (chunk, chunk), 1)
    diff = (i - j).astype(F32)
    r = lax.broadcasted_iota(jnp.int32, (chunk, ROT_HALF), 0).astype(F32)
    for hd in range(HEADS):
        lg = dec_ref[hd]
        inner_s[hd] = jnp.where(diff >= 0, jnp.exp(lg * jnp.maximum(diff, 0.0)), 0.0)
        qdec_s[hd] = jnp.exp(lg * (r + 1.0))
        kdec_s[hd] = jnp.exp(lg * (chunk - 1.0 - r))


def _conv_branch(cu, cb, cc, cg, cbuf, cw_ref, cbias_ref, rows, cols):
    pre = cc * cu
    cbuf[SUBLANES:SUBLANES + rows, cols] = pre
    conv = (cbias_ref[:, cols]
            + cw_ref[0:1, cols] * cbuf[SUBLANES - 2:SUBLANES - 2 + rows, cols]
            + cw_ref[1:2, cols] * cbuf[SUBLANES - 1:SUBLANES - 1 + rows, cols]
            + cw_ref[2:3, cols] * pre)
    return pre, cb * conv * _silu(cg)


def _softmax_rows(s):
    e = jnp.exp(s - jnp.max(s, axis=-1, keepdims=True))
    return e * (1.0 / jnp.sum(e, axis=-1, keepdims=True))


def _resident(shape, block=None):
    zeros = (0,) * len(shape)
    return pl.BlockSpec(block or shape, lambda *_: zeros, pipeline_mode=pl.Buffered(1))


def _pack_weights_body(*refs):
    n = len(refs) // 2
    for w_ref, out_ref in zip(refs[:n], refs[n:]):
        out_ref[...] = _pack_rows(_bf(w_ref[...]))


def _pack_weights(*ws):
    return pl.pallas_call(
        _pack_weights_body,
        out_shape=[jax.ShapeDtypeStruct((w.shape[0] // 2, w.shape[1]), jnp.int32) for w in ws],
        compiler_params=pltpu.CompilerParams(vmem_limit_bytes=VMEM_LIMIT_BYTES),
        name="pack_weights",
    )(*ws)


def _prompt_branch_body(dec_ref, x_ref, cos_ref, sin_ref, mk_ref, mv_ref, ng_ref, win_ref, rng_ref,
                        cw_ref, cbias_ref,
                        aret_ref, aconv_ref, amem_ref, state_ref, nconv_ref,
                        cbuf, inner_s, qdec_s, kdec_s):
    b = pl.program_id(0)
    c = pl.program_id(1)
    tile = x_ref.shape[0]

    @pl.when(jnp.logical_and(b == 0, c == 0))
    def _():
        _fill_decay(dec_ref, inner_s, qdec_s, kdec_s, RET_CHUNK)

    @pl.when(c == 0)
    def _():
        state_ref[...] = jnp.zeros(state_ref.shape, F32)
        cbuf[0:SUBLANES, :] = jnp.zeros((SUBLANES, D_MODEL), F32)

    h = _bf(_rms(x_ref[...], ng_ref[...]))
    cos = cos_ref[...]
    sin = sin_ref[...]

    def proj(off):
        return _dot(h, _unpack_rows(win_ref[:, off:off + HEAD_DIM]))

    heads = range(HEADS)
    head_cols = [slice(hd * HEAD_DIM, (hd + 1) * HEAD_DIM) for hd in heads]

    def conv_block(hd):
        lo = hd * HEAD_DIM
        cols = head_cols[hd]
        pre, a_conv = _conv_branch(proj(OFF_CU + lo), proj(OFF_CB + lo), proj(OFF_CC + lo), proj(OFF_CG + lo),
                                   cbuf, cw_ref, cbias_ref, tile, cols)
        cbuf[0:SUBLANES, cols] = pre[tile - SUBLANES:, :]
        nconv_ref[:, cols] = pre[tile - (CONV_WIDTH - 1):, :]
        aconv_ref[:, cols] = _bf(a_conv)

    q = [_bf(_rope(proj(OFF_RQ + hd * HEAD_DIM), cos, sin)) for hd in heads]
    k = [_rope(proj(OFF_RK + hd * HEAD_DIM), cos, sin) * QK_SCALE for hd in heads]
    v = [_bf(proj(OFF_RV + hd * HEAD_DIM)) for hd in heads]
    chunks = [slice(j * RET_CHUNK, (j + 1) * RET_CHUNK) for j in range(tile // RET_CHUNK)]
    scores, cross = [], []
    for rows in chunks:
        for hd in heads:
            state = state_ref[hd]
            kc = k[hd][rows]
            scores.append(_bf(_dot_nt(q[hd][rows], _bf(kc)) * inner_s[hd]))
            cross.append(_dot(q[hd][rows], _bf(state)))
            state_ref[hd] = (state * dec_ref[HEADS + hd]
                             + _dot_tn(_bf(kc * _lanes_twice(kdec_s[hd])), v[hd][rows]))
    conv_block(0)
    ret_o = []
    for hd in heads:
        qdec = _lanes_twice(qdec_s[hd])
        ret_o.append(jnp.concatenate(
            [_dot(scores[j * HEADS + hd], v[hd][rows]) + cross[j * HEADS + hd] * qdec
             for j, rows in enumerate(chunks)], axis=0))
    mq = [_bf(proj(OFF_MQ + hd * HEAD_DIM)) for hd in heads]
    conv_block(1)
    mscores = [_dot_nt(mq[hd], mk_ref[:, head_cols[hd]]) * QK_SCALE for hd in heads]
    for hd in heads:
        cols = head_cols[hd]
        aret_ref[:, cols] = _bf(_rms(ret_o[hd], rng_ref[:, cols]) * _silu(proj(OFF_RG + hd * HEAD_DIM)))
    conv_block(2)
    probs = [_bf(_softmax_rows(mscores[hd])) for hd in heads]
    conv_block(3)
    for hd in heads:
        cols = head_cols[hd]
        om = _dot(probs[hd], mv_ref[:, cols])
        amem_ref[:, cols] = _bf(om * _silu(proj(OFF_MG + hd * HEAD_DIM)))


def _prompt_branches(x, dec, cos, sin, mk_bf, mv_bf, ng, win_pk, rng, cw, cbias):
    nb, seq, _ = x.shape
    tile = PROMPT_TILE
    tok = pl.BlockSpec((None, tile, D_MODEL), lambda b, c: (b, c, 0))
    rot = pl.BlockSpec((tile, ROT_HALF), lambda b, c: (c, 0))
    mem = pl.BlockSpec((None, MEM_LEN, D_MODEL), lambda b, c: (b, 0, 0))
    act = jax.ShapeDtypeStruct(x.shape, BF16)
    return pl.pallas_call(
        _prompt_branch_body,
        grid=(nb, seq // tile),
        in_specs=[pl.BlockSpec(memory_space=pltpu.SMEM), tok, rot, rot, mem, mem,
                  _resident(ng.shape), _resident(win_pk.shape, (D_MODEL // 2, OFF_GATE)), _resident(rng.shape),
                  _resident(cw.shape), _resident(cbias.shape)],
        out_specs=[tok, tok, tok,
                   pl.BlockSpec((None, HEADS, HEAD_DIM, HEAD_DIM), lambda b, c: (b, 0, 0, 0)),
                   pl.BlockSpec((None, CONV_WIDTH - 1, D_MODEL), lambda b, c: (b, 0, 0))],
        out_shape=[act, act, act,
                   jax.ShapeDtypeStruct((nb, HEADS, HEAD_DIM, HEAD_DIM), F32),
                   jax.ShapeDtypeStruct((nb, CONV_WIDTH - 1, D_MODEL), F32)],
        scratch_shapes=[pltpu.VMEM((SUBLANES + tile, D_MODEL), F32),
                        pltpu.VMEM((HEADS, RET_CHUNK, RET_CHUNK), F32),
                        pltpu.VMEM((HEADS, RET_CHUNK, ROT_HALF), F32),
                        pltpu.VMEM((HEADS, RET_CHUNK, ROT_HALF), F32)],
        compiler_params=pltpu.CompilerParams(dimension_semantics=("arbitrary", "arbitrary"),
                                             vmem_limit_bytes=VMEM_LIMIT_BYTES),
        name="prompt_branches",
    )(dec, x, cos, sin, mk_bf, mv_bf, ng, win_pk, rng, cw, cbias)


def _prep_body(x_ref, ng_ref, w_ref, mem_ref, mg_ref, wkv_ref,
               z_ref, wpk_ref, k_ref, v_ref, kb_ref, vb_ref, h_s, wkv_s, *, n_mem):
    n = pl.program_id(0)

    @pl.when(n == 0)
    def _():
        h_s[...] = _bf(_rms(x_ref[...], ng_ref[...]))
        wkv_s[...] = _bf(wkv_ref[...])

    w = _bf(w_ref[...])
    wpk_ref[...] = _pack_rows(w)

    @pl.when(n < OFF_GATE // w_ref.shape[1])
    def _():
        z_ref[...] = _dot(h_s[...], w)

    @pl.when(n < n_mem)
    def _():
        kv = _dot(_bf(_rms(mem_ref[...], mg_ref[...])), wkv_s[...])
        k = kv[:, :D_MODEL]
        v = kv[:, D_MODEL:]
        for hd in range(HEADS):
            lo = hd * HEAD_DIM
            k_ref[:, hd, :] = k[:, lo:lo + HEAD_DIM]
            v_ref[:, hd, :] = v[:, lo:lo + HEAD_DIM]
        kb_ref[...] = _bf(k)
        vb_ref[...] = _bf(v)


def _prep(x2d, ng, w_in, mem, mem_g, w_mem_kv):
    ntok = x2d.shape[0]
    n_mem = mem.shape[0]
    blk = D_MODEL
    steps = IN_TOTAL // blk
    z_blocks = OFF_GATE // blk
    assert n_mem <= steps

    def mem_blk(*tail):
        zeros = (0,) * len(tail)
        return pl.BlockSpec((None,) + tail, lambda n: (jnp.minimum(n, n_mem - 1),) + zeros)

    heads_shape = jax.ShapeDtypeStruct((n_mem, MEM_LEN, HEADS, HEAD_DIM), F32)
    return pl.pallas_call(
        functools.partial(_prep_body, n_mem=n_mem),
        grid=(steps,),
        in_specs=[_resident(x2d.shape), _resident(ng.shape),
                  pl.BlockSpec((D_MODEL, blk), lambda n: (0, n)),
                  mem_blk(MEM_LEN, D_MODEL), _resident(mem_g.shape), _resident(w_mem_kv.shape)],
        out_specs=[pl.BlockSpec((ntok, blk), lambda n: (0, jnp.minimum(n, z_blocks - 1))),
                   pl.BlockSpec((D_MODEL // 2, blk), lambda n: (0, n)),
                   mem_blk(MEM_LEN, HEADS, HEAD_DIM), mem_blk(MEM_LEN, HEADS, HEAD_DIM),
                   mem_blk(MEM_LEN, D_MODEL), mem_blk(MEM_LEN, D_MODEL)],
        out_shape=[jax.ShapeDtypeStruct((ntok, OFF_GATE), F32),
                   jax.ShapeDtypeStruct((D_MODEL // 2, IN_TOTAL), jnp.int32),
                   heads_shape, heads_shape,
                   jax.ShapeDtypeStruct(mem.shape, BF16), jax.ShapeDtypeStruct(mem.shape, BF16)],
        scratch_shapes=[pltpu.VMEM(x2d.shape, BF16), pltpu.VMEM(w_mem_kv.shape, BF16)],
        compiler_params=pltpu.CompilerParams(dimension_semantics=("arbitrary",),
                                             vmem_limit_bytes=VMEM_LIMIT_BYTES),
        name="prep",
    )(x2d, ng, w_in, mem, mem_g, w_mem_kv)


def _tail_stages(x_ref, acts, ng_ref, wg_refs, bg_ref, wro_ref, wco_ref, wmo_ref, wout_ref, fg_ref, y_ref):
    live = {}
    w_refs = (wro_ref, wco_ref, wmo_ref)

    def gate(branch):
        def run(between=()):
            if branch == 0:
                live["h"] = _bf(_rms(x_ref[...], ng_ref[...]))
            nblk = max(len(between), 1)
            width = D_MODEL // nblk
            parts = []
            for i in range(nblk):
                cols = slice(i * width, (i + 1) * width)
                logits = _dot(live["h"], _unpack_rows(wg_refs[branch][:, cols]))
                parts.append(jax.nn.sigmoid(logits + bg_ref[:, branch * D_MODEL + i * width:
                                                            branch * D_MODEL + (i + 1) * width]))
                if between:
                    between[i]()
            live["gate"] = jnp.concatenate(parts, axis=-1)
        return run

    def project(branch):
        def run(between=()):
            nblk = max(len(between), 1)
            width = D_MODEL // nblk
            parts = []
            for i in range(nblk):
                cols = slice(i * width, (i + 1) * width)
                parts.append(live["gate"][:, cols] * _dot(acts[branch](), _unpack_rows(w_refs[branch][:, cols])))
                if between:
                    between[i]()
            y_branch = jnp.concatenate(parts, axis=-1)
            live["merged"] = y_branch if branch == 0 else live["merged"] + y_branch
        return run

    def out_project():
        live["y"] = x_ref[...] + _dot(_bf(live["merged"]), _unpack_rows(wout_ref[...]))

    def finish():
        y_ref[...] = _rms(live["y"], fg_ref[...])

    stages = []
    for branch in range(N_BRANCH):
        stages += [gate(branch), project(branch)]
    return stages + [out_project, finish]


def _sample_stages(z_ref, cos_ref, sin_ref, state_ref, cstate_ref, mk_ref, mv_ref, rng_ref, cw_ref, cbias_ref,
                   store_act, nstate_ref, nconv_ref, cbuf, inner_s, qdec_s, kdec_s, dec_ref):
    group, seq, _ = z_ref.shape
    pairs = [(g, hd) for g in range(group) for hd in range(HEADS)]
    live = {}

    def col(g, off, width=HEAD_DIM):
        return z_ref[g, :, off:off + width]

    def ret_inputs():
        cos = cos_ref[...]
        sin = sin_ref[...]
        live["q"] = {p: _bf(_rope(col(p[0], OFF_RQ + p[1] * HEAD_DIM), cos, sin)) for p in pairs}
        live["k"] = {p: _rope(col(p[0], OFF_RK + p[1] * HEAD_DIM), cos, sin) * QK_SCALE for p in pairs}
        live["v"] = {p: _bf(col(p[0], OFF_RV + p[1] * HEAD_DIM)) for p in pairs}

    def ret_products():
        live["scores"], live["cross"] = {}, {}
        for p in pairs:
            g, hd = p
            state = state_ref[g, hd]
            k = live["k"][p]
            live["scores"][p] = _bf(_dot_nt(live["q"][p], _bf(k)) * inner_s[hd])
            live["cross"][p] = _dot(live["q"][p], _bf(state))
            nstate_ref[g, hd] = (state * dec_ref[HEADS + hd]
                                 + _dot_tn(_bf(k * _lanes_twice(kdec_s[hd])), live["v"][p]))

    def ret_outputs():
        for p in pairs:
            g, hd = p
            lo = hd * HEAD_DIM
            o = _dot(live["scores"][p], live["v"][p]) + live["cross"][p] * _lanes_twice(qdec_s[hd])
            o = _rms(o, rng_ref[:, lo:lo + HEAD_DIM])
            store_act(0, g, slice(lo, lo + HEAD_DIM), o * _silu(col(g, OFF_RG + lo)))

    def conv():
        for g in range(group):
            base = g * (SUBLANES + seq)
            cbuf[base:base + SUBLANES, :] = jnp.zeros((SUBLANES, D_MODEL), F32)
            cbuf[base + SUBLANES - (CONV_WIDTH - 1):base + SUBLANES, :] = cstate_ref[g]
            pre, a_conv = _conv_branch(col(g, OFF_CU, D_MODEL), col(g, OFF_CB, D_MODEL), col(g, OFF_CC, D_MODEL),
                                       col(g, OFF_CG, D_MODEL), cbuf.at[base:base + SUBLANES + seq], cw_ref,
                                       cbias_ref, seq, slice(0, D_MODEL))
            nconv_ref[g] = pre[seq - (CONV_WIDTH - 1):, :]
            store_act(1, g, slice(0, D_MODEL), a_conv)

    def mem_score_piece(g, j):
        def run():
            if j == 0:
                live.setdefault("ms", {})[g] = []
                live.setdefault("mq", {})[g] = _bf(
                    jnp.concatenate([col(g, OFF_MQ + hd * HEAD_DIM) for hd in range(HEADS)], axis=0))
            keys = mk_ref[g, j * MEM_PIECE:(j + 1) * MEM_PIECE].reshape(MEM_PIECE * HEADS, HEAD_DIM)
            live["ms"][g].append(_dot_nt(live["mq"][g], _bf(keys)) * QK_SCALE)
        return run

    def mem_probs():
        live["mp"] = []
        for g in range(group):
            s = jnp.concatenate(live["ms"][g], axis=-1)
            row_head = lax.broadcasted_iota(jnp.int32, s.shape, 0) // seq
            col_head = lax.broadcasted_iota(jnp.int32, s.shape, 1) % HEADS
            live["mp"].append(_bf(_softmax_rows(jnp.where(row_head == col_head, s, -jnp.inf))))

    def mem_outputs():
        for g in range(group):
            mv2d = _bf(mv_ref[g].reshape(MEM_LEN * HEADS, HEAD_DIM))
            om_all = _dot(live["mp"][g], mv2d)
            for hd in range(HEADS):
                lo = hd * HEAD_DIM
                om = om_all[hd * seq:(hd + 1) * seq, :]
                store_act(2, g, slice(lo, lo + HEAD_DIM), om * _silu(col(g, OFF_MG + lo)))

    pieces = [[mem_score_piece(g, j) for j in range(MEM_LEN // MEM_PIECE)] for g in range(group)]
    assert group == 2
    return [(ret_inputs, ()), (ret_products, ()), (None, pieces[0]), (ret_outputs, pieces[1]),
            (mem_probs, ()), (conv, ()), (mem_outputs, ())]


def _tail_with_sample_body(x_ref, aret_ref, aconv_ref, amem_ref, ng_ref, win_hbm, bg_ref,
                           wro_hbm, wco_hbm, wmo_hbm, wout_hbm, fg_ref,
                           dec_ref, z_ref, cos_ref, sin_ref, state_ref, cstate_ref, mk_ref, mv_ref, rng_ref,
                           cw_ref, cbias_ref, xs_ref,
                           y_ref, nstate_ref, nconv_ref, ys_ref,
                           cbuf, inner_s, qdec_s, kdec_s, sact_s, w_s, w_sems):
    i = pl.program_id(0)
    group, seq, _ = z_ref.shape
    steps_per_tile = xs_ref.shape[0] // (group * seq)
    slot = lax.rem(i, steps_per_tile)

    def gate_cols(branch):
        return win_hbm.at[:, pl.ds(OFF_GATE + branch * D_MODEL, D_MODEL)]

    sources = (gate_cols(0), wro_hbm, gate_cols(1), wco_hbm, gate_cols(2), wmo_hbm, wout_hbm)
    copies = [pltpu.make_async_copy(src, w_s.at[k], w_sems.at[k]) for k, src in enumerate(sources)]
    weights = ((w_s.at[0], w_s.at[2], w_s.at[4]), bg_ref, w_s.at[1], w_s.at[3], w_s.at[5], w_s.at[6], fg_ref)

    def store_act(branch, g, cols, value):
        row = pl.multiple_of((slot * group + g) * seq, seq)
        sact_s[branch, pl.ds(row, seq), cols] = value

    def run(first):
        prompt_acts = [lambda ref=ref: ref[...] for ref in (aret_ref, aconv_ref, amem_ref)]
        tail = _tail_stages(x_ref, prompt_acts, ng_ref, *weights, y_ref)
        sample = _sample_stages(z_ref, cos_ref, sin_ref, state_ref, cstate_ref, mk_ref, mv_ref, rng_ref, cw_ref,
                                cbias_ref, store_act, nstate_ref, nconv_ref, cbuf, inner_s, qdec_s, kdec_s,
                                dec_ref)
        assert len(sample) == len(tail) - 1 and len(copies) == len(tail) - 1
        for j, tail_stage in enumerate(tail):
            before, between = (sample[j] if j < len(sample) - 1 else
                               sample[-1] if j == len(tail) - 1 else (None, ()))
            if before is not None:
                before()
            if first and j < len(copies):
                copies[j].wait()
            if between:
                tail_stage(between=between)
            else:
                tail_stage()

    @pl.when(i == 0)
    def _():
        for copy in copies:
            copy.start()
        _fill_decay(dec_ref, inner_s, qdec_s, kdec_s, seq)
        run(True)

    @pl.when(i > 0)
    def _():
        run(False)

    @pl.when(slot == steps_per_tile - 1)
    def _():
        sample_acts = [lambda branch=branch: _bf(sact_s[branch]) for branch in range(N_BRANCH)]
        for stage in _tail_stages(xs_ref, sample_acts, ng_ref, *weights, ys_ref):
            stage()


def _tail_specs(tile, ng, bg, wro, wco, wmo, wout, fg):
    tok = pl.BlockSpec((tile, D_MODEL), lambda i: (i, 0))

    in_hbm = pl.BlockSpec(memory_space=pl.ANY)
    in_specs = [tok, tok, tok, tok,
                _resident(ng.shape), in_hbm, _resident(bg.shape), in_hbm, in_hbm, in_hbm, in_hbm,
                _resident(fg.shape)]
    return tok, in_specs


def _tail_with_sample(x2d, aret, aconv, amem, ng, win_pk, bg, wro, wco, wmo, wout, fg,
                      dec, z, cos, sin, state, cstate, mk, mv, rng, cw, cbias, xs2d):
    steps = x2d.shape[0] // TAIL_TILE
    nb, seq, _ = z.shape
    grp = nb // steps
    assert grp * steps == nb and TAIL_TILE % (grp * seq) == 0
    steps_per_tile = TAIL_TILE // (grp * seq)
    tok, in_specs = _tail_specs(TAIL_TILE, ng, bg, wro, wco, wmo, wout, fg)
    sample_tok = pl.BlockSpec((TAIL_TILE, D_MODEL), lambda i: (i // steps_per_tile, 0))

    def per_req(*tail):
        zeros = (0,) * len(tail)
        return pl.BlockSpec((grp,) + tail, lambda i: (i,) + zeros)

    def const(shape):
        zeros = (0,) * len(shape)
        return pl.BlockSpec(shape, lambda i: zeros)

    return pl.pallas_call(
        _tail_with_sample_body,
        grid=(steps,),
        in_specs=in_specs + [pl.BlockSpec(memory_space=pltpu.SMEM),
                             per_req(seq, OFF_GATE), const(cos.shape), const(sin.shape),
                             per_req(HEADS, HEAD_DIM, HEAD_DIM), per_req(CONV_WIDTH - 1, D_MODEL),
                             per_req(MEM_LEN, HEADS, HEAD_DIM), per_req(MEM_LEN, HEADS, HEAD_DIM),
                             const(rng.shape), const(cw.shape), const(cbias.shape), sample_tok],
        out_specs=[tok, per_req(HEADS, HEAD_DIM, HEAD_DIM), per_req(CONV_WIDTH - 1, D_MODEL), sample_tok],
        out_shape=[jax.ShapeDtypeStruct(x2d.shape, F32),
                   jax.ShapeDtypeStruct(state.shape, F32),
                   jax.ShapeDtypeStruct(cstate.shape, F32),
                   jax.ShapeDtypeStruct(xs2d.shape, F32)],
        scratch_shapes=[pltpu.VMEM((grp * (SUBLANES + seq), D_MODEL), F32),
                        pltpu.VMEM((HEADS, seq, seq), F32),
                        pltpu.VMEM((HEADS, seq, ROT_HALF), F32),
                        pltpu.VMEM((HEADS, seq, ROT_HALF), F32),
                        pltpu.VMEM((N_BRANCH, TAIL_TILE, D_MODEL), F32),
                        pltpu.VMEM((2 * N_BRANCH + 1,) + wout.shape, wout.dtype),
                        pltpu.SemaphoreType.DMA((2 * N_BRANCH + 1,))],
        compiler_params=pltpu.CompilerParams(dimension_semantics=("arbitrary",),
                                             vmem_limit_bytes=VMEM_LIMIT_BYTES),
        name="tail_with_sample",
    )(x2d, aret, aconv, amem, ng, win_pk, bg, wro, wco, wmo, wout, fg,
      dec, z, cos, sin, state, cstate, mk, mv, rng, cw, cbias, xs2d)


def _rope_tables(pos):
    inv = ROPE_BASE ** (-jnp.arange(ROT_HALF, dtype=F32) / ROT_HALF)
    ang = pos.astype(F32)[:, None] * inv[None, :]
    return jnp.cos(ang), jnp.sin(ang)


def _rope_tables_range(n, stride=64):
    cos_hi, sin_hi = _rope_tables(stride * jnp.arange(n // stride, dtype=jnp.int32))
    cos_lo, sin_lo = _rope_tables(jnp.arange(stride, dtype=jnp.int32))
    cos = cos_hi[:, None, :] * cos_lo[None] - sin_hi[:, None, :] * sin_lo[None]
    sin = sin_hi[:, None, :] * cos_lo[None] + cos_hi[:, None, :] * sin_lo[None]
    return cos.reshape(n, ROT_HALF), sin.reshape(n, ROT_HALF)


def _decay_scalars(chunk):
    lg = jnp.log1p(-jnp.exp2(-5.0 - jnp.arange(HEADS, dtype=F32)))
    return jnp.concatenate([lg, jnp.exp(lg * chunk)])


def kernel(x_prompt, x_sample, state_ret, state_conv, cache_mem_k, cache_mem_v, mem_prompt, norm_g, w_in, b_gate, ret_norm_g, conv_w, conv_b, w_ret_o, w_conv_o, w_mem_o, w_out, mem_norm_g, w_mem_kv, final_norm_g):
    assert norm_g.shape[0] == 1, "single-layer trunk"
    nbp, seq_p, _ = x_prompt.shape
    nbs, seq_s, _ = x_sample.shape
    assert seq_p % PROMPT_TILE == 0 and PROMPT_TILE % RET_CHUNK == 0 and (nbp * seq_p) % TAIL_TILE == 0
    assert seq_s == SUBLANES and (nbs * seq_s) % TAIL_TILE == 0

    ng = norm_g[0][None, :]
    bg = b_gate[0][None, :]
    rng = ret_norm_g[0].reshape(1, D_MODEL)
    cw = conv_w[0]
    cbias = conv_b[0][None, :]
    wro, wco, wmo, wout = _pack_weights(w_ret_o[0], w_conv_o[0], w_mem_o[0], w_out[0])
    fg = final_norm_g[None, :]

    cos_p, sin_p = _rope_tables_range(seq_p)
    cos_s, sin_s = _rope_tables(PAST_LEN + jnp.arange(seq_s, dtype=jnp.int32))

    xs2d = x_sample.reshape(nbs * seq_s, D_MODEL)
    z, win_pk, mk, mv, mk_bf, mv_bf = _prep(xs2d, ng, w_in[0], mem_prompt, mem_norm_g[0][None, :],
                                            w_mem_kv[0])
    z = z.reshape(nbs, seq_s, OFF_GATE)
    tail_weights = (ng, win_pk, bg, wro, wco, wmo, wout, fg)

    aret_p, aconv_p, amem_p, ret_p, conv_p = _prompt_branches(
        x_prompt, _decay_scalars(RET_CHUNK), cos_p, sin_p, mk_bf, mv_bf, ng, win_pk, rng, cw, cbias)

    flat_p = lambda a: a.reshape(nbp * seq_p, D_MODEL)
    y_prompt, ret_s, conv_s, y_sample = _tail_with_sample(
        flat_p(x_prompt), flat_p(aret_p), flat_p(aconv_p), flat_p(amem_p), *tail_weights,
        _decay_scalars(seq_s), z, cos_s, sin_s, state_ret[0], state_conv[0], cache_mem_k[0], cache_mem_v[0],
        rng, cw, cbias, xs2d)

    return (y_prompt.reshape(x_prompt.shape), y_sample.reshape(x_sample.shape),
            ret_p[None], ret_s[None], conv_p[None], conv_s[None], mk[None], mv[None])
```

```python
import functools

import jax
import jax.numpy as jnp
from jax import lax
from jax.experimental import pallas as pl
from jax.experimental.pallas import tpu as pltpu

F32 = jnp.float32
BF16 = jnp.bfloat16

D_MODEL = 1024
HEADS = 4
HEAD_DIM = D_MODEL // HEADS
ROT_HALF = HEAD_DIM // 2
ROPE_BASE = 10000.0
CONV_WIDTH = 3
MEM_LEN = 256
MEM_PIECE = 64
N_BRANCH = 3
EPS = 1e-6
PAST_LEN = 16384
QK_SCALE = HEAD_DIM ** -0.5

OFF_RQ, OFF_RK, OFF_RV, OFF_RG = 0, 1024, 2048, 3072
OFF_CU, OFF_CB, OFF_CC, OFF_CG = 4096, 5120, 6144, 7168
OFF_MQ, OFF_MG = 8192, 9216
OFF_GATE = 10240
IN_TOTAL = OFF_GATE + N_BRANCH * D_MODEL

SUBLANES = 8
VMEM_LIMIT_BYTES = 60 * 1024 * 1024

PROMPT_TILE = 512
RET_CHUNK = 256
TAIL_TILE = 256
PACK_STEPS = 8
BF16_ROWS = 16


def _bf(x):
    return x.astype(BF16)


def _dot(a, b):
    return jnp.dot(a, b, preferred_element_type=F32)


def _dot_nt(a, b):
    return lax.dot_general(a, b, (((1,), (1,)), ((), ())), preferred_element_type=F32)


def _dot_tn(a, b):
    return lax.dot_general(a, b, (((0,), (0,)), ((), ())), preferred_element_type=F32)


def _rms(x, g):
    return x * lax.rsqrt(jnp.mean(x * x, axis=-1, keepdims=True) + EPS) * g


def _silu(x):
    return x * jax.nn.sigmoid(x)


def _rope(x, cos, sin):
    x1 = x[:, :ROT_HALF]
    x2 = x[:, ROT_HALF:]
    return jnp.concatenate([x1 * cos - x2 * sin, x1 * sin + x2 * cos], axis=-1)


def _lanes_twice(t):
    return jnp.concatenate([t, t], axis=-1)


def _pack_rows(w_bf):
    return pltpu.bitcast(w_bf, jnp.int32)


def _unpack_rows(w_packed):
    return pltpu.bitcast(w_packed, BF16)


def _fill_decay(dec_ref, inner_s, qdec_s, kdec_s, chunk):
    i = lax.broadcasted_iota(jnp.int32, (chunk, chunk), 0)
    j = lax.broadcasted_iota(jnp.int32, (chunk, chunk), 1)
    diff = (i - j).astype(F32)
    r = lax.broadcasted_iota(jnp.int32, (chunk, ROT_HALF), 0).astype(F32)
    for hd in range(HEADS):
        lg = dec_ref[hd]
        inner_s[hd] = jnp.where(diff >= 0, jnp.exp(lg * jnp.maximum(diff, 0.0)), 0.0)
        qdec_s[hd] = jnp.exp(lg * (r + 1.0))
        kdec_s[hd] = jnp.exp(lg * (chunk - 1.0 - r))


def _conv_branch(cu, cb, cc, cg, cbuf, cw_ref, cbias_ref, rows, cols):
    pre = cc * cu
    cbuf[SUBLANES:SUBLANES + rows, cols] = pre
    conv = (cbias_ref[:, cols]
            + cw_ref[0:1, cols] * cbuf[SUBLANES - 2:SUBLANES - 2 + rows, cols]
            + cw_ref[1:2, cols] * cbuf[SUBLANES - 1:SUBLANES - 1 + rows, cols]
            + cw_ref[2:3, cols] * pre)
    return pre, cb * conv * _silu(cg)


def _softmax_rows(s):
    e = jnp.exp(s - jnp.max(s, axis=-1, keepdims=True))
    return e * (1.0 / jnp.sum(e, axis=-1, keepdims=True))


def _resident(shape, block=None):
    zeros = (0,) * len(shape)
    return pl.BlockSpec(block or shape, lambda *_: zeros, pipeline_mode=pl.Buffered(1))


def _pack_weights_body(*refs):
    n = len(refs) // 2
    for w_ref, out_ref in zip(refs[:n], refs[n:]):
        out_ref[...] = _pack_rows(_bf(w_ref[...]))


def _pack_weights(*ws):
    assert all(w.shape == ws[0].shape for w in ws) and ws[0].shape[0] % (PACK_STEPS * 2 * BF16_ROWS) == 0
    rows, cols = ws[0].shape
    return pl.pallas_call(
        _pack_weights_body,
        grid=(PACK_STEPS,),
        in_specs=[pl.BlockSpec((rows // PACK_STEPS, cols), lambda i: (i, 0)) for _ in ws],
        out_specs=[pl.BlockSpec((rows // (2 * PACK_STEPS), cols), lambda i: (i, 0)) for _ in ws],
        out_shape=[jax.ShapeDtypeStruct((w.shape[0] // 2, w.shape[1]), jnp.int32) for w in ws],
        compiler_params=pltpu.CompilerParams(dimension_semantics=("arbitrary",),
                                             vmem_limit_bytes=VMEM_LIMIT_BYTES),
        name="pack_weights",
    )(*ws)


def _prompt_branch_body(dec_ref, x_ref, cos_ref, sin_ref, mk_ref, mv_ref, ng_ref, win_ref, rng_ref,
                        cw_ref, cbias_ref,
                        aret_ref, aconv_ref, amem_ref, state_ref, nconv_ref,
                        cbuf, inner_s, qdec_s, kdec_s):
    b = pl.program_id(0)
    c = pl.program_id(1)
    tile = x_ref.shape[0]

    @pl.when(jnp.logical_and(b == 0, c == 0))
    def _():
        _fill_decay(dec_ref, inner_s, qdec_s, kdec_s, RET_CHUNK)

    @pl.when(c == 0)
    def _():
        state_ref[...] = jnp.zeros(state_ref.shape, F32)
        cbuf[0:SUBLANES, :] = jnp.zeros((SUBLANES, D_MODEL), F32)

    h = _bf(_rms(x_ref[...], ng_ref[...]))
    cos = cos_ref[...]
    sin = sin_ref[...]

    def proj(off):
        return _dot(h, _unpack_rows(win_ref[:, off:off + HEAD_DIM]))

    heads = range(HEADS)
    head_cols = [slice(hd * HEAD_DIM, (hd + 1) * HEAD_DIM) for hd in heads]

    def conv_block(hd):
        lo = hd * HEAD_DIM
        cols = head_cols[hd]
        pre, a_conv = _conv_branch(proj(OFF_CU + lo), proj(OFF_CB + lo), proj(OFF_CC + lo), proj(OFF_CG + lo),
                                   cbuf, cw_ref, cbias_ref, tile, cols)
        cbuf[0:SUBLANES, cols] = pre[tile - SUBLANES:, :]
        nconv_ref[:, cols] = pre[tile - (CONV_WIDTH - 1):, :]
        aconv_ref[:, cols] = _bf(a_conv)

    q = [_bf(_rope(proj(OFF_RQ + hd * HEAD_DIM), cos, sin)) for hd in heads]
    k = [_rope(proj(OFF_RK + hd * HEAD_DIM), cos, sin) * QK_SCALE for hd in heads]
    v = [_bf(proj(OFF_RV + hd * HEAD_DIM)) for hd in heads]
    chunks = [slice(j * RET_CHUNK, (j + 1) * RET_CHUNK) for j in range(tile // RET_CHUNK)]
    scores, cross = [], []
    for rows in chunks:
        for hd in heads:
            state = state_ref[hd]
            kc = k[hd][rows]
            scores.append(_bf(_dot_nt(q[hd][rows], _bf(kc)) * inner_s[hd]))
            cross.append(_dot(q[hd][rows], _bf(state)))
            state_ref[hd] = (state * dec_ref[HEADS + hd]
                             + _dot_tn(_bf(kc * _lanes_twice(kdec_s[hd])), v[hd][rows]))
    conv_block(0)
    ret_o = []
    for hd in heads:
        qdec = _lanes_twice(qdec_s[hd])
        ret_o.append(jnp.concatenate(
            [_dot(scores[j * HEADS + hd], v[hd][rows]) + cross[j * HEADS + hd] * qdec
             for j, rows in enumerate(chunks)], axis=0))
    mq = [_bf(proj(OFF_MQ + hd * HEAD_DIM)) for hd in heads]
    conv_block(1)
    mscores = [_dot_nt(mq[hd], mk_ref[:, head_cols[hd]]) * QK_SCALE for hd in heads]
    for hd in heads:
        cols = head_cols[hd]
        aret_ref[:, cols] = _bf(_rms(ret_o[hd], rng_ref[:, cols]) * _silu(proj(OFF_RG + hd * HEAD_DIM)))
    conv_block(2)
    probs = [_bf(_softmax_rows(mscores[hd])) for hd in heads]
    conv_block(3)
    for hd in heads:
        cols = head_cols[hd]
        om = _dot(probs[hd], mv_ref[:, cols])
        amem_ref[:, cols] = _bf(om * _silu(proj(OFF_MG + hd * HEAD_DIM)))


def _prompt_branches(x, dec, cos, sin, mk_bf, mv_bf, ng, win_pk, rng, cw, cbias):
    nb, seq, _ = x.shape
    tile = PROMPT_TILE
    tok = pl.BlockSpec((None, tile, D_MODEL), lambda b, c: (b, c, 0))
    rot = pl.BlockSpec((tile, ROT_HALF), lambda b, c: (c, 0))
    mem = pl.BlockSpec((None, MEM_LEN, D_MODEL), lambda b, c: (b, 0, 0))
    act = jax.ShapeDtypeStruct(x.shape, BF16)
    return pl.pallas_call(
        _prompt_branch_body,
        grid=(nb, seq // tile),
        in_specs=[pl.BlockSpec(memory_space=pltpu.SMEM), tok, rot, rot, mem, mem,
                  _resident(ng.shape), _resident(win_pk.shape, (D_MODEL // 2, OFF_GATE)), _resident(rng.shape),
                  _resident(cw.shape), _resident(cbias.shape)],
        out_specs=[tok, tok, tok,
                   pl.BlockSpec((None, HEADS, HEAD_DIM, HEAD_DIM), lambda b, c: (b, 0, 0, 0)),
                   pl.BlockSpec((None, CONV_WIDTH - 1, D_MODEL), lambda b, c: (b, 0, 0))],
        out_shape=[act, act, act,
                   jax.ShapeDtypeStruct((nb, HEADS, HEAD_DIM, HEAD_DIM), F32),
                   jax.ShapeDtypeStruct((nb, CONV_WIDTH - 1, D_MODEL), F32)],
        scratch_shapes=[pltpu.VMEM((SUBLANES + tile, D_MODEL), F32),
                        pltpu.VMEM((HEADS, RET_CHUNK, RET_CHUNK), F32),
                        pltpu.VMEM((HEADS, RET_CHUNK, ROT_HALF), F32),
                        pltpu.VMEM((HEADS, RET_CHUNK, ROT_HALF), F32)],
        compiler_params=pltpu.CompilerParams(dimension_semantics=("arbitrary", "arbitrary"),
                                             vmem_limit_bytes=VMEM_LIMIT_BYTES),
        name="prompt_branches",
    )(dec, x, cos, sin, mk_bf, mv_bf, ng, win_pk, rng, cw, cbias)


def _prep_body(x_ref, ng_ref, w_ref, mem_ref, mg_ref, wkv_ref,
               z_ref, wpk_ref, k_ref, v_ref, kb_ref, vb_ref, h_s, wkv_s, *, n_mem):
    n = pl.program_id(0)

    @pl.when(n == 0)
    def _():
        h_s[...] = _bf(_rms(x_ref[...], ng_ref[...]))
        wkv_s[...] = _bf(wkv_ref[...])

    w = _bf(w_ref[...])
    wpk_ref[...] = _pack_rows(w)

    @pl.when(n < OFF_GATE // w_ref.shape[1])
    def _():
        z_ref[...] = _dot(h_s[...], w)

    @pl.when(n < n_mem)
    def _():
        kv = _dot(_bf(_rms(mem_ref[...], mg_ref[...])), wkv_s[...])
        k = kv[:, :D_MODEL]
        v = kv[:, D_MODEL:]
        for hd in range(HEADS):
            lo = hd * HEAD_DIM
            k_ref[:, hd, :] = k[:, lo:lo + HEAD_DIM]
            v_ref[:, hd, :] = v[:, lo:lo + HEAD_DIM]
        kb_ref[...] = _bf(k)
        vb_ref[...] = _bf(v)


def _prep(x2d, ng, w_in, mem, mem_g, w_mem_kv):
    ntok = x2d.shape[0]
    n_mem = mem.shape[0]
    blk = D_MODEL
    steps = IN_TOTAL // blk
    z_blocks = OFF_GATE // blk
    assert n_mem <= steps

    def mem_blk(*tail):
        zeros = (0,) * len(tail)
        return pl.BlockSpec((None,) + tail, lambda n: (jnp.minimum(n, n_mem - 1),) + zeros)

    heads_shape = jax.ShapeDtypeStruct((n_mem, MEM_LEN, HEADS, HEAD_DIM), F32)
    return pl.pallas_call(
        functools.partial(_prep_body, n_mem=n_mem),
        grid=(steps,),
        in_specs=[_resident(x2d.shape), _resident(ng.shape),
                  pl.BlockSpec((D_MODEL, blk), lambda n: (0, n)),
                  mem_blk(MEM_LEN, D_MODEL), _resident(mem_g.shape), _resident(w_mem_kv.shape)],
        out_specs=[pl.BlockSpec((ntok, blk), lambda n: (0, jnp.minimum(n, z_blocks - 1))),
                   pl.BlockSpec((D_MODEL // 2, blk), lambda n: (0, n)),
                   mem_blk(MEM_LEN, HEADS, HEAD_DIM), mem_blk(MEM_LEN, HEADS, HEAD_DIM),
                   mem_blk(MEM_LEN, D_MODEL), mem_blk(MEM_LEN, D_MODEL)],
        out_shape=[jax.ShapeDtypeStruct((ntok, OFF_GATE), F32),
                   jax.ShapeDtypeStruct((D_MODEL // 2, IN_TOTAL), jnp.int32),
                   heads_shape, heads_shape,
                   jax.ShapeDtypeStruct(mem.shape, BF16), jax.ShapeDtypeStruct(mem.shape, BF16)],
        scratch_shapes=[pltpu.VMEM(x2d.shape, BF16), pltpu.VMEM(w_mem_kv.shape, BF16)],
        compiler_params=pltpu.CompilerParams(dimension_semantics=("arbitrary",),
                                             vmem_limit_bytes=VMEM_LIMIT_BYTES),
        name="prep",
    )(x2d, ng, w_in, mem, mem_g, w_mem_kv)


def _tail_stages(x_ref, acts, ng_ref, wg_refs, bg_ref, wro_ref, wco_ref, wmo_ref, wout_ref, fg_ref, y_ref):
    live = {}
    w_refs = (wro_ref, wco_ref, wmo_ref)

    def gate(branch):
        def run(between=()):
            if branch == 0:
                live["h"] = _bf(_rms(x_ref[...], ng_ref[...]))
            nblk = max(len(between), 1)
            width = D_MODEL // nblk
            parts = []
            for i in range(nblk):
                cols = slice(i * width, (i + 1) * width)
                logits = _dot(live["h"], _unpack_rows(wg_refs[branch][:, cols]))
                parts.append(jax.nn.sigmoid(logits + bg_ref[:, branch * D_MODEL + i * width:
                                                            branch * D_MODEL + (i + 1) * width]))
                if between:
                    between[i]()
            live["gate"] = jnp.concatenate(parts, axis=-1)
        return run

    def project(branch):
        def run(between=()):
            nblk = max(len(between), 1)
            width = D_MODEL // nblk
            parts = []
            for i in range(nblk):
                cols = slice(i * width, (i + 1) * width)
                parts.append(live["gate"][:, cols] * _dot(acts[branch](), _unpack_rows(w_refs[branch][:, cols])))
                if between:
                    between[i]()
            y_branch = jnp.concatenate(parts, axis=-1)
            live["merged"] = y_branch if branch == 0 else live["merged"] + y_branch
        return run

    def out_project():
        live["y"] = x_ref[...] + _dot(_bf(live["merged"]), _unpack_rows(wout_ref[...]))

    def finish():
        y_ref[...] = _rms(live["y"], fg_ref[...])

    stages = []
    for branch in range(N_BRANCH):
        stages += [gate(branch), project(branch)]
    return stages + [out_project, finish]


def _sample_stages(z_ref, cos_ref, sin_ref, state_ref, cstate_ref, mk_ref, mv_ref, rng_ref, cw_ref, cbias_ref,
                   store_act, nstate_ref, nconv_ref, cbuf, inner_s, qdec_s, kdec_s, dec_ref):
    group, seq, _ = z_ref.shape
    pairs = [(g, hd) for g in range(group) for hd in range(HEADS)]
    live = {}

    def col(g, off, width=HEAD_DIM):
        return z_ref[g, :, off:off + width]

    def ret_inputs():
        cos = cos_ref[...]
        sin = sin_ref[...]
        live["q"] = {p: _bf(_rope(col(p[0], OFF_RQ + p[1] * HEAD_DIM), cos, sin)) for p in pairs}
        live["k"] = {p: _rope(col(p[0], OFF_RK + p[1] * HEAD_DIM), cos, sin) * QK_SCALE for p in pairs}
        live["v"] = {p: _bf(col(p[0], OFF_RV + p[1] * HEAD_DIM)) for p in pairs}

    def ret_products():
        live["scores"], live["cross"] = {}, {}
        for p in pairs:
            g, hd = p
            state = state_ref[g, hd]
            k = live["k"][p]
            live["scores"][p] = _bf(_dot_nt(live["q"][p], _bf(k)) * inner_s[hd])
            live["cross"][p] = _dot(live["q"][p], _bf(state))
            nstate_ref[g, hd] = (state * dec_ref[HEADS + hd]
                                 + _dot_tn(_bf(k * _lanes_twice(kdec_s[hd])), live["v"][p]))

    def ret_outputs():
        for p in pairs:
            g, hd = p
            lo = hd * HEAD_DIM
            o = _dot(live["scores"][p], live["v"][p]) + live["cross"][p] * _lanes_twice(qdec_s[hd])
            o = _rms(o, rng_ref[:, lo:lo + HEAD_DIM])
            store_act(0, g, slice(lo, lo + HEAD_DIM), o * _silu(col(g, OFF_RG + lo)))

    def conv():
        for g in range(group):
            base = g * (SUBLANES + seq)
            cbuf[base:base + SUBLANES, :] = jnp.zeros((SUBLANES, D_MODEL), F32)
            cbuf[base + SUBLANES - (CONV_WIDTH - 1):base + SUBLANES, :] = cstate_ref[g]
            pre, a_conv = _conv_branch(col(g, OFF_CU, D_MODEL), col(g, OFF_CB, D_MODEL), col(g, OFF_CC, D_MODEL),
                                       col(g, OFF_CG, D_MODEL), cbuf.at[base:base + SUBLANES + seq], cw_ref,
                                       cbias_ref, seq, slice(0, D_MODEL))
            nconv_ref[g] = pre[seq - (CONV_WIDTH - 1):, :]
            store_act(1, g, slice(0, D_MODEL), a_conv)

    def mem_score_piece(g, j):
        def run():
            if j == 0:
                live.setdefault("ms", {})[g] = []
                live.setdefault("mq", {})[g] = _bf(
                    jnp.concatenate([col(g, OFF_MQ + hd * HEAD_DIM) for hd in range(HEADS)], axis=0))
            keys = mk_ref[g, j * MEM_PIECE:(j + 1) * MEM_PIECE].reshape(MEM_PIECE * HEADS, HEAD_DIM)
            live["ms"][g].append(_dot_nt(live["mq"][g], _bf(keys)) * QK_SCALE)
        return run

    def mem_probs():
        live["mp"] = []
        for g in range(group):
            s = jnp.concatenate(live["ms"][g], axis=-1)
            row_head = lax.broadcasted_iota(jnp.int32, s.shape, 0) // seq
            col_head = lax.broadcasted_iota(jnp.int32, s.shape, 1) % HEADS
            live["mp"].append(_bf(_softmax_rows(jnp.where(row_head == col_head, s, -jnp.inf))))

    def mem_outputs():
        for g in range(group):
            mv2d = _bf(mv_ref[g].reshape(MEM_LEN * HEADS, HEAD_DIM))
            om_all = _dot(live["mp"][g], mv2d)
            for hd in range(HEADS):
                lo = hd * HEAD_DIM
                om = om_all[hd * seq:(hd + 1) * seq, :]
                store_act(2, g, slice(lo, lo + HEAD_DIM), om * _silu(col(g, OFF_MG + lo)))

    pieces = [[mem_score_piece(g, j) for j in range(MEM_LEN // MEM_PIECE)] for g in range(group)]
    assert group == 2
    return [(ret_inputs, ()), (ret_products, ()), (None, pieces[0]), (ret_outputs, pieces[1]),
            (mem_probs, ()), (conv, ()), (mem_outputs, ())]


def _tail_with_sample_body(x_ref, aret_ref, aconv_ref, amem_ref, ng_ref, wg0_ref, wg1_ref, wg2_ref, bg_ref,
                           wro_ref, wco_ref, wmo_ref, wout_ref, fg_ref,
                           dec_ref, z_ref, cos_ref, sin_ref, state_ref, cstate_ref, mk_ref, mv_ref, rng_ref,
                           cw_ref, cbias_ref, xs_ref,
                           y_ref, nstate_ref, nconv_ref, ys_ref,
                           cbuf, inner_s, qdec_s, kdec_s, sact_s):
    i = pl.program_id(0)
    group, seq, _ = z_ref.shape
    steps_per_tile = xs_ref.shape[0] // (group * seq)
    slot = lax.rem(i, steps_per_tile)

    @pl.when(i == 0)
    def _():
        _fill_decay(dec_ref, inner_s, qdec_s, kdec_s, seq)

    def store_act(branch, g, cols, value):
        row = pl.multiple_of((slot * group + g) * seq, seq)
        sact_s[branch, pl.ds(row, seq), cols] = value

    weights = ((wg0_ref, wg1_ref, wg2_ref), bg_ref, wro_ref, wco_ref, wmo_ref, wout_ref, fg_ref)
    prompt_acts = [lambda ref=ref: ref[...] for ref in (aret_ref, aconv_ref, amem_ref)]
    tail = _tail_stages(x_ref, prompt_acts, ng_ref, *weights, y_ref)
    sample = _sample_stages(z_ref, cos_ref, sin_ref, state_ref, cstate_ref, mk_ref, mv_ref, rng_ref, cw_ref,
                            cbias_ref, store_act, nstate_ref, nconv_ref, cbuf, inner_s, qdec_s, kdec_s, dec_ref)
    assert len(sample) == len(tail) - 1
    for j, tail_stage in enumerate(tail):
        before, between = sample[j] if j < len(sample) - 1 else sample[-1] if j == len(tail) - 1 else (None, ())
        if before is not None:
            before()
        if between:
            tail_stage(between=between)
        else:
            tail_stage()

    @pl.when(slot == steps_per_tile - 1)
    def _():
        sample_acts = [lambda branch=branch: _bf(sact_s[branch]) for branch in range(N_BRANCH)]
        for stage in _tail_stages(xs_ref, sample_acts, ng_ref, *weights, ys_ref):
            stage()


def _tail_specs(tile, ng, bg, wro, wco, wmo, wout, fg):
    tok = pl.BlockSpec((tile, D_MODEL), lambda i: (i, 0))

    def gate_cols(branch):
        blk = OFF_GATE // D_MODEL + branch
        return pl.BlockSpec((D_MODEL // 2, D_MODEL), lambda i: (0, blk), pipeline_mode=pl.Buffered(1))

    in_specs = [tok, tok, tok, tok,
                _resident(ng.shape), gate_cols(0), gate_cols(1), gate_cols(2), _resident(bg.shape),
                _resident(wro.shape), _resident(wco.shape), _resident(wmo.shape), _resident(wout.shape),
                _resident(fg.shape)]
    return tok, in_specs


def _tail_with_sample(x2d, aret, aconv, amem, ng, win_pk, bg, wro, wco, wmo, wout, fg,
                      dec, z, cos, sin, state, cstate, mk, mv, rng, cw, cbias, xs2d):
    steps = x2d.shape[0] // TAIL_TILE
    nb, seq, _ = z.shape
    grp = nb // steps
    assert grp * steps == nb and TAIL_TILE % (grp * seq) == 0
    steps_per_tile = TAIL_TILE // (grp * seq)
    tok, in_specs = _tail_specs(TAIL_TILE, ng, bg, wro, wco, wmo, wout, fg)
    sample_tok = pl.BlockSpec((TAIL_TILE, D_MODEL), lambda i: (i // steps_per_tile, 0))

    def per_req(*tail):
        zeros = (0,) * len(tail)
        return pl.BlockSpec((grp,) + tail, lambda i: (i,) + zeros)

    def const(shape):
        zeros = (0,) * len(shape)
        return pl.BlockSpec(shape, lambda i: zeros)

    return pl.pallas_call(
        _tail_with_sample_body,
        grid=(steps,),
        in_specs=in_specs + [pl.BlockSpec(memory_space=pltpu.SMEM),
                             per_req(seq, OFF_GATE), const(cos.shape), const(sin.shape),
                             per_req(HEADS, HEAD_DIM, HEAD_DIM), per_req(CONV_WIDTH - 1, D_MODEL),
                             per_req(MEM_LEN, HEADS, HEAD_DIM), per_req(MEM_LEN, HEADS, HEAD_DIM),
                             const(rng.shape), const(cw.shape), const(cbias.shape), sample_tok],
        out_specs=[tok, per_req(HEADS, HEAD_DIM, HEAD_DIM), per_req(CONV_WIDTH - 1, D_MODEL), sample_tok],
        out_shape=[jax.ShapeDtypeStruct(x2d.shape, F32),
                   jax.ShapeDtypeStruct(state.shape, F32),
                   jax.ShapeDtypeStruct(cstate.shape, F32),
                   jax.ShapeDtypeStruct(xs2d.shape, F32)],
        scratch_shapes=[pltpu.VMEM((grp * (SUBLANES + seq), D_MODEL), F32),
                        pltpu.VMEM((HEADS, seq, seq), F32),
                        pltpu.VMEM((HEADS, seq, ROT_HALF), F32),
                        pltpu.VMEM((HEADS, seq, ROT_HALF), F32),
                        pltpu.VMEM((N_BRANCH, TAIL_TILE, D_MODEL), F32)],
        compiler_params=pltpu.CompilerParams(dimension_semantics=("arbitrary",),
                                             vmem_limit_bytes=VMEM_LIMIT_BYTES),
        name="tail_with_sample",
    )(x2d, aret, aconv, amem, ng, win_pk, win_pk, win_pk, bg, wro, wco, wmo, wout, fg,
      dec, z, cos, sin, state, cstate, mk, mv, rng, cw, cbias, xs2d)


def _rope_tables(pos):
    inv = ROPE_BASE ** (-jnp.arange(ROT_HALF, dtype=F32) / ROT_HALF)
    ang = pos.astype(F32)[:, None] * inv[None, :]
    return jnp.cos(ang), jnp.sin(ang)


def _rope_tables_range(n, stride=64):
    cos_hi, sin_hi = _rope_tables(stride * jnp.arange(n // stride, dtype=jnp.int32))
    cos_lo, sin_lo = _rope_tables(jnp.arange(stride, dtype=jnp.int32))
    cos = cos_hi[:, None, :] * cos_lo[None] - sin_hi[:, None, :] * sin_lo[None]
    sin = sin_hi[:, None, :] * cos_lo[None] + cos_hi[:, None, :] * sin_lo[None]
    return cos.reshape(n, ROT_HALF), sin.reshape(n, ROT_HALF)


def _decay_scalars(chunk):
    lg = jnp.log1p(-jnp.exp2(-5.0 - jnp.arange(HEADS, dtype=F32)))
    return jnp.concatenate([lg, jnp.exp(lg * chunk)])


def kernel(x_prompt, x_sample, state_ret, state_conv, cache_mem_k, cache_mem_v, mem_prompt, norm_g, w_in, b_gate, ret_norm_g, conv_w, conv_b, w_ret_o, w_conv_o, w_mem_o, w_out, mem_norm_g, w_mem_kv, final_norm_g):
    assert norm_g.shape[0] == 1, "single-layer trunk"
    nbp, seq_p, _ = x_prompt.shape
    nbs, seq_s, _ = x_sample.shape
    assert seq_p % PROMPT_TILE == 0 and PROMPT_TILE % RET_CHUNK == 0 and (nbp * seq_p) % TAIL_TILE == 0
    assert seq_s == SUBLANES and (nbs * seq_s) % TAIL_TILE == 0

    ng = norm_g[0][None, :]
    bg = b_gate[0][None, :]
    rng = ret_norm_g[0].reshape(1, D_MODEL)
    cw = conv_w[0]
    cbias = conv_b[0][None, :]
    wro, wco, wmo, wout = _pack_weights(w_ret_o[0], w_conv_o[0], w_mem_o[0], w_out[0])
    fg = final_norm_g[None, :]

    cos_p, sin_p = _rope_tables_range(seq_p)
    cos_s, sin_s = _rope_tables(PAST_LEN + jnp.arange(seq_s, dtype=jnp.int32))

    xs2d = x_sample.reshape(nbs * seq_s, D_MODEL)
    z, win_pk, mk, mv, mk_bf, mv_bf = _prep(xs2d, ng, w_in[0], mem_prompt, mem_norm_g[0][None, :],
                                            w_mem_kv[0])
    z = z.reshape(nbs, seq_s, OFF_GATE)
    tail_weights = (ng, win_pk, bg, wro, wco, wmo, wout, fg)

    aret_p, aconv_p, amem_p, ret_p, conv_p = _prompt_branches(
        x_prompt, _decay_scalars(RET_CHUNK), cos_p, sin_p, mk_bf, mv_bf, ng, win_pk, rng, cw, cbias)

    flat_p = lambda a: a.reshape(nbp * seq_p, D_MODEL)
    y_prompt, ret_s, conv_s, y_sample = _tail_with_sample(
        flat_p(x_prompt), flat_p(aret_p), flat_p(aconv_p), flat_p(amem_p), *tail_weights,
        _decay_scalars(seq_s), z, cos_s, sin_s, state_ret[0], state_conv[0], cache_mem_k[0], cache_mem_v[0],
        rng, cw, cbias, xs2d)

    return (y_prompt.reshape(x_prompt.shape), y_sample.reshape(x_sample.shape),
            ret_p[None], ret_s[None], conv_p[None], conv_s[None], mk[None], mv[None])
```

```python
import functools

import jax
import jax.numpy as jnp
from jax import lax
from jax.experimental import pallas as pl
from jax.experimental.pallas import tpu as pltpu

F32 = jnp.float32
BF16 = jnp.bfloat16

D_MODEL = 1024
HEADS = 4
HEAD_DIM = D_MODEL // HEADS
ROT_HALF = HEAD_DIM // 2
ROPE_BASE = 10000.0
CONV_WIDTH = 3
MEM_LEN = 256
MEM_PIECE = 64
N_BRANCH = 3
EPS = 1e-6
PAST_LEN = 16384
QK_SCALE = HEAD_DIM ** -0.5

OFF_RQ, OFF_RK, OFF_RV, OFF_RG = 0, 1024, 2048, 3072
OFF_CU, OFF_CB, OFF_CC, OFF_CG = 4096, 5120, 6144, 7168
OFF_MQ, OFF_MG = 8192, 9216
OFF_GATE = 10240
IN_TOTAL = OFF_GATE + N_BRANCH * D_MODEL

SUBLANES = 8
VMEM_LIMIT_BYTES = 60 * 1024 * 1024

PROMPT_TILE = 512
RET_CHUNK = 256
TAIL_TILE = 256


def _bf(x):
    return x.astype(BF16)


def _dot(a, b):
    return jnp.dot(a, b, preferred_element_type=F32)


def _dot_nt(a, b):
    return lax.dot_general(a, b, (((1,), (1,)), ((), ())), preferred_element_type=F32)


def _dot_tn(a, b):
    return lax.dot_general(a, b, (((0,), (0,)), ((), ())), preferred_element_type=F32)


def _rms(x, g):
    return x * lax.rsqrt(jnp.mean(x * x, axis=-1, keepdims=True) + EPS) * g


def _silu(x):
    return x * jax.nn.sigmoid(x)


def _rope(x, cos, sin):
    x1 = x[:, :ROT_HALF]
    x2 = x[:, ROT_HALF:]
    return jnp.concatenate([x1 * cos - x2 * sin, x1 * sin + x2 * cos], axis=-1)


def _lanes_twice(t):
    return jnp.concatenate([t, t], axis=-1)


def _pack_rows(w_bf):
    return pltpu.bitcast(w_bf, jnp.int32)


def _unpack_rows(w_packed):
    return pltpu.bitcast(w_packed, BF16)


def _fill_decay(dec_ref, inner_s, qdec_s, kdec_s, chunk):
    i = lax.broadcasted_iota(jnp.int32, (chunk, chunk), 0)
    j = lax.broadcasted_iota(jnp.int32, (chunk, chunk), 1)
    diff = (i - j).astype(F32)
    r = lax.broadcasted_iota(jnp.int32, (chunk, ROT_HALF), 0).astype(F32)
    for hd in range(HEADS):
        lg = dec_ref[hd]
        inner_s[hd] = jnp.where(diff >= 0, jnp.exp(lg * jnp.maximum(diff, 0.0)), 0.0)
        qdec_s[hd] = jnp.exp(lg * (r + 1.0))
        kdec_s[hd] = jnp.exp(lg * (chunk - 1.0 - r))


def _conv_branch(cu, cb, cc, cg, cbuf, cw_ref, cbias_ref, rows, cols):
    pre = cc * cu
    cbuf[SUBLANES:SUBLANES + rows, cols] = pre
    conv = (cbias_ref[:, cols]
            + cw_ref[0:1, cols] * cbuf[SUBLANES - 2:SUBLANES - 2 + rows, cols]
            + cw_ref[1:2, cols] * cbuf[SUBLANES - 1:SUBLANES - 1 + rows, cols]
            + cw_ref[2:3, cols] * pre)
    return pre, cb * conv * _silu(cg)


def _softmax_rows(s):
    e = jnp.exp(s - jnp.max(s, axis=-1, keepdims=True))
    return e * (1.0 / jnp.sum(e, axis=-1, keepdims=True))


def _resident(shape, block=None):
    zeros = (0,) * len(shape)
    return pl.BlockSpec(block or shape, lambda *_: zeros, pipeline_mode=pl.Buffered(1))


def _pack_weights_body(*refs):
    n = len(refs) // 2
    for w_ref, out_ref in zip(refs[:n], refs[n:]):
        out_ref[...] = _pack_rows(_bf(w_ref[...]))


def _pack_weights(*ws):
    return pl.pallas_call(
        _pack_weights_body,
        out_shape=[jax.ShapeDtypeStruct((w.shape[0] // 2, w.shape[1]), jnp.int32) for w in ws],
        compiler_params=pltpu.CompilerParams(vmem_limit_bytes=VMEM_LIMIT_BYTES),
        name="pack_weights",
    )(*ws)


def _prompt_branch_body(dec_ref, x_ref, cos_ref, sin_ref, mk_ref, mv_ref, ng_ref, win_ref, rng_ref,
                        cw_ref, cbias_ref,
                        aret_ref, aconv_ref, amem_ref, state_ref, nconv_ref,
                        cbuf, inner_s, qdec_s, kdec_s):
    b = pl.program_id(0)
    c = pl.program_id(1)
    tile = x_ref.shape[0]

    @pl.when(jnp.logical_and(b == 0, c == 0))
    def _():
        _fill_decay(dec_ref, inner_s, qdec_s, kdec_s, RET_CHUNK)

    @pl.when(c == 0)
    def _():
        state_ref[...] = jnp.zeros(state_ref.shape, F32)
        cbuf[0:SUBLANES, :] = jnp.zeros((SUBLANES, D_MODEL), F32)

    h = _bf(_rms(x_ref[...], ng_ref[...]))
    cos = cos_ref[...]
    sin = sin_ref[...]

    def proj(off):
        return _dot(h, _unpack_rows(win_ref[:, off:off + HEAD_DIM]))

    heads = range(HEADS)
    head_cols = [slice(hd * HEAD_DIM, (hd + 1) * HEAD_DIM) for hd in heads]

    def conv_block(hd):
        lo = hd * HEAD_DIM
        cols = head_cols[hd]
        pre, a_conv = _conv_branch(proj(OFF_CU + lo), proj(OFF_CB + lo), proj(OFF_CC + lo), proj(OFF_CG + lo),
                                   cbuf, cw_ref, cbias_ref, tile, cols)
        cbuf[0:SUBLANES, cols] = pre[tile - SUBLANES:, :]
        nconv_ref[:, cols] = pre[tile - (CONV_WIDTH - 1):, :]
        aconv_ref[:, cols] = _bf(a_conv)

    q = [_bf(_rope(proj(OFF_RQ + hd * HEAD_DIM), cos, sin)) for hd in heads]
    k = [_rope(proj(OFF_RK + hd * HEAD_DIM), cos, sin) * QK_SCALE for hd in heads]
    v = [_bf(proj(OFF_RV + hd * HEAD_DIM)) for hd in heads]
    chunks = [slice(j * RET_CHUNK, (j + 1) * RET_CHUNK) for j in range(tile // RET_CHUNK)]
    scores, cross = [], []
    for rows in chunks:
        for hd in heads:
            state = state_ref[hd]
            kc = k[hd][rows]
            scores.append(_bf(_dot_nt(q[hd][rows], _bf(kc)) * inner_s[hd]))
            cross.append(_dot(q[hd][rows], _bf(state)))
            state_ref[hd] = (state * dec_ref[HEADS + hd]
                             + _dot_tn(_bf(kc * _lanes_twice(kdec_s[hd])), v[hd][rows]))
    conv_block(0)
    ret_o = []
    for hd in heads:
        qdec = _lanes_twice(qdec_s[hd])
        ret_o.append(jnp.concatenate(
            [_dot(scores[j * HEADS + hd], v[hd][rows]) + cross[j * HEADS + hd] * qdec
             for j, rows in enumerate(chunks)], axis=0))
    mq = [_bf(proj(OFF_MQ + hd * HEAD_DIM)) for hd in heads]
    conv_block(1)
    mscores = [_dot_nt(mq[hd], mk_ref[:, head_cols[hd]]) * QK_SCALE for hd in heads]
    for hd in heads:
        cols = head_cols[hd]
        aret_ref[:, cols] = _bf(_rms(ret_o[hd], rng_ref[:, cols]) * _silu(proj(OFF_RG + hd * HEAD_DIM)))
    conv_block(2)
    probs = [_bf(_softmax_rows(mscores[hd])) for hd in heads]
    conv_block(3)
    for hd in heads:
        cols = head_cols[hd]
        om = _dot(probs[hd], mv_ref[:, cols])
        amem_ref[:, cols] = _bf(om * _silu(proj(OFF_MG + hd * HEAD_DIM)))


def _prompt_branches(x, dec, cos, sin, mk_bf, mv_bf, ng, win_pk, rng, cw, cbias):
    nb, seq, _ = x.shape
    tile = PROMPT_TILE
    tok = pl.BlockSpec((None, tile, D_MODEL), lambda b, c: (b, c, 0))
    rot = pl.BlockSpec((tile, ROT_HALF), lambda b, c: (c, 0))
    mem = pl.BlockSpec((None, MEM_LEN, D_MODEL), lambda b, c: (b, 0, 0))
    act = jax.ShapeDtypeStruct(x.shape, BF16)
    return pl.pallas_call(
        _prompt_branch_body,
        grid=(nb, seq // tile),
        in_specs=[pl.BlockSpec(memory_space=pltpu.SMEM), tok, rot, rot, mem, mem,
                  _resident(ng.shape), _resident(win_pk.shape, (D_MODEL // 2, OFF_GATE)), _resident(rng.shape),
                  _resident(cw.shape), _resident(cbias.shape)],
        out_specs=[tok, tok, tok,
                   pl.BlockSpec((None, HEADS, HEAD_DIM, HEAD_DIM), lambda b, c: (b, 0, 0, 0)),
                   pl.BlockSpec((None, CONV_WIDTH - 1, D_MODEL), lambda b, c: (b, 0, 0))],
        out_shape=[act, act, act,
                   jax.ShapeDtypeStruct((nb, HEADS, HEAD_DIM, HEAD_DIM), F32),
                   jax.ShapeDtypeStruct((nb, CONV_WIDTH - 1, D_MODEL), F32)],
        scratch_shapes=[pltpu.VMEM((SUBLANES + tile, D_MODEL), F32),
                        pltpu.VMEM((HEADS, RET_CHUNK, RET_CHUNK), F32),
                        pltpu.VMEM((HEADS, RET_CHUNK, ROT_HALF), F32),
                        pltpu.VMEM((HEADS, RET_CHUNK, ROT_HALF), F32)],
        compiler_params=pltpu.CompilerParams(dimension_semantics=("arbitrary", "arbitrary"),
                                             vmem_limit_bytes=VMEM_LIMIT_BYTES),
        name="prompt_branches",
    )(dec, x, cos, sin, mk_bf, mv_bf, ng, win_pk, rng, cw, cbias)


def _prep_body(x_ref, ng_ref, w_ref, mem_ref, mg_ref, wkv_ref,
               z_ref, wpk_ref, k_ref, v_ref, kb_ref, vb_ref, h_s, wkv_s, *, n_mem):
    n = pl.program_id(0)

    @pl.when(n == 0)
    def _():
        h_s[...] = _bf(_rms(x_ref[...], ng_ref[...]))
        wkv_s[...] = _bf(wkv_ref[...])

    w = _bf(w_ref[...])
    wpk_ref[...] = _pack_rows(w)

    @pl.when(n < OFF_GATE // w_ref.shape[1])
    def _():
        z_ref[...] = _dot(h_s[...], w)

    @pl.when(n < n_mem)
    def _():
        kv = _dot(_bf(_rms(mem_ref[...], mg_ref[...])), wkv_s[...])
        k = kv[:, :D_MODEL]
        v = kv[:, D_MODEL:]
        for hd in range(HEADS):
            lo = hd * HEAD_DIM
            k_ref[:, hd, :] = k[:, lo:lo + HEAD_DIM]
            v_ref[:, hd, :] = v[:, lo:lo + HEAD_DIM]
        kb_ref[...] = _bf(k)
        vb_ref[...] = _bf(v)


def _prep(x2d, ng, w_in, mem, mem_g, w_mem_kv):
    ntok = x2d.shape[0]
    n_mem = mem.shape[0]
    blk = D_MODEL
    steps = IN_TOTAL // blk
    z_blocks = OFF_GATE // blk
    assert n_mem <= steps

    def mem_blk(*tail):
        zeros = (0,) * len(tail)
        return pl.BlockSpec((None,) + tail, lambda n: (jnp.minimum(n, n_mem - 1),) + zeros)

    heads_shape = jax.ShapeDtypeStruct((n_mem, MEM_LEN, HEADS, HEAD_DIM), F32)
    return pl.pallas_call(
        functools.partial(_prep_body, n_mem=n_mem),
        grid=(steps,),
        in_specs=[_resident(x2d.shape), _resident(ng.shape),
                  pl.BlockSpec((D_MODEL, blk), lambda n: (0, n)),
                  mem_blk(MEM_LEN, D_MODEL), _resident(mem_g.shape), _resident(w_mem_kv.shape)],
        out_specs=[pl.BlockSpec((ntok, blk), lambda n: (0, jnp.minimum(n, z_blocks - 1))),
                   pl.BlockSpec((D_MODEL // 2, blk), lambda n: (0, n)),
                   mem_blk(MEM_LEN, HEADS, HEAD_DIM), mem_blk(MEM_LEN, HEADS, HEAD_DIM),
                   mem_blk(MEM_LEN, D_MODEL), mem_blk(MEM_LEN, D_MODEL)],
        out_shape=[jax.ShapeDtypeStruct((ntok, OFF_GATE), F32),
                   jax.ShapeDtypeStruct((D_MODEL // 2, IN_TOTAL), jnp.int32),
                   heads_shape, heads_shape,
                   jax.ShapeDtypeStruct(mem.shape, BF16), jax.ShapeDtypeStruct(mem.shape, BF16)],
        scratch_shapes=[pltpu.VMEM(x2d.shape, BF16), pltpu.VMEM(w_mem_kv.shape, BF16)],
        compiler_params=pltpu.CompilerParams(dimension_semantics=("arbitrary",),
                                             vmem_limit_bytes=VMEM_LIMIT_BYTES),
        name="prep",
    )(x2d, ng, w_in, mem, mem_g, w_mem_kv)


def _tail_stages(x_ref, acts, ng_ref, wg_refs, bg_ref, wro_ref, wco_ref, wmo_ref, wout_ref, fg_ref, y_ref):
    live = {}
    w_refs = (wro_ref, wco_ref, wmo_ref)

    def gate(branch):
        def run(between=()):
            if branch == 0:
                live["h"] = _bf(_rms(x_ref[...], ng_ref[...]))
            nblk = max(len(between), 1)
            width = D_MODEL // nblk
            parts = []
            for i in range(nblk):
                cols = slice(i * width, (i + 1) * width)
                logits = _dot(live["h"], _unpack_rows(wg_refs[branch][:, cols]))
                parts.append(jax.nn.sigmoid(logits + bg_ref[:, branch * D_MODEL + i * width:
                                                            branch * D_MODEL + (i + 1) * width]))
                if between:
                    between[i]()
            live["gate"] = jnp.concatenate(parts, axis=-1)
        return run

    def project(branch):
        def run(between=()):
            nblk = max(len(between), 1)
            width = D_MODEL // nblk
            parts = []
            for i in range(nblk):
                cols = slice(i * width, (i + 1) * width)
                parts.append(live["gate"][:, cols] * _dot(acts[branch](), _unpack_rows(w_refs[branch][:, cols])))
                if between:
                    between[i]()
            y_branch = jnp.concatenate(parts, axis=-1)
            live["merged"] = y_branch if branch == 0 else live["merged"] + y_branch
        return run

    def out_project():
        live["y"] = x_ref[...] + _dot(_bf(live["merged"]), _unpack_rows(wout_ref[...]))

    def finish():
        y_ref[...] = _rms(live["y"], fg_ref[...])

    stages = []
    for branch in range(N_BRANCH):
        stages += [gate(branch), project(branch)]
    return stages + [out_project, finish]


def _sample_stages(z_ref, cos_ref, sin_ref, state_ref, cstate_ref, mk_ref, mv_ref, rng_ref, cw_ref, cbias_ref,
                   store_act, nstate_ref, nconv_ref, cbuf, inner_s, qdec_s, kdec_s, dec_ref):
    group, seq, _ = z_ref.shape
    pairs = [(g, hd) for g in range(group) for hd in range(HEADS)]
    live = {}

    def col(g, off, width=HEAD_DIM):
        return z_ref[g, :, off:off + width]

    def ret_inputs():
        cos = cos_ref[...]
        sin = sin_ref[...]
        live["q"] = {p: _bf(_rope(col(p[0], OFF_RQ + p[1] * HEAD_DIM), cos, sin)) for p in pairs}
        live["k"] = {p: _rope(col(p[0], OFF_RK + p[1] * HEAD_DIM), cos, sin) * QK_SCALE for p in pairs}
        live["v"] = {p: _bf(col(p[0], OFF_RV + p[1] * HEAD_DIM)) for p in pairs}

    def ret_products():
        live["scores"], live["cross"] = {}, {}
        for p in pairs:
            g, hd = p
            state = state_ref[g, hd]
            k = live["k"][p]
            live["scores"][p] = _bf(_dot_nt(live["q"][p], _bf(k)) * inner_s[hd])
            live["cross"][p] = _dot(live["q"][p], _bf(state))
            nstate_ref[g, hd] = (state * dec_ref[HEADS + hd]
                                 + _dot_tn(_bf(k * _lanes_twice(kdec_s[hd])), live["v"][p]))

    def ret_outputs():
        for p in pairs:
            g, hd = p
            lo = hd * HEAD_DIM
            o = _dot(live["scores"][p], live["v"][p]) + live["cross"][p] * _lanes_twice(qdec_s[hd])
            o = _rms(o, rng_ref[:, lo:lo + HEAD_DIM])
            store_act(0, g, slice(lo, lo + HEAD_DIM), o * _silu(col(g, OFF_RG + lo)))

    def conv():
        for g in range(group):
            base = g * (SUBLANES + seq)
            cbuf[base:base + SUBLANES, :] = jnp.zeros((SUBLANES, D_MODEL), F32)
            cbuf[base + SUBLANES - (CONV_WIDTH - 1):base + SUBLANES, :] = cstate_ref[g]
            pre, a_conv = _conv_branch(col(g, OFF_CU, D_MODEL), col(g, OFF_CB, D_MODEL), col(g, OFF_CC, D_MODEL),
                                       col(g, OFF_CG, D_MODEL), cbuf.at[base:base + SUBLANES + seq], cw_ref,
                                       cbias_ref, seq, slice(0, D_MODEL))
            nconv_ref[g] = pre[seq - (CONV_WIDTH - 1):, :]
            store_act(1, g, slice(0, D_MODEL), a_conv)

    def mem_score_piece(g, j):
        def run():
            if j == 0:
                live.setdefault("ms", {})[g] = []
                live.setdefault("mq", {})[g] = _bf(
                    jnp.concatenate([col(g, OFF_MQ + hd * HEAD_DIM) for hd in range(HEADS)], axis=0))
            keys = mk_ref[g, j * MEM_PIECE:(j + 1) * MEM_PIECE].reshape(MEM_PIECE * HEADS, HEAD_DIM)
            live["ms"][g].append(_dot_nt(live["mq"][g], _bf(keys)) * QK_SCALE)
        return run

    def mem_probs():
        live["mp"] = []
        for g in range(group):
            s = jnp.concatenate(live["ms"][g], axis=-1)
            row_head = lax.broadcasted_iota(jnp.int32, s.shape, 0) // seq
            col_head = lax.broadcasted_iota(jnp.int32, s.shape, 1) % HEADS
            live["mp"].append(_bf(_softmax_rows(jnp.where(row_head == col_head, s, -jnp.inf))))

    def mem_outputs():
        for g in range(group):
            mv2d = _bf(mv_ref[g].reshape(MEM_LEN * HEADS, HEAD_DIM))
            om_all = _dot(live["mp"][g], mv2d)
            for hd in range(HEADS):
                lo = hd * HEAD_DIM
                om = om_all[hd * seq:(hd + 1) * seq, :]
                store_act(2, g, slice(lo, lo + HEAD_DIM), om * _silu(col(g, OFF_MG + lo)))

    pieces = [[mem_score_piece(g, j) for j in range(MEM_LEN // MEM_PIECE)] for g in range(group)]
    assert group == 2
    return [(ret_inputs, ()), (ret_products, ()), (None, pieces[0]), (ret_outputs, pieces[1]),
            (mem_probs, ()), (conv, ()), (mem_outputs, ())]


def _tail_with_sample_body(x_ref, aret_ref, aconv_ref, amem_ref, ng_ref, wg0_ref, wg1_ref, wg2_ref, bg_ref,
                           wro_ref, wco_ref, wmo_ref, wout_ref, fg_ref,
                           dec_ref, z_ref, cos_ref, sin_ref, state_ref, cstate_ref, mk_ref, mv_ref, rng_ref,
                           cw_ref, cbias_ref, xs_ref,
                           y_ref, nstate_ref, nconv_ref, ys_ref,
                           cbuf, inner_s, qdec_s, kdec_s, sact_s):
    i = pl.program_id(0)
    group, seq, _ = z_ref.shape
    steps_per_tile = xs_ref.shape[0] // (group * seq)
    slot = lax.rem(i, steps_per_tile)

    @pl.when(i == 0)
    def _():
        _fill_decay(dec_ref, inner_s, qdec_s, kdec_s, seq)

    def store_act(branch, g, cols, value):
        row = pl.multiple_of((slot * group + g) * seq, seq)
        sact_s[branch, pl.ds(row, seq), cols] = value

    weights = ((wg0_ref, wg1_ref, wg2_ref), bg_ref, wro_ref, wco_ref, wmo_ref, wout_ref, fg_ref)
    prompt_acts = [lambda ref=ref: ref[...] for ref in (aret_ref, aconv_ref, amem_ref)]
    tail = _tail_stages(x_ref, prompt_acts, ng_ref, *weights, y_ref)
    sample = _sample_stages(z_ref, cos_ref, sin_ref, state_ref, cstate_ref, mk_ref, mv_ref, rng_ref, cw_ref,
                            cbias_ref, store_act, nstate_ref, nconv_ref, cbuf, inner_s, qdec_s, kdec_s, dec_ref)
    assert len(sample) == len(tail) - 1
    for j, tail_stage in enumerate(tail):
        before, between = sample[j] if j < len(sample) - 1 else sample[-1] if j == len(tail) - 1 else (None, ())
        if before is not None:
            before()
        if between:
            tail_stage(between=between)
        else:
            tail_stage()

    @pl.when(slot == steps_per_tile - 1)
    def _():
        sample_acts = [lambda branch=branch: _bf(sact_s[branch]) for branch in range(N_BRANCH)]
        for stage in _tail_stages(xs_ref, sample_acts, ng_ref, *weights, ys_ref):
            stage()


def _tail_specs(tile, ng, bg, wro, wco, wmo, wout, fg):
    tok = pl.BlockSpec((tile, D_MODEL), lambda i: (i, 0))

    def gate_cols(branch):
        blk = OFF_GATE // D_MODEL + branch
        return pl.BlockSpec((D_MODEL // 2, D_MODEL), lambda i: (0, blk), pipeline_mode=pl.Buffered(1))

    in_specs = [tok, tok, tok, tok,
                _resident(ng.shape), gate_cols(0), gate_cols(1), gate_cols(2), _resident(bg.shape),
                _resident(wro.shape), _resident(wco.shape), _resident(wmo.shape), _resident(wout.shape),
                _resident(fg.shape)]
    return tok, in_specs


def _tail_with_sample(x2d, aret, aconv, amem, ng, win_pk, bg, wro, wco, wmo, wout, fg,
                      dec, z, cos, sin, state, cstate, mk, mv, rng, cw, cbias, xs2d):
    steps = x2d.shape[0] // TAIL_TILE
    nb, seq, _ = z.shape
    grp = nb // steps
    assert grp * steps == nb and TAIL_TILE % (grp * seq) == 0
    steps_per_tile = TAIL_TILE // (grp * seq)
    tok, in_specs = _tail_specs(TAIL_TILE, ng, bg, wro, wco, wmo, wout, fg)
    sample_tok = pl.BlockSpec((TAIL_TILE, D_MODEL), lambda i: (i // steps_per_tile, 0))

    def per_req(*tail):
        zeros = (0,) * len(tail)
        return pl.BlockSpec((grp,) + tail, lambda i: (i,) + zeros)

    def const(shape):
        zeros = (0,) * len(shape)
        return pl.BlockSpec(shape, lambda i: zeros)

    conv_state = pl.BlockSpec((None, grp, CONV_WIDTH - 1, D_MODEL), lambda i: (0, i, 0, 0))

    return pl.pallas_call(
        _tail_with_sample_body,
        grid=(steps,),
        in_specs=in_specs + [pl.BlockSpec(memory_space=pltpu.SMEM),
                             per_req(seq, OFF_GATE), const(cos.shape), const(sin.shape),
                             per_req(HEADS, HEAD_DIM, HEAD_DIM), conv_state,
                             per_req(MEM_LEN, HEADS, HEAD_DIM), per_req(MEM_LEN, HEADS, HEAD_DIM),
                             const(rng.shape), const(cw.shape), const(cbias.shape), sample_tok],
        out_specs=[tok, per_req(HEADS, HEAD_DIM, HEAD_DIM), conv_state, sample_tok],
        out_shape=[jax.ShapeDtypeStruct(x2d.shape, F32),
                   jax.ShapeDtypeStruct(state.shape, F32),
                   jax.ShapeDtypeStruct(cstate.shape, F32),
                   jax.ShapeDtypeStruct(xs2d.shape, F32)],
        scratch_shapes=[pltpu.VMEM((grp * (SUBLANES + seq), D_MODEL), F32),
                        pltpu.VMEM((HEADS, seq, seq), F32),
                        pltpu.VMEM((HEADS, seq, ROT_HALF), F32),
                        pltpu.VMEM((HEADS, seq, ROT_HALF), F32),
                        pltpu.VMEM((N_BRANCH, TAIL_TILE, D_MODEL), F32)],
        compiler_params=pltpu.CompilerParams(dimension_semantics=("arbitrary",),
                                             vmem_limit_bytes=VMEM_LIMIT_BYTES),
        name="tail_with_sample",
    )(x2d, aret, aconv, amem, ng, win_pk, win_pk, win_pk, bg, wro, wco, wmo, wout, fg,
      dec, z, cos, sin, state, cstate, mk, mv, rng, cw, cbias, xs2d)


def _rope_tables(pos):
    inv = ROPE_BASE ** (-jnp.arange(ROT_HALF, dtype=F32) / ROT_HALF)
    ang = pos.astype(F32)[:, None] * inv[None, :]
    return jnp.cos(ang), jnp.sin(ang)


def _rope_tables_range(n, stride=64):
    cos_hi, sin_hi = _rope_tables(stride * jnp.arange(n // stride, dtype=jnp.int32))
    cos_lo, sin_lo = _rope_tables(jnp.arange(stride, dtype=jnp.int32))
    cos = cos_hi[:, None, :] * cos_lo[None] - sin_hi[:, None, :] * sin_lo[None]
    sin = sin_hi[:, None, :] * cos_lo[None] + cos_hi[:, None, :] * sin_lo[None]
    return cos.reshape(n, ROT_HALF), sin.reshape(n, ROT_HALF)


def _decay_scalars(chunk):
    lg = jnp.log1p(-jnp.exp2(-5.0 - jnp.arange(HEADS, dtype=F32)))
    return jnp.concatenate([lg, jnp.exp(lg * chunk)])


def kernel(x_prompt, x_sample, state_ret, state_conv, cache_mem_k, cache_mem_v, mem_prompt, norm_g, w_in, b_gate, ret_norm_g, conv_w, conv_b, w_ret_o, w_conv_o, w_mem_o, w_out, mem_norm_g, w_mem_kv, final_norm_g):
    assert norm_g.shape[0] == 1, "single-layer trunk"
    nbp, seq_p, _ = x_prompt.shape
    nbs, seq_s, _ = x_sample.shape
    assert seq_p % PROMPT_TILE == 0 and PROMPT_TILE % RET_CHUNK == 0 and (nbp * seq_p) % TAIL_TILE == 0
    assert seq_s == SUBLANES and (nbs * seq_s) % TAIL_TILE == 0

    ng = norm_g[0][None, :]
    bg = b_gate[0][None, :]
    rng = ret_norm_g[0].reshape(1, D_MODEL)
    cw = conv_w[0]
    cbias = conv_b[0][None, :]
    wro, wco, wmo, wout = _pack_weights(w_ret_o[0], w_conv_o[0], w_mem_o[0], w_out[0])
    fg = final_norm_g[None, :]

    cos_p, sin_p = _rope_tables_range(seq_p)
    cos_s, sin_s = _rope_tables(PAST_LEN + jnp.arange(seq_s, dtype=jnp.int32))

    xs2d = x_sample.reshape(nbs * seq_s, D_MODEL)
    z, win_pk, mk, mv, mk_bf, mv_bf = _prep(xs2d, ng, w_in[0], mem_prompt, mem_norm_g[0][None, :],
                                            w_mem_kv[0])
    z = z.reshape(nbs, seq_s, OFF_GATE)
    tail_weights = (ng, win_pk, bg, wro, wco, wmo, wout, fg)

    aret_p, aconv_p, amem_p, ret_p, conv_p = _prompt_branches(
        x_prompt, _decay_scalars(RET_CHUNK), cos_p, sin_p, mk_bf, mv_bf, ng, win_pk, rng, cw, cbias)

    flat_p = lambda a: a.reshape(nbp * seq_p, D_MODEL)
    y_prompt, ret_s, conv_s, y_sample = _tail_with_sample(
        flat_p(x_prompt), flat_p(aret_p), flat_p(aconv_p), flat_p(amem_p), *tail_weights,
        _decay_scalars(seq_s), z, cos_s, sin_s, state_ret[0], state_conv, cache_mem_k[0], cache_mem_v[0],
        rng, cw, cbias, xs2d)

    return (y_prompt.reshape(x_prompt.shape), y_sample.reshape(x_sample.shape),
            ret_p[None], ret_s[None], conv_p[None], conv_s, mk[None], mv[None])
```
